```python
import math
import jax, jax.numpy as jnp
from jax import lax
import numpy as np

D_MODEL = 1024
BATCH = 4
SEQ = 4096
DEPTH = 4

HEAD_DIM = 64
DIL_GROUPS = ((128, 1), (512, 4), (2048, 16))
A_HEADS_PER_GROUP = 4
A_HEADS = A_HEADS_PER_GROUP * len(DIL_GROUPS)
A_OUT = A_HEADS_PER_GROUP * HEAD_DIM
B_HEADS = 4
B_V_DIM = 2 * HEAD_DIM
B_OUT = B_HEADS * B_V_DIM
C_HEADS = 4
C_OUT = C_HEADS * HEAD_DIM
IDX_HEADS = 8
IDX_DIM = 64
TOPK_MAX = 256
D_FF = 2816
NUM_BUCKETS = 32
MAX_DISTANCE = 2048
BIAS_HEADS = A_HEADS + B_HEADS + C_HEADS
Q_BLOCK = 128
RMS_EPS = 1e-6

A_QKV_COLS = 3 * A_HEADS * HEAD_DIM
B_QK_COLS = 4 * B_HEADS * HEAD_DIM
B_V_COLS = B_HEADS * B_V_DIM
C_COLS = C_HEADS * HEAD_DIM + 2 * HEAD_DIM
IDX_COLS = IDX_HEADS * IDX_DIM + IDX_DIM + IDX_HEADS
GATE_COLS = 3 * D_MODEL
N_IN = A_QKV_COLS + B_QK_COLS + B_V_COLS + C_COLS + IDX_COLS + GATE_COLS
SPLIT_POINTS = [A_QKV_COLS,
                A_QKV_COLS + B_QK_COLS,
                A_QKV_COLS + B_QK_COLS + B_V_COLS,
                A_QKV_COLS + B_QK_COLS + B_V_COLS + C_COLS,
                A_QKV_COLS + B_QK_COLS + B_V_COLS + C_COLS + IDX_COLS]

kernel_name = "hybrid_gated_dilated_diff_dsa_macaron"


def rms_norm(x, g):
    xf = x.astype(jnp.float32)
    y = xf * lax.rsqrt(jnp.mean(xf * xf, axis=-1, keepdims=True) + RMS_EPS)
    return (y * g.astype(jnp.float32)).astype(x.dtype)


def swiglu(x, w_gate, w_up, w_down):
    return (jax.nn.silu(x @ w_gate) * (x @ w_up)) @ w_down


def rel_bucket(dist):
    n = jnp.maximum(dist, 0)
    max_exact = NUM_BUCKETS // 2
    nf = jnp.maximum(n, 1).astype(jnp.float32)
    large = max_exact + (jnp.log(nf / max_exact) / math.log(MAX_DISTANCE / max_exact)
                         * (NUM_BUCKETS - max_exact)).astype(jnp.int32)
    large = jnp.minimum(large, NUM_BUCKETS - 1)
    return jnp.where(n < max_exact, n, large)


def dilated_window_group(q, k, v, window, dilation, bias_table):
    b, t, h, dh = q.shape
    n = t // dilation
    wn = window // dilation
    nb = -(-n // wn)
    pad = nb * wn - n

    def to_sub(z):
        z = z.reshape(b, n, dilation, h, dh).transpose(0, 2, 1, 3, 4)
        return jnp.pad(z, ((0, 0), (0, 0), (0, pad), (0, 0), (0, 0)))

    qs, ks, vs = to_sub(q), to_sub(k), to_sub(v)
    qb = qs.reshape(b, dilation, nb, wn, h, dh)

    def band(z):
        zp = jnp.pad(z, ((0, 0), (0, 0), (wn, 0), (0, 0), (0, 0))).reshape(b, dilation, nb + 1, wn, h, dh)
        return jnp.concatenate([zp[:, :, :-1], zp[:, :, 1:]], axis=3)

    kb, vb = band(ks), band(vs)
    logits = jnp.einsum('brnqhe,brnkhe->brnhqk', qb, kb).astype(jnp.float32) * (dh ** -0.5)
    a = jnp.arange(wn)[:, None]
    c = jnp.arange(2 * wn)[None, :]
    sub_dist = wn + a - c
    bias = bias_table[rel_bucket(sub_dist * dilation)].astype(jnp.float32)
    logits = logits + bias.transpose(2, 0, 1)
    key_pos = (jnp.arange(nb)[:, None] - 1) * wn + c
    mask = ((sub_dist >= 0) & (sub_dist <= wn))[None] & (key_pos >= 0)[:, None, :]
    logits = jnp.where(mask[:, None], logits, -jnp.inf)
    m = jnp.max(logits, axis=-1)
    p = jnp.exp(logits - m[..., None])
    s = jnp.sum(p, axis=-1)
    o = jnp.einsum('brnhqk,brnkhe->brnqhe', p, vb.astype(jnp.float32))

    def from_sub(z):
        z = z.reshape((b, dilation, nb * wn) + z.shape[4:])[:, :, :n]
        z = jnp.moveaxis(z, 1, 2)
        return z.reshape((b, t) + z.shape[3:])

    return from_sub(o), from_sub(jnp.swapaxes(m, 3, 4)), from_sub(jnp.swapaxes(s, 3, 4))


def dilated_attention(q, k, v, bias_table):
    outs = []
    for g, (w, d) in enumerate(DIL_GROUPS):
        sl = slice(g * A_HEADS_PER_GROUP, (g + 1) * A_HEADS_PER_GROUP)
        outs.append(dilated_window_group(q[:, :, sl], k[:, :, sl], v[:, :, sl], w, d, bias_table[:, sl]))
    m_all = jnp.stack([m for _, m, _ in outs])
    scale = jnp.exp(m_all - jnp.max(m_all, axis=0))
    num = sum(scale[g][..., None] * outs[g][0] for g in range(len(DIL_GROUPS)))
    den = sum(scale[g] * outs[g][2] for g in range(len(DIL_GROUPS)))
    return num / den[..., None]


def diff_attention(qs, ks, v, lam, bias_table):
    t = qs.shape[1]
    dh = qs.shape[-1]
    outs = []
    for start in range(0, t, Q_BLOCK):
        end = start + Q_BLOCK
        logits = jnp.einsum('bqmhe,bkmhe->bmhqk', qs[:, start:end], ks[:, :end]).astype(jnp.float32) * (dh ** -0.5)
        dist = jnp.arange(start, end)[:, None] - jnp.arange(end)[None, :]
        bias = bias_table[rel_bucket(dist)].astype(jnp.float32).transpose(2, 0, 1)
        logits = jnp.where(dist >= 0, logits + bias, -jnp.inf)
        p = jax.nn.softmax(logits, axis=-1)
        w = p[:, 0] - lam * p[:, 1]
        outs.append(jnp.einsum('bhqk,bkhe->bqhe', w, v[:, :end].astype(jnp.float32)))
    return jnp.concatenate(outs, axis=1)


def dsa_attention(q, k, v, q_idx, k_idx, w_idx, bias_table):
    t = q.shape[1]
    dh = q.shape[-1]
    k_sel = min(TOPK_MAX, t // 4)
    gather = jax.vmap(lambda z, i: z[i])
    outs = []
    for start in range(0, t, Q_BLOCK):
        end = start + Q_BLOCK
        n_sel = min(k_sel, end)
        qpos = jnp.arange(start, end)
        raw = jnp.einsum('bqhe,bke->bqhk', q_idx[:, start:end], k_idx[:, :end]).astype(jnp.float32) * (IDX_DIM ** -0.5)
        score = jnp.einsum('bqh,bqhk->bqk', w_idx[:, start:end].astype(jnp.float32) * (IDX_HEADS ** -0.5),
                           jax.nn.relu(raw))
        score = jnp.where(jnp.arange(end)[None, :] <= qpos[:, None], score, -jnp.inf)
        top_val, top_idx = lax.top_k(score, n_sel)
        valid = jnp.isfinite(top_val)
        k_g = gather(k[:, :end], top_idx)
        v_g = gather(v[:, :end], top_idx)
        logits = jnp.einsum('bqhe,bqne->bhqn', q[:, start:end], k_g).astype(jnp.float32) * (dh ** -0.5)
        bias = bias_table[rel_bucket(qpos[None, :, None] - top_idx)].astype(jnp.float32)
        logits = jnp.where(valid[:, None], logits + jnp.moveaxis(bias, 3, 1), -jnp.inf)
        p = jax.nn.softmax(logits, axis=-1)
        outs.append(jnp.einsum('bhqn,bqne->bqhe', p, v_g.astype(jnp.float32)))
    return jnp.concatenate(outs, axis=1)


def token_mixer(h, w_in, qk_gain, diff_lambda, diff_out_norm, w_branch_a, w_branch_b, w_branch_c,
                w_out, rel_bias, layer_idx):
    b, t, _ = h.shape
    proj = h @ w_in
    a_qkv, b_qk, b_v, c_qkv, ix, gates = jnp.split(proj, SPLIT_POINTS, axis=-1)

    a_qkv = a_qkv.reshape(b, t, 3, A_HEADS, HEAD_DIM)
    qa = rms_norm(a_qkv[:, :, 0], qk_gain[0, 0])
    ka = rms_norm(a_qkv[:, :, 1], qk_gain[0, 1])
    o_a = dilated_attention(qa, ka, a_qkv[:, :, 2], rel_bias[:, :A_HEADS])
    o_a = o_a.reshape(b, t, A_OUT).astype(h.dtype)

    b_qk = b_qk.reshape(b, t, 4, B_HEADS, HEAD_DIM)
    qb = rms_norm(b_qk[:, :, 0:2], qk_gain[1, 0])
    kb = rms_norm(b_qk[:, :, 2:4], qk_gain[1, 1])
    lam_init = 0.8 - 0.6 * math.exp(-0.3 * layer_idx)
    lv = diff_lambda.astype(jnp.float32)
    lam = jnp.exp(jnp.sum(lv[0] * lv[1])) - jnp.exp(jnp.sum(lv[2] * lv[3])) + lam_init
    o_b = diff_attention(qb, kb, b_v.reshape(b, t, B_HEADS, B_V_DIM), lam, rel_bias[:, A_HEADS:A_HEADS + B_HEADS])
    o_b = rms_norm(o_b, diff_out_norm) * (1.0 - lam_init)
    o_b = o_b.reshape(b, t, B_OUT).astype(h.dtype)

    qc = rms_norm(c_qkv[..., :C_HEADS * HEAD_DIM].reshape(b, t, C_HEADS, HEAD_DIM), qk_gain[2, 0])
    kc = rms_norm(c_qkv[..., C_HEADS * HEAD_DIM:C_HEADS * HEAD_DIM + HEAD_DIM], qk_gain[2, 1])
    vc = c_qkv[..., C_HEADS * HEAD_DIM + HEAD_DIM:]
    q_idx = ix[..., :IDX_HEADS * IDX_DIM].reshape(b, t, IDX_HEADS, IDX_DIM)
    k_idx = ix[..., IDX_HEADS * IDX_DIM:IDX_HEADS * IDX_DIM + IDX_DIM]
    w_idx = ix[..., IDX_HEADS * IDX_DIM + IDX_DIM:]
    o_c = dsa_attention(qc, kc, vc, q_idx, k_idx, w_idx, rel_bias[:, A_HEADS + B_HEADS:])
    o_c = o_c.reshape(b, t, C_OUT).astype(h.dtype)

    g = jax.nn.sigmoid(gates.reshape(b, t, 3, D_MODEL))
    y = (g[:, :, 0] * (o_a @ w_branch_a) + g[:, :, 1] * (o_b @ w_branch_b)
         + g[:, :, 2] * (o_c @ w_branch_c))
    return y @ w_out


def setup_inputs(seed: int = 0) -> dict:
    key = jax.random.key(seed)
    ks = jax.random.split(key, 20)
    f32 = jnp.float32

    def dense(k, shape, fan_in):
        return jax.random.normal(k, shape, f32) * fan_in ** -0.5

    def gain(k, shape):
        return 1.0 + 0.05 * jax.random.normal(k, shape, f32)

    return {
        "x": jax.random.normal(ks[0], (BATCH, SEQ, D_MODEL), f32),
        "rel_bias": 0.5 * jax.random.normal(ks[1], (NUM_BUCKETS, BIAS_HEADS), f32),
        "ffn1_norm": gain(ks[2], (DEPTH, D_MODEL)),
        "ffn1_w_gate": dense(ks[3], (DEPTH, D_MODEL, D_FF), D_MODEL),
        "ffn1_w_up": dense(ks[4], (DEPTH, D_MODEL, D_FF), D_MODEL),
        "ffn1_w_down": dense(ks[5], (DEPTH, D_FF, D_MODEL), D_FF),
        "mix_norm": gain(ks[6], (DEPTH, D_MODEL)),
        "w_in": dense(ks[7], (DEPTH, D_MODEL, N_IN), D_MODEL),
        "qk_gain": gain(ks[8], (DEPTH, 3, 2, HEAD_DIM)),
        "diff_lambda": 0.1 * jax.random.normal(ks[9], (DEPTH, 4, HEAD_DIM), f32),
        "diff_out_norm": gain(ks[10], (DEPTH, B_V_DIM)),
        "w_branch_a": dense(ks[11], (DEPTH, A_OUT, D_MODEL), A_OUT),
        "w_branch_b": dense(ks[12], (DEPTH, B_OUT, D_MODEL), B_OUT),
        "w_branch_c": dense(ks[13], (DEPTH, C_OUT, D_MODEL), C_OUT),
        "w_out": dense(ks[14], (DEPTH, D_MODEL, D_MODEL), D_MODEL),
        "ffn2_norm": gain(ks[15], (DEPTH, D_MODEL)),
        "ffn2_w_gate": dense(ks[16], (DEPTH, D_MODEL, D_FF), D_MODEL),
        "ffn2_w_up": dense(ks[17], (DEPTH, D_MODEL, D_FF), D_MODEL),
        "ffn2_w_down": dense(ks[18], (DEPTH, D_FF, D_MODEL), D_FF),
    }


def reference(x, rel_bias, ffn1_norm, ffn1_w_gate, ffn1_w_up, ffn1_w_down, mix_norm, w_in, qk_gain,
              diff_lambda, diff_out_norm, w_branch_a, w_branch_b, w_branch_c, w_out,
              ffn2_norm, ffn2_w_gate, ffn2_w_up, ffn2_w_down):
    for i in range(DEPTH):
        x = x + 0.5 * swiglu(rms_norm(x, ffn1_norm[i]), ffn1_w_gate[i], ffn1_w_up[i], ffn1_w_down[i])
        x = x + token_mixer(rms_norm(x, mix_norm[i]), w_in[i], qk_gain[i], diff_lambda[i], diff_out_norm[i],
                            w_branch_a[i], w_branch_b[i], w_branch_c[i], w_out[i], rel_bias, i)
        x = x + 0.5 * swiglu(rms_norm(x, ffn2_norm[i]), ffn2_w_gate[i], ffn2_w_up[i], ffn2_w_down[i])
    return x
```

```python
import functools

import numpy as np
import jax
import jax.numpy as jnp
from jax import lax
from jax.experimental import pallas as pl
from jax.experimental.pallas import tpu as pltpu

F32 = jnp.float32
BF16 = jnp.bfloat16

HEAD_DIM = 64
DIL_GROUPS = ((128, 1), (512, 4), (2048, 16))
A_GROUP_HEADS = 4
A_HEADS = A_GROUP_HEADS * len(DIL_GROUPS)
A_OUT = A_GROUP_HEADS * HEAD_DIM
B_HEADS = 4
B_V_DIM = 2 * HEAD_DIM
B_OUT = B_HEADS * B_V_DIM
C_HEADS = 4
C_OUT = C_HEADS * HEAD_DIM
IDX_HEADS = 8
IDX_DIM = 64
TOPK_MAX = 256
NUM_BUCKETS = 32
MAX_DISTANCE = 2048
RMS_EPS = 1e-6

LANES = 128
TOKEN_TILE = 512
B_Q_TILE = 256
FFN_CHUNK = 256
VMEM_LIMIT = 58 * 1024 * 1024

NEG = -1e30
M_INIT = -1e29
BIG = 1e30
THR_ALL = -1e29
N_BISECT = 20


def _cparams(sem):
    return pltpu.CompilerParams(dimension_semantics=sem, vmem_limit_bytes=VMEM_LIMIT)


def _const_spec(shape):
    nd = len(shape)
    return pl.BlockSpec(shape, lambda *_: (0,) * nd, pipeline_mode=pl.Buffered(1))


def _rel_bucket_np(dist):
    n = np.maximum(dist, 0)
    max_exact = NUM_BUCKETS // 2
    nf = np.maximum(n, 1).astype(np.float64)
    large = max_exact + (np.log(nf / max_exact) / np.log(MAX_DISTANCE / max_exact)
                         * (NUM_BUCKETS - max_exact)).astype(np.int64)
    large = np.minimum(large, NUM_BUCKETS - 1)
    return np.where(n < max_exact, n, large)


def _far_delta():
    d = 1
    while not np.all(_rel_bucket_np(np.arange(d * LANES - LANES + 1, d * LANES + LANES)) == NUM_BUCKETS - 1):
        d += 1
    return d


FAR = _far_delta()
MASKED = FAR + 1


def _toeplitz_tables(bias_heads):
    j = np.arange(LANES)[:, None]
    i = np.arange(LANES)[None, :]
    dist = np.arange(FAR + 1)[:, None, None] * LANES + i - j
    bucket = _rel_bucket_np(dist)
    vals = jnp.take(bias_heads.astype(F32), jnp.asarray(bucket.reshape(-1), jnp.int32), axis=0)
    vals = vals.reshape(FAR + 1, LANES, LANES, -1).transpose(3, 0, 1, 2)
    vals = jnp.where(jnp.asarray(dist >= 0)[None], vals, NEG)
    masked = jnp.full((vals.shape[0], 1, LANES, LANES), NEG, F32)
    return jnp.concatenate([vals, masked], axis=1)


def _band_tables(bias_heads, dilation):
    wn = LANES
    a = np.arange(wn)[:, None]
    c = np.arange(2 * wn)[None, :]
    sub = wn + a - c
    valid = (sub >= 0) & (sub <= wn)
    bucket = _rel_bucket_np(sub * dilation)
    vals = jnp.take(bias_heads.astype(F32), jnp.asarray(bucket.reshape(-1), jnp.int32), axis=0)
    vals = vals.reshape(wn, 2 * wn, -1).transpose(2, 0, 1)
    later = jnp.where(jnp.asarray(valid)[None], vals, NEG)
    first = jnp.where(jnp.asarray(valid & (c >= wn))[None], vals, NEG)
    return jnp.stack([first, later])


def _ffn_kernel(x_ref, g_ref, wg_ref, wu_ref, wd_ref, o_ref, acc_ref):
    x = x_ref[...]
    ms = jnp.mean(x * x, axis=-1, keepdims=True)
    h = (x * lax.rsqrt(ms + RMS_EPS) * g_ref[...]).astype(BF16)
    acc_ref[...] = jnp.zeros_like(acc_ref)

    def body(c, carry):
        g = jnp.dot(h, wg_ref[c], preferred_element_type=F32)
        u = jnp.dot(h, wu_ref[c], preferred_element_type=F32)
        a = (g * jax.nn.sigmoid(g) * u).astype(BF16)
        acc_ref[...] += jnp.dot(a, wd_ref[c], preferred_element_type=F32)
        return carry

    lax.fori_loop(0, wg_ref.shape[0], body, 0)
    o_ref[...] = x + 0.5 * acc_ref[...]


def _ffn(x, gain, w_gate, w_up, w_down):
    n, d = x.shape
    f = w_gate.shape[1]
    nck = f // FFN_CHUNK
    wg = w_gate.astype(BF16).reshape(d, nck, FFN_CHUNK).transpose(1, 0, 2)
    wu = w_up.astype(BF16).reshape(d, nck, FFN_CHUNK).transpose(1, 0, 2)
    wd = w_down.astype(BF16).reshape(nck, FFN_CHUNK, d)
    tm = TOKEN_TILE
    return pl.pallas_call(
        _ffn_kernel,
        out_shape=jax.ShapeDtypeStruct((n, d), F32),
        grid=(n // tm,),
        in_specs=[pl.BlockSpec((tm, d), lambda i: (i, 0)),
                  _const_spec((1, d)),
                  _const_spec((nck, d, FFN_CHUNK)),
                  _const_spec((nck, d, FFN_CHUNK)),
                  _const_spec((nck, FFN_CHUNK, d))],
        out_specs=pl.BlockSpec((tm, d), lambda i: (i, 0)),
        scratch_shapes=[pltpu.VMEM((tm, d), F32)],
        compiler_params=_cparams(("parallel",)),
        name="ffn",
    )(x, gain.reshape(1, d).astype(F32), wg, wu, wd)


S_AQ, S_AK, S_AV = 0, 768, 1536
S_BK, S_CK, S_GATE, S_END = 2304, 2816, 2944, 6016
T_BQ, T_BV, T_CQ, T_CV, T_IQ, T_IW, T_END = 0, 512, 1024, 1280, 1344, 1856, 1872


def _proj_kernel(x_ref, g_ref, ws_ref, wt_ref, bd_ref, gs_ref, gt_ref,
                 aq_ref, ak_ref, av_ref, bk_ref, ck_ref, gate_ref,
                 bqt_ref, bvt_ref, cqt_ref, cvt_ref, iqt_ref, iwt_ref):
    tm = x_ref.shape[0]
    x = x_ref[...]
    ms = jnp.mean(x * x, axis=-1, keepdims=True)
    h = (x * lax.rsqrt(ms + RMS_EPS) * g_ref[...]).astype(BF16)
    bd = bd_ref[...]

    def dot_s(c0, c1):
        return jnp.dot(h, ws_ref[:, c0:c1], preferred_element_type=F32)

    def head_inv_rms(y):
        outs = []
        for c in range(y.shape[1] // LANES):
            sq = y[:, c * LANES:(c + 1) * LANES]
            sq = sq * sq
            hi = sq.astype(BF16)
            lo = (sq - hi.astype(F32)).astype(BF16)
            msq = (jnp.dot(hi, bd, preferred_element_type=F32)
                   + jnp.dot(lo, bd, preferred_element_type=F32))
            outs.append(lax.rsqrt(msq + RMS_EPS))
        return outs[0] if len(outs) == 1 else jnp.concatenate(outs, axis=1)

    y = dot_s(S_AQ, S_AK)
    aq_ref[...] = (y * head_inv_rms(y) * gs_ref[:, 0:768]).astype(BF16)
    y = dot_s(S_AK, S_AV)
    ak_ref[...] = (y * head_inv_rms(y) * gs_ref[:, 768:1536]).astype(BF16)
    av_ref[...] = dot_s(S_AV, S_BK).astype(BF16)
    y = dot_s(S_BK, S_CK)
    bk_ref[...] = (y * head_inv_rms(y) * gs_ref[:, 1536:2048]).astype(BF16)
    y = dot_s(S_CK, S_GATE)
    lane = lax.broadcasted_iota(jnp.int32, y.shape, 1)
    inv = jnp.where(lane < HEAD_DIM, head_inv_rms(y), 1.0)
    ck_ref[...] = (y * inv * gs_ref[:, 2048:2176]).astype(BF16)
    for c in range(3):
        y = dot_s(S_GATE + c * 1024, S_GATE + (c + 1) * 1024)
        gate_ref[:, c * 1024:(c + 1) * 1024] = jax.nn.sigmoid(y).astype(BF16)

    def dot_t(r0, r1):
        return lax.dot_general(wt_ref[r0:r1, :], h, (((1,), (1,)), ((), ())),
                               preferred_element_type=F32)

    def norm_t(y, gain):
        r = y.shape[0] // HEAD_DIM
        y3 = y.reshape(r, HEAD_DIM, tm)
        msq = jnp.mean(y3 * y3, axis=1, keepdims=True)
        return (y3 * lax.rsqrt(msq + RMS_EPS)).reshape(r * HEAD_DIM, tm) * gain

    bqt_ref[...] = norm_t(dot_t(T_BQ, T_BV), gt_ref[0:512, :]).astype(BF16)
    bvt_ref[...] = dot_t(T_BV, T_CQ).astype(BF16)
    cqt_ref[...] = norm_t(dot_t(T_CQ, T_CV), gt_ref[512:768, :]).astype(BF16)
    cvt_ref[...] = dot_t(T_CV, T_IQ).astype(BF16)
    iqt_ref[...] = dot_t(T_IQ, T_IW).astype(BF16)
    iwt_ref[...] = dot_t(T_IW, T_END) * (IDX_HEADS ** -0.5 * IDX_DIM ** -0.5)


def _proj_weights(w_in, qk_gain, tm):
    d = w_in.shape[0]
    o = 0
    a_qkv = w_in[:, o:o + 3 * A_HEADS * HEAD_DIM].reshape(d, 3, A_HEADS * HEAD_DIM)
    o += 3 * A_HEADS * HEAD_DIM
    b_qk = w_in[:, o:o + 4 * B_HEADS * HEAD_DIM].reshape(d, 4, B_HEADS, HEAD_DIM)
    o += 4 * B_HEADS * HEAD_DIM
    b_v = w_in[:, o:o + B_OUT]
    o += B_OUT
    c_q = w_in[:, o:o + C_OUT]
    c_k = w_in[:, o + C_OUT:o + C_OUT + HEAD_DIM]
    c_v = w_in[:, o + C_OUT + HEAD_DIM:o + C_OUT + 2 * HEAD_DIM]
    o += C_OUT + 2 * HEAD_DIM
    i_q = w_in[:, o:o + IDX_HEADS * IDX_DIM]
    i_k = w_in[:, o + IDX_HEADS * IDX_DIM:o + IDX_HEADS * IDX_DIM + IDX_DIM]
    i_w = w_in[:, o + IDX_HEADS * IDX_DIM + IDX_DIM:o + IDX_HEADS * IDX_DIM + IDX_DIM + IDX_HEADS]
    o += IDX_HEADS * IDX_DIM + IDX_DIM + IDX_HEADS
    gates = w_in[:, o:]
    b_k = jnp.stack([b_qk[:, 2], b_qk[:, 3]], axis=2).reshape(d, 2 * B_HEADS * HEAD_DIM)
    b_q = jnp.stack([b_qk[:, 0], b_qk[:, 1]], axis=2).reshape(d, 2 * B_HEADS * HEAD_DIM)
    w_s = jnp.concatenate([a_qkv[:, 0], a_qkv[:, 1], a_qkv[:, 2], b_k, c_k, i_k, gates], axis=1)
    w_t = jnp.concatenate([b_q, b_v, c_q, c_v, i_q, i_w, jnp.zeros((d, 8), w_in.dtype)], axis=1).T
    assert w_s.shape[1] == S_END and w_t.shape[0] == T_END
    scale = HEAD_DIM ** -0.5
    g = qk_gain.astype(F32)
    gs = jnp.concatenate([jnp.tile(g[0, 0] * scale, A_HEADS), jnp.tile(g[0, 1], A_HEADS),
                          jnp.tile(g[1, 1], 2 * B_HEADS), g[2, 1], jnp.ones((IDX_DIM,), F32)])[None]
    gt = jnp.concatenate([jnp.tile(g[1, 0] * scale, 2 * B_HEADS), jnp.tile(g[2, 0] * scale, C_HEADS)])
    gt = jnp.broadcast_to(gt[:, None], (gt.shape[0], tm))
    return w_s.astype(BF16), w_t.astype(BF16), gs, gt


def _head_block_diag():
    r = np.arange(LANES)
    return jnp.asarray((r[:, None] // HEAD_DIM == r[None, :] // HEAD_DIM) / HEAD_DIM, BF16)


def _project(x, gain, w_in, qk_gain):
    n, d = x.shape
    tm = TOKEN_TILE
    nt = n // tm
    w_s, w_t, gs, gt = _proj_weights(w_in, qk_gain, tm)
    tok = lambda c: pl.BlockSpec((tm, c), lambda i: (i, 0))
    feat = lambda r: pl.BlockSpec((None, r, tm), lambda i: (i, 0, 0))
    out_shape = [jax.ShapeDtypeStruct((n, 768), BF16)] * 3 + [
        jax.ShapeDtypeStruct((n, 512), BF16), jax.ShapeDtypeStruct((n, 128), BF16),
        jax.ShapeDtypeStruct((n, 3072), BF16),
        jax.ShapeDtypeStruct((nt, 512, tm), BF16), jax.ShapeDtypeStruct((nt, 512, tm), BF16),
        jax.ShapeDtypeStruct((nt, 256, tm), BF16), jax.ShapeDtypeStruct((nt, 64, tm), BF16),
        jax.ShapeDtypeStruct((nt, 512, tm), BF16), jax.ShapeDtypeStruct((nt, 16, tm), F32)]
    out_specs = [tok(768), tok(768), tok(768), tok(512), tok(128), tok(3072),
                 feat(512), feat(512), feat(256), feat(64), feat(512), feat(16)]
    return pl.pallas_call(
        _proj_kernel,
        out_shape=out_shape,
        grid=(nt,),
        in_specs=[tok(d), _const_spec((1, d)), _const_spec(w_s.shape), _const_spec(w_t.shape),
                  _const_spec((LANES, LANES)), _const_spec(gs.shape), _const_spec(gt.shape)],
        out_specs=out_specs,
        compiler_params=_cparams(("parallel",)),
        name="proj",
    )(x, gain.reshape(1, d).astype(F32), w_s, w_t, _head_block_diag(), gs, gt)


def _dil_kernel(q_ref, kp_ref, kc_ref, vp_ref, vc_ref, bias_ref, o_ref, m_ref, s_ref):
    nq = q_ref.shape[0] // LANES
    qi = pl.program_id(2)
    lane = lax.broadcasted_iota(jnp.int32, (LANES, A_OUT), 1) // HEAD_DIM
    for jb in range(nq):
        rows = slice(jb * LANES, (jb + 1) * LANES)
        q = q_ref[rows, :]
        if jb == 0:
            kband = jnp.concatenate([kp_ref[...], kc_ref[rows, :]], axis=0)
            vband = jnp.concatenate([vp_ref[...], vc_ref[rows, :]], axis=0)
            variant = jnp.minimum(qi, 1)
        else:
            band = slice((jb - 1) * LANES, (jb + 1) * LANES)
            kband = kc_ref[band, :]
            vband = vc_ref[band, :]
            variant = 1
        o = jnp.zeros((LANES, A_OUT), F32)
        mb = jnp.zeros((LANES, A_OUT), F32)
        sb = jnp.zeros((LANES, A_OUT), F32)
        for h in range(A_GROUP_HEADS):
            mine = lane == h
            qm = jnp.where(mine, q, jnp.zeros_like(q))
            s = lax.dot_general(qm, kband, (((1,), (1,)), ((), ())), preferred_element_type=F32)
            s = s + bias_ref[variant, h]
            m = jnp.max(s, axis=1, keepdims=True)
            p = jnp.exp(s - m)
            ssum = jnp.sum(p, axis=1, keepdims=True)
            pv = jnp.dot(p.astype(BF16), vband, preferred_element_type=F32)
            o = jnp.where(mine, pv, o)
            mb = jnp.where(mine, m, mb)
            sb = jnp.where(mine, ssum, sb)
        o_ref[rows, :] = o
        m_ref[rows, :] = mb
        s_ref[rows, :] = sb


def _dilated_group(aq, ak, av, bias, batch, t, g, dilation):
    n = t // dilation
    nblk = n // LANES
    nq = min(nblk, 4)
    qt = nq * LANES
    view = lambda z: z.reshape(batch, n, dilation * z.shape[-1])
    ng = len(DIL_GROUPS)
    cur = pl.BlockSpec((None, qt, A_OUT), lambda b, r, i: (b, i, r * ng + g))
    prev = pl.BlockSpec((None, LANES, A_OUT), lambda b, r, i: (b, jnp.maximum(i * nq - 1, 0), r * ng + g))
    out = pl.BlockSpec((None, qt, A_OUT), lambda b, r, i: (b, i, r))
    shp = jax.ShapeDtypeStruct((batch, n, dilation * A_OUT), F32)
    o, m, s = pl.pallas_call(
        _dil_kernel,
        out_shape=[shp, shp, shp],
        grid=(batch, dilation, nblk // nq),
        in_specs=[cur, prev, cur, prev, cur, _const_spec(bias.shape)],
        out_specs=[out, out, out],
        compiler_params=_cparams(("parallel", "parallel", "parallel")),
        name=f"dilated_d{dilation}",
    )(view(aq), view(ak), view(ak), view(av), view(av), bias)
    return [z.reshape(batch * t, A_OUT) for z in (o, m, s)]


def _bias_tile(tab_ref, head, qblk, kblk):
    delta = qblk - kblk
    idx = jnp.where(delta < 0, MASKED, jnp.minimum(delta, FAR))
    if head is None:
        return tab_ref[idx]
    return tab_ref[head, idx]


def _diff_kernel(qt_ref, k_ref, vt_ref, tab_ref, lam_ref, gn_ref, o_ref):
    tq = qt_ref.shape[1]
    tk = TOKEN_TILE
    qi = pl.program_id(2)
    qt = qt_ref[...]
    row = lax.broadcasted_iota(jnp.int32, qt.shape, 0)
    q1 = jnp.where(row < HEAD_DIM, qt, jnp.zeros_like(qt))
    q2 = jnp.where(row >= HEAD_DIM, qt, jnp.zeros_like(qt))
    nqb = tq // LANES
    nkb = tk // LANES

    def update(s, m, l, acc, vt):
        m_new = jnp.maximum(m, jnp.max(s, axis=0, keepdims=True))
        alpha = jnp.exp(m - m_new)
        p = jnp.exp(s - m_new)
        l = alpha * l + jnp.sum(p, axis=0, keepdims=True)
        acc = alpha * acc + jnp.dot(vt, p.astype(BF16), preferred_element_type=F32)
        return m_new, l, acc

    def body(c, carry):
        m1, l1, a1, m2, l2, a2 = carry
        k0 = pl.multiple_of(c * tk, tk)
        kc = k_ref[pl.ds(k0, tk), :]
        vt = vt_ref[c]
        bias = jnp.concatenate(
            [jnp.concatenate([_bias_tile(tab_ref, None, qi * nqb + iq, c * nkb + jk)
                              for iq in range(nqb)], axis=1) for jk in range(nkb)], axis=0)
        s1 = jnp.dot(kc, q1, preferred_element_type=F32) + bias
        m1, l1, a1 = update(s1, m1, l1, a1, vt)
        s2 = jnp.dot(kc, q2, preferred_element_type=F32) + bias
        m2, l2, a2 = update(s2, m2, l2, a2, vt)
        return m1, l1, a1, m2, l2, a2

    nch = ((qi + 1) * tq + tk - 1) // tk
    mi = jnp.full((1, tq), M_INIT, F32)
    li = jnp.zeros((1, tq), F32)
    ai = jnp.zeros((B_V_DIM, tq), F32)
    m1, l1, a1, m2, l2, a2 = lax.fori_loop(0, nch, body, (mi, li, ai, mi, li, ai))

    lv = lam_ref[...]
    lam = (jnp.exp(jnp.sum(lv[0:1] * lv[1:2], axis=1, keepdims=True))
           - jnp.exp(jnp.sum(lv[2:3] * lv[3:4], axis=1, keepdims=True)) + lv[4:5, 0:1])
    o = a1 / l1 - lam * (a2 / l2)
    ms = jnp.mean(o * o, axis=0, keepdims=True)
    o = o * lax.rsqrt(ms + RMS_EPS) * gn_ref[...]
    o_ref[...] = o.T.astype(BF16)


def _diff_attention(bqt, bk, bvt, tab, lam_rows, gn, batch, t):
    tq = B_Q_TILE
    tk = TOKEN_TILE
    per = tk // tq
    nkt = t // tk
    bqt = bqt.reshape(batch, nkt, B_HEADS * LANES, tk)
    bvt = bvt.reshape(batch, nkt, B_OUT, tk)
    bk = bk.reshape(batch, t, B_HEADS * LANES)
    return pl.pallas_call(
        _diff_kernel,
        out_shape=jax.ShapeDtypeStruct((batch, t, B_OUT), BF16),
        grid=(batch, B_HEADS, t // tq),
        in_specs=[pl.BlockSpec((None, None, LANES, tq), lambda b, h, i: (b, i // per, h, i % per)),
                  pl.BlockSpec((None, t, LANES), lambda b, h, i: (b, 0, h)),
                  pl.BlockSpec((None, nkt, B_V_DIM, tk), lambda b, h, i: (b, 0, h, 0)),
                  pl.BlockSpec((None, FAR + 2, LANES, LANES), lambda b, h, i: (h, 0, 0, 0)),
                  _const_spec(lam_rows.shape), _const_spec(gn.shape)],
        out_specs=pl.BlockSpec((None, tq, B_V_DIM), lambda b, h, i: (b, i, h)),
        compiler_params=_cparams(("parallel", "parallel", "arbitrary")),
        name="diff_attention",
    )(bqt, bk, bvt, tab, lam_rows, gn).reshape(batch * t, B_OUT)


def _dsa_kernel(iqt_ref, iwt_ref, cqt_ref, k_ref, vt_ref, tab_ref, tri_ref, o_ref, s_ref, *, k_sel):
    tk = TOKEN_TILE
    nkb = tk // LANES
    qi = pl.program_id(1)
    nch = qi // nkb + 1
    qpos = qi * LANES + lax.broadcasted_iota(jnp.int32, (1, LANES), 1)
    zeros = jnp.zeros((HEAD_DIM, LANES), BF16)
    iq = iqt_ref[...]
    w = iwt_ref[...]
    iq_pad = [jnp.concatenate([zeros, iq[h * IDX_DIM:(h + 1) * IDX_DIM]], axis=0) for h in range(IDX_HEADS)]
    cq = cqt_ref[...]
    cq_pad = [jnp.concatenate([cq[h * HEAD_DIM:(h + 1) * HEAD_DIM], zeros], axis=0) for h in range(C_HEADS)]

    def score_body(c, carry):
        mn, mx = carry
        k0 = pl.multiple_of(c * tk, tk)
        kc = k_ref[pl.ds(k0, tk), :]
        acc = jnp.zeros((tk, LANES), F32)
        for h in range(IDX_HEADS):
            raw = jnp.dot(kc, iq_pad[h], preferred_element_type=F32)
            acc = acc + w[h:h + 1, :] * jnp.maximum(raw, 0.0)
        kpos = k0 + lax.broadcasted_iota(jnp.int32, (tk, LANES), 0)
        causal = kpos <= qpos
        s_ref[pl.ds(k0, tk), :] = jnp.where(causal, acc, NEG)
        mn = jnp.minimum(mn, jnp.min(jnp.where(causal, acc, BIG), axis=0, keepdims=True))
        mx = jnp.maximum(mx, jnp.max(jnp.where(causal, acc, NEG), axis=0, keepdims=True))
        return mn, mx

    lo, hi = lax.fori_loop(0, nch, score_body,
                           (jnp.full((1, LANES), BIG, F32), jnp.full((1, LANES), NEG, F32)))

    def count(thr):
        def body(c, carry):
            gt, ge = carry
            s = s_ref[pl.ds(pl.multiple_of(c * tk, tk), tk), :]
            gt = gt + jnp.sum(jnp.where(s > thr, 1.0, 0.0), axis=0, keepdims=True)
            ge = ge + jnp.sum(jnp.where(s >= thr, 1.0, 0.0), axis=0, keepdims=True)
            return gt, ge
        z = jnp.zeros((1, LANES), F32)
        return lax.fori_loop(0, nch, body, (z, z))

    def count_gt(thr):
        def body(c, gt):
            s = s_ref[pl.ds(pl.multiple_of(c * tk, tk), tk), :]
            return gt + jnp.sum(jnp.where(s > thr, 1.0, 0.0), axis=0, keepdims=True)
        return lax.fori_loop(0, nch, body, jnp.zeros((1, LANES), F32))

    def max_below(bound, strict):
        def body(c, mx):
            s = s_ref[pl.ds(pl.multiple_of(c * tk, tk), tk), :]
            keep = (s < bound) if strict else (s <= bound)
            return jnp.maximum(mx, jnp.max(jnp.where(keep, s, NEG), axis=0, keepdims=True))
        return lax.fori_loop(0, nch, body, jnp.full((1, LANES), NEG, F32))

    kf = float(k_sel)
    need = qpos >= k_sel

    def bisect(_, carry):
        lo, hi = carry
        mid = lo + (hi - lo) * 0.5
        below = count_gt(mid) < kf
        return jnp.where(below, lo, mid), jnp.where(below, mid, hi)

    lo, hi = lax.fori_loop(0, N_BISECT, bisect, (lo, hi))

    cand = max_below(hi, strict=False)
    gt, ge = count(cand)

    def walk_cond(carry):
        _, _, ge = carry
        return jnp.max(jnp.where(need & (ge < kf), 1.0, 0.0)) > 0.0

    def walk_body(carry):
        cand, gt, ge = carry
        nxt = max_below(cand, strict=True)
        cand = jnp.where(ge < kf, nxt, cand)
        gt, ge = count(cand)
        return cand, gt, ge

    cand, gt, ge = lax.while_loop(walk_cond, walk_body, (cand, gt, ge))
    thr = jnp.where(need, cand, THR_ALL)
    want_eq = jnp.where(need, kf - gt, 0.0)

    def attn_body(c, carry):
        eq_seen, ms, ls, accs = carry
        k0 = pl.multiple_of(c * tk, tk)
        s = s_ref[pl.ds(k0, tk), :]
        kc = k_ref[pl.ds(k0, tk), :]
        vt = vt_ref[c]
        eq = jnp.where(s == thr, 1.0, 0.0)
        rank = eq_seen + jnp.dot(tri_ref[...], eq.astype(BF16), preferred_element_type=F32)
        sel = jnp.where(s > thr, 1.0, jnp.where(rank <= want_eq, eq, 0.0)) > 0.5
        eq_seen = eq_seen + jnp.sum(eq, axis=0, keepdims=True)
        ms2, ls2, accs2 = [], [], []
        for h in range(C_HEADS):
            bias = jnp.concatenate([_bias_tile(tab_ref, h, qi, c * nkb + jk) for jk in range(nkb)], axis=0)
            lg = jnp.dot(kc, cq_pad[h], preferred_element_type=F32) + bias
            lg = jnp.where(sel, lg, NEG)
            m_new = jnp.maximum(ms[h], jnp.max(lg, axis=0, keepdims=True))
            alpha = jnp.exp(ms[h] - m_new)
            p = jnp.exp(lg - m_new)
            ls2.append(alpha * ls[h] + jnp.sum(p, axis=0, keepdims=True))
            accs2.append(alpha * accs[h] + jnp.dot(vt, p.astype(BF16), preferred_element_type=F32))
            ms2.append(m_new)
        return eq_seen, tuple(ms2), tuple(ls2), tuple(accs2)

    z1 = jnp.zeros((1, LANES), F32)
    init = (z1, (jnp.full((1, LANES), M_INIT, F32),) * C_HEADS, (z1,) * C_HEADS,
            (jnp.zeros((HEAD_DIM, LANES), F32),) * C_HEADS)
    _, _, ls, accs = lax.fori_loop(0, nch, attn_body, init)
    ot = jnp.concatenate([accs[h] / ls[h] for h in range(C_HEADS)], axis=0)
    o_ref[...] = ot.T.astype(BF16)


def _dsa_attention(iqt, iwt, cqt, ck, cvt, tab, batch, t):
    tk = TOKEN_TILE
    per = tk // LANES
    nkt = t // tk
    k_sel = min(TOPK_MAX, t // 4)
    iqt = iqt.reshape(batch, nkt, IDX_HEADS * IDX_DIM, tk)
    iwt = iwt.reshape(batch, nkt, 16, tk)
    cqt = cqt.reshape(batch, nkt, C_OUT, tk)
    cvt = cvt.reshape(batch, nkt, HEAD_DIM, tk)
    ck = ck.reshape(batch, t, LANES)
    r = np.arange(tk)
    tri = jnp.asarray(r[:, None] >= r[None, :], BF16)
    qblock = lambda rows: pl.BlockSpec((None, None, rows, LANES), lambda b, i: (b, i // per, 0, i % per))
    return pl.pallas_call(
        functools.partial(_dsa_kernel, k_sel=k_sel),
        out_shape=jax.ShapeDtypeStruct((batch, t, C_OUT), BF16),
        grid=(batch, t // LANES),
        in_specs=[qblock(IDX_HEADS * IDX_DIM), qblock(16), qblock(C_OUT),
                  pl.BlockSpec((None, t, LANES), lambda b, i: (b, 0, 0)),
                  pl.BlockSpec((None, nkt, HEAD_DIM, tk), lambda b, i: (b, 0, 0, 0)),
                  _const_spec(tab.shape), _const_spec(tri.shape)],
        out_specs=pl.BlockSpec((None, LANES, C_OUT), lambda b, i: (b, i, 0)),
        scratch_shapes=[pltpu.VMEM((t, LANES), F32)],
        compiler_params=_cparams(("parallel", "arbitrary")),
        name="dsa_attention",
    )(iqt, iwt, cqt, ck, cvt, tab, tri).reshape(batch * t, C_OUT)


def _merge_kernel(x_ref, o0_ref, m0_ref, s0_ref, o1_ref, m1_ref, s1_ref, o2_ref, m2_ref, s2_ref,
                  ob_ref, oc_ref, gate_ref, wa_ref, wb_ref, wc_ref, wo_ref, out_ref):
    d = x_ref.shape[1]
    m0, m1, m2 = m0_ref[...], m1_ref[...], m2_ref[...]
    mx = jnp.maximum(jnp.maximum(m0, m1), m2)
    e0, e1, e2 = jnp.exp(m0 - mx), jnp.exp(m1 - mx), jnp.exp(m2 - mx)
    num = e0 * o0_ref[...] + e1 * o1_ref[...] + e2 * o2_ref[...]
    den = e0 * s0_ref[...] + e1 * s1_ref[...] + e2 * s2_ref[...]
    oa = (num / den).astype(BF16)
    y = gate_ref[:, 0:d].astype(F32) * jnp.dot(oa, wa_ref[...], preferred_element_type=F32)
    y = y + gate_ref[:, d:2 * d].astype(F32) * jnp.dot(ob_ref[...], wb_ref[...], preferred_element_type=F32)
    y = y + gate_ref[:, 2 * d:3 * d].astype(F32) * jnp.dot(oc_ref[...], wc_ref[...], preferred_element_type=F32)
    out_ref[...] = x_ref[...] + jnp.dot(y.astype(BF16), wo_ref[...], preferred_element_type=F32)


def _merge(x, a_parts, ob, oc, gates, wa, wb, wc, wo):
    n, d = x.shape
    tm = TOKEN_TILE
    tok = lambda c: pl.BlockSpec((tm, c), lambda i: (i, 0))
    ws = [w.astype(BF16) for w in (wa, wb, wc, wo)]
    return pl.pallas_call(
        _merge_kernel,
        out_shape=jax.ShapeDtypeStruct((n, d), F32),
        grid=(n // tm,),
        in_specs=[tok(d)] + [tok(A_OUT)] * 9 + [tok(B_OUT), tok(C_OUT), tok(3 * d)]
                 + [_const_spec(w.shape) for w in ws],
        out_specs=tok(d),
        compiler_params=_cparams(("parallel",)),
        name="merge",
    )(x, *a_parts, ob, oc, gates, *ws)


def _token_mixer(x, batch, t, layer, mix_norm, w_in, qk_gain, diff_lambda, diff_out_norm,
                 w_branch_a, w_branch_b, w_branch_c, w_out, band_tabs, tab_b, tab_c):
    aq, ak, av, bk, ck, gates, bqt, bvt, cqt, cvt, iqt, iwt = _project(x, mix_norm, w_in, qk_gain)
    a_parts = []
    for g, (_, dilation) in enumerate(DIL_GROUPS):
        a_parts += _dilated_group(aq, ak, av, band_tabs[g], batch, t, g, dilation)
    lam_init = 0.8 - 0.6 * np.exp(-0.3 * layer)
    lam_rows = jnp.concatenate([diff_lambda.astype(F32), jnp.full((4, HEAD_DIM), lam_init, F32)], axis=0)
    gn = jnp.broadcast_to((diff_out_norm.astype(F32) * (1.0 - lam_init))[:, None], (B_V_DIM, B_Q_TILE))
    ob = _diff_attention(bqt, bk, bvt, tab_b, lam_rows, gn, batch, t)
    oc = _dsa_attention(iqt, iwt, cqt, ck, cvt, tab_c, batch, t)
    return _merge(x, a_parts, ob, oc, gates, w_branch_a, w_branch_b, w_branch_c, w_out)


def kernel(x, rel_bias, ffn1_norm, ffn1_w_gate, ffn1_w_up, ffn1_w_down, mix_norm, w_in, qk_gain,
           diff_lambda, diff_out_norm, w_branch_a, w_branch_b, w_branch_c, w_out,
           ffn2_norm, ffn2_w_gate, ffn2_w_up, ffn2_w_down):
    batch, t, d = x.shape
    depth = w_in.shape[0]
    assert t % (DIL_GROUPS[-1][1] * LANES) == 0 and t % TOKEN_TILE == 0
    band_tabs = [_band_tables(rel_bias[:, g * A_GROUP_HEADS:(g + 1) * A_GROUP_HEADS], dil)
                 for g, (_, dil) in enumerate(DIL_GROUPS)]
    tab_b = _toeplitz_tables(rel_bias[:, A_HEADS:A_HEADS + B_HEADS])
    tab_c = _toeplitz_tables(rel_bias[:, A_HEADS + B_HEADS:])
    h = x.reshape(batch * t, d).astype(F32)
    for i in range(depth):
        h = _ffn(h, ffn1_norm[i], ffn1_w_gate[i], ffn1_w_up[i], ffn1_w_down[i])
        h = _token_mixer(h, batch, t, i, mix_norm[i], w_in[i], qk_gain[i], diff_lambda[i], diff_out_norm[i],
                         w_branch_a[i], w_branch_b[i], w_branch_c[i], w_out[i], band_tabs, tab_b, tab_c)
        h = _ffn(h, ffn2_norm[i], ffn2_w_gate[i], ffn2_w_up[i], ffn2_w_down[i])
    return h.reshape(batch, t, d).astype(x.dtype)
```

```python
import functools

import numpy as np
import jax
import jax.numpy as jnp
from jax import lax
from jax.experimental import pallas as pl
from jax.experimental.pallas import tpu as pltpu

F32 = jnp.float32
BF16 = jnp.bfloat16

HEAD_DIM = 64
DIL_GROUPS = ((128, 1), (512, 4), (2048, 16))
A_GROUP_HEADS = 4
A_HEADS = A_GROUP_HEADS * len(DIL_GROUPS)
A_OUT = A_GROUP_HEADS * HEAD_DIM
B_HEADS = 4
B_V_DIM = 2 * HEAD_DIM
B_OUT = B_HEADS * B_V_DIM
C_HEADS = 4
C_OUT = C_HEADS * HEAD_DIM
IDX_HEADS = 8
IDX_DIM = 64
TOPK_MAX = 256
NUM_BUCKETS = 32
MAX_DISTANCE = 2048
RMS_EPS = 1e-6

LANES = 128
TOKEN_TILE = 512
B_Q_TILE = 256
FFN_CHUNK = 256
VMEM_LIMIT = 58 * 1024 * 1024

NEG = -1e30
M_INIT = -1e29
BIG = 1e30
THR_ALL = -1e29
N_BISECT = 20


def _cparams(sem):
    return pltpu.CompilerParams(dimension_semantics=sem, vmem_limit_bytes=VMEM_LIMIT)


def _const_spec(shape):
    nd = len(shape)
    return pl.BlockSpec(shape, lambda *_: (0,) * nd, pipeline_mode=pl.Buffered(1))


def _rel_bucket_np(dist):
    n = np.maximum(dist, 0)
    max_exact = NUM_BUCKETS // 2
    nf = np.maximum(n, 1).astype(np.float64)
    large = max_exact + (np.log(nf / max_exact) / np.log(MAX_DISTANCE / max_exact)
                         * (NUM_BUCKETS - max_exact)).astype(np.int64)
    large = np.minimum(large, NUM_BUCKETS - 1)
    return np.where(n < max_exact, n, large)


def _far_delta():
    d = 1
    while not np.all(_rel_bucket_np(np.arange(d * LANES - LANES + 1, d * LANES + LANES)) == NUM_BUCKETS - 1):
        d += 1
    return d


FAR = _far_delta()
MASKED = FAR + 1


def _toeplitz(w, n_rows, n_cols):
    period = n_rows + n_cols
    w = jnp.pad(w, ((0, 0), (0, period - w.shape[1])))
    m = jnp.tile(w, (1, n_rows))[:, :n_rows * (period - 1)].reshape(-1, n_rows, period - 1)
    return m[:, :, n_rows - 1:n_rows - 1 + n_cols]


def _bias_by_distance(bias_heads, dist, valid):
    vals = jnp.take(bias_heads.astype(F32), jnp.asarray(_rel_bucket_np(dist), jnp.int32), axis=0).T
    return jnp.where(jnp.asarray(valid)[None], vals, NEG)


def _toeplitz_tables(bias_heads):
    n_cols = (FAR + 1) * LANES
    dist = np.arange(LANES - 1 + n_cols) - (LANES - 1)
    tiles = _toeplitz(_bias_by_distance(bias_heads, dist, dist >= 0), LANES, n_cols)
    tiles = tiles.reshape(-1, LANES, FAR + 1, LANES).transpose(0, 2, 1, 3)
    masked = jnp.full((tiles.shape[0], 1, LANES, LANES), NEG, F32)
    return jnp.concatenate([tiles, masked], axis=1)


def _band_tables(bias_heads, dilation):
    wn = LANES
    sub = np.arange(3 * wn - 1) - (wn - 1)
    w = _bias_by_distance(bias_heads, sub * dilation, (sub >= 0) & (sub <= wn))
    later = jnp.flip(_toeplitz(w, wn, 2 * wn), axis=(1, 2))
    first = jnp.where(jnp.asarray(np.arange(2 * wn) >= wn)[None, None], later, NEG)
    return jnp.stack([first, later])


def _ffn_kernel(x_ref, g_ref, wg_ref, wu_ref, wd_ref, o_ref, acc_ref):
    x = x_ref[...]
    ms = jnp.mean(x * x, axis=-1, keepdims=True)
    h = (x * lax.rsqrt(ms + RMS_EPS) * g_ref[...]).astype(BF16)
    acc_ref[...] = jnp.zeros_like(acc_ref)

    def body(c, carry):
        g = jnp.dot(h, wg_ref[c], preferred_element_type=F32)
        u = jnp.dot(h, wu_ref[c], preferred_element_type=F32)
        a = (g * jax.nn.sigmoid(g) * u).astype(BF16)
        acc_ref[...] += jnp.dot(a, wd_ref[c], preferred_element_type=F32)
        return carry

    lax.fori_loop(0, wg_ref.shape[0], body, 0)
    o_ref[...] = x + 0.5 * acc_ref[...]


def _ffn(x, gain, w_gate, w_up, w_down):
    n, d = x.shape
    f = w_gate.shape[1]
    nck = f // FFN_CHUNK
    wg = w_gate.astype(BF16).reshape(d, nck, FFN_CHUNK).transpose(1, 0, 2)
    wu = w_up.astype(BF16).reshape(d, nck, FFN_CHUNK).transpose(1, 0, 2)
    wd = w_down.astype(BF16).reshape(nck, FFN_CHUNK, d)
    tm = TOKEN_TILE
    return pl.pallas_call(
        _ffn_kernel,
        out_shape=jax.ShapeDtypeStruct((n, d), F32),
        grid=(n // tm,),
        in_specs=[pl.BlockSpec((tm, d), lambda i: (i, 0)),
                  _const_spec((1, d)),
                  _const_spec((nck, d, FFN_CHUNK)),
                  _const_spec((nck, d, FFN_CHUNK)),
                  _const_spec((nck, FFN_CHUNK, d))],
        out_specs=pl.BlockSpec((tm, d), lambda i: (i, 0)),
        scratch_shapes=[pltpu.VMEM((tm, d), F32)],
        compiler_params=_cparams(("parallel",)),
        name="ffn",
    )(x, gain.reshape(1, d).astype(F32), wg, wu, wd)


S_AQ, S_AK, S_AV = 0, 768, 1536
S_BK, S_CK, S_GATE, S_END = 2304, 2816, 2944, 6016
T_BQ, T_BV, T_CQ, T_CV, T_IQ, T_IW, T_END = 0, 512, 1024, 1280, 1344, 1856, 1872


def _proj_kernel(x_ref, g_ref, ws_ref, wt_ref, bd_ref, gs_ref, gt_ref, *refs):
    ng = len(DIL_GROUPS)
    a_refs = refs[:3 * ng]
    bk_ref, ck_ref, gate_ref, bqt_ref, bvt_ref, cqt_ref, cvt_ref, iqt_ref, iwt_ref = refs[3 * ng:-1]
    shuffle_ref = refs[-1]
    tm = x_ref.shape[0]

    def store_by_residue(y, which):
        for g, (_, dil) in enumerate(DIL_GROUPS):
            out = a_refs[which * ng + g]
            part = y[:, g * A_OUT:(g + 1) * A_OUT]
            if dil == 1:
                out[0] = part.astype(BF16)
            else:
                for half in range(A_OUT // LANES):
                    shuffle_ref[half] = part[:, half * LANES:(half + 1) * LANES]
                for r in range(dil):
                    out[r] = jnp.concatenate(
                        [shuffle_ref[half, pl.ds(r, tm // dil, stride=dil), :] for half in range(A_OUT // LANES)],
                        axis=1).astype(BF16)

    x = x_ref[...]
    ms = jnp.mean(x * x, axis=-1, keepdims=True)
    h = (x * lax.rsqrt(ms + RMS_EPS) * g_ref[...]).astype(BF16)
    bd = bd_ref[...]

    def dot_s(c0, c1):
        return jnp.dot(h, ws_ref[:, c0:c1], preferred_element_type=F32)

    def head_inv_rms(y):
        outs = []
        for c in range(y.shape[1] // LANES):
            sq = y[:, c * LANES:(c + 1) * LANES]
            sq = sq * sq
            hi = sq.astype(BF16)
            lo = (sq - hi.astype(F32)).astype(BF16)
            msq = (jnp.dot(hi, bd, preferred_element_type=F32)
                   + jnp.dot(lo, bd, preferred_element_type=F32))
            outs.append(lax.rsqrt(msq + RMS_EPS))
        return outs[0] if len(outs) == 1 else jnp.concatenate(outs, axis=1)

    y = dot_s(S_AQ, S_AK)
    store_by_residue(y * head_inv_rms(y) * gs_ref[:, 0:768], 0)
    y = dot_s(S_AK, S_AV)
    store_by_residue(y * head_inv_rms(y) * gs_ref[:, 768:1536], 1)
    store_by_residue(dot_s(S_AV, S_BK), 2)
    y = dot_s(S_BK, S_CK)
    bk_ref[...] = (y * head_inv_rms(y) * gs_ref[:, 1536:2048]).astype(BF16)
    y = dot_s(S_CK, S_GATE)
    lane = lax.broadcasted_iota(jnp.int32, y.shape, 1)
    inv = jnp.where(lane < HEAD_DIM, head_inv_rms(y), 1.0)
    ck_ref[...] = (y * inv * gs_ref[:, 2048:2176]).astype(BF16)
    for c in range(3):
        y = dot_s(S_GATE + c * 1024, S_GATE + (c + 1) * 1024)
        gate_ref[:, c * 1024:(c + 1) * 1024] = jax.nn.sigmoid(y).astype(BF16)

    def dot_t(r0, r1):
        return lax.dot_general(wt_ref[r0:r1, :], h, (((1,), (1,)), ((), ())),
                               preferred_element_type=F32)

    def norm_t(y, gain):
        r = y.shape[0] // HEAD_DIM
        y3 = y.reshape(r, HEAD_DIM, tm)
        msq = jnp.mean(y3 * y3, axis=1, keepdims=True)
        return (y3 * lax.rsqrt(msq + RMS_EPS)).reshape(r * HEAD_DIM, tm) * gain

    bqt_ref[...] = norm_t(dot_t(T_BQ, T_BV), gt_ref[0:512, :]).astype(BF16)
    bvt_ref[...] = dot_t(T_BV, T_CQ).astype(BF16)
    cqt_ref[...] = norm_t(dot_t(T_CQ, T_CV), gt_ref[512:768, :]).astype(BF16)
    cvt_ref[...] = dot_t(T_CV, T_IQ).astype(BF16)
    iqt_ref[...] = dot_t(T_IQ, T_IW).astype(BF16)
    iwt_ref[...] = dot_t(T_IW, T_END) * (IDX_HEADS ** -0.5 * IDX_DIM ** -0.5)


def _proj_weights(w_in, qk_gain, tm):
    d = w_in.shape[0]
    o = 0
    a_qkv = w_in[:, o:o + 3 * A_HEADS * HEAD_DIM].reshape(d, 3, A_HEADS * HEAD_DIM)
    o += 3 * A_HEADS * HEAD_DIM
    b_qk = w_in[:, o:o + 4 * B_HEADS * HEAD_DIM].reshape(d, 4, B_HEADS, HEAD_DIM)
    o += 4 * B_HEADS * HEAD_DIM
    b_v = w_in[:, o:o + B_OUT]
    o += B_OUT
    c_q = w_in[:, o:o + C_OUT]
    c_k = w_in[:, o + C_OUT:o + C_OUT + HEAD_DIM]
    c_v = w_in[:, o + C_OUT + HEAD_DIM:o + C_OUT + 2 * HEAD_DIM]
    o += C_OUT + 2 * HEAD_DIM
    i_q = w_in[:, o:o + IDX_HEADS * IDX_DIM]
    i_k = w_in[:, o + IDX_HEADS * IDX_DIM:o + IDX_HEADS * IDX_DIM + IDX_DIM]
    i_w = w_in[:, o + IDX_HEADS * IDX_DIM + IDX_DIM:o + IDX_HEADS * IDX_DIM + IDX_DIM + IDX_HEADS]
    o += IDX_HEADS * IDX_DIM + IDX_DIM + IDX_HEADS
    gates = w_in[:, o:]
    b_k = jnp.stack([b_qk[:, 2], b_qk[:, 3]], axis=2).reshape(d, 2 * B_HEADS * HEAD_DIM)
    b_q = jnp.stack([b_qk[:, 0], b_qk[:, 1]], axis=2).reshape(d, 2 * B_HEADS * HEAD_DIM)
    w_s = jnp.concatenate([a_qkv[:, 0], a_qkv[:, 1], a_qkv[:, 2], b_k, c_k, i_k, gates], axis=1)
    w_t = jnp.concatenate([b_q, b_v, c_q, c_v, i_q, i_w, jnp.zeros((d, 8), w_in.dtype)], axis=1).T
    assert w_s.shape[1] == S_END and w_t.shape[0] == T_END
    scale = HEAD_DIM ** -0.5
    g = qk_gain.astype(F32)
    gs = jnp.concatenate([jnp.tile(g[0, 0] * scale, A_HEADS), jnp.tile(g[0, 1], A_HEADS),
                          jnp.tile(g[1, 1], 2 * B_HEADS), g[2, 1], jnp.ones((IDX_DIM,), F32)])[None]
    gt = jnp.concatenate([jnp.tile(g[1, 0] * scale, 2 * B_HEADS), jnp.tile(g[2, 0] * scale, C_HEADS)])
    gt = jnp.broadcast_to(gt[:, None], (gt.shape[0], tm))
    return w_s.astype(BF16), w_t.astype(BF16), gs, gt


def _head_block_diag():
    r = np.arange(LANES)
    return jnp.asarray((r[:, None] // HEAD_DIM == r[None, :] // HEAD_DIM) / HEAD_DIM, BF16)


def _project(x, gain, w_in, qk_gain, batch, t):
    n, d = x.shape
    tm = TOKEN_TILE
    nt = n // tm
    per_batch = t // tm
    w_s, w_t, gs, gt = _proj_weights(w_in, qk_gain, tm)
    tok = lambda c: pl.BlockSpec((tm, c), lambda i: (i, 0))
    feat = lambda r: pl.BlockSpec((None, r, tm), lambda i: (i, 0, 0))
    a_shapes, a_specs = [], []
    for _ in range(3):
        for _, dil in DIL_GROUPS:
            a_shapes.append(jax.ShapeDtypeStruct((batch, dil, t // dil, A_OUT), BF16))
            a_specs.append(pl.BlockSpec((None, dil, tm // dil, A_OUT),
                                        lambda i: (i // per_batch, 0, i % per_batch, 0)))
    out_shape = a_shapes + [
        jax.ShapeDtypeStruct((n, 512), BF16), jax.ShapeDtypeStruct((n, 128), BF16),
        jax.ShapeDtypeStruct((n, 3072), BF16),
        jax.ShapeDtypeStruct((nt, 512, tm), BF16), jax.ShapeDtypeStruct((nt, 512, tm), BF16),
        jax.ShapeDtypeStruct((nt, 256, tm), BF16), jax.ShapeDtypeStruct((nt, 64, tm), BF16),
        jax.ShapeDtypeStruct((nt, 512, tm), BF16), jax.ShapeDtypeStruct((nt, 16, tm), F32)]
    out_specs = a_specs + [tok(512), tok(128), tok(3072),
                           feat(512), feat(512), feat(256), feat(64), feat(512), feat(16)]
    return pl.pallas_call(
        _proj_kernel,
        out_shape=out_shape,
        grid=(nt,),
        in_specs=[tok(d), _const_spec((1, d)), _const_spec(w_s.shape), _const_spec(w_t.shape),
                  _const_spec((LANES, LANES)), _const_spec(gs.shape), _const_spec(gt.shape)],
        out_specs=out_specs,
        scratch_shapes=[pltpu.VMEM((A_OUT // LANES, tm, LANES), F32)],
        compiler_params=_cparams(("parallel",)),
        name="proj",
    )(x, gain.reshape(1, d).astype(F32), w_s, w_t, _head_block_diag(), gs, gt)


def _dil_kernel(q_ref, kp_ref, kc_ref, vp_ref, vc_ref, bias_ref, o_ref, lse_ref):
    nq = q_ref.shape[0] // LANES
    qi = pl.program_id(2)
    lane = lax.broadcasted_iota(jnp.int32, (LANES, A_OUT), 1) // HEAD_DIM
    for jb in range(nq):
        rows = slice(jb * LANES, (jb + 1) * LANES)
        q = q_ref[rows, :]
        if jb == 0:
            kband = jnp.concatenate([kp_ref[...], kc_ref[rows, :]], axis=0)
            vband = jnp.concatenate([vp_ref[...], vc_ref[rows, :]], axis=0)
            variant = jnp.minimum(qi, 1)
        else:
            band = slice((jb - 1) * LANES, (jb + 1) * LANES)
            kband = kc_ref[band, :]
            vband = vc_ref[band, :]
            variant = 1
        o = jnp.zeros((LANES, A_OUT), F32)
        lse = jnp.zeros((LANES, A_OUT), F32)
        for h in range(A_GROUP_HEADS):
            mine = lane == h
            qm = jnp.where(mine, q, jnp.zeros_like(q))
            s = lax.dot_general(qm, kband, (((1,), (1,)), ((), ())), preferred_element_type=F32)
            s = s + bias_ref[variant, h]
            m = jnp.max(s, axis=1, keepdims=True)
            p = jnp.exp(s - m)
            ssum = jnp.sum(p, axis=1, keepdims=True)
            pv = jnp.dot(p.astype(BF16), vband, preferred_element_type=F32)
            o = jnp.where(mine, pv / ssum, o)
            lse = jnp.where(mine, m + jnp.log(ssum), lse)
        o_ref[rows, :] = o
        lse_ref[rows, :] = lse


def _dilated_group(aq, ak, av, bias, dilation):
    batch, _, n, _ = aq.shape
    nblk = n // LANES
    nq = min(nblk, 4)
    qt = nq * LANES
    cur = pl.BlockSpec((None, None, qt, A_OUT), lambda b, r, i: (b, r, i, 0))
    prev = pl.BlockSpec((None, None, LANES, A_OUT), lambda b, r, i: (b, r, jnp.maximum(i * nq - 1, 0), 0))
    shp = jax.ShapeDtypeStruct((batch, dilation, n, A_OUT), F32)
    return pl.pallas_call(
        _dil_kernel,
        out_shape=[shp, shp],
        grid=(batch, dilation, nblk // nq),
        in_specs=[cur, prev, cur, prev, cur, _const_spec(bias.shape)],
        out_specs=[cur, cur],
        compiler_params=_cparams(("parallel", "parallel", "parallel")),
        name=f"dilated_d{dilation}",
    )(aq, ak, ak, av, av, bias)


def _fold_rows(x, op):
    r, c = x.shape
    return op(x.reshape(r // 64, 64, c), axis=0) if r > 64 else x


def _reduce_rows(x, op):
    x = _fold_rows(x, op)
    x = op(x.reshape(8, 8, x.shape[1]), axis=0)
    return op(x, axis=0, keepdims=True)


def _bias_tile(tab_ref, head, qblk, kblk):
    delta = qblk - kblk
    idx = jnp.where(delta < 0, MASKED, jnp.minimum(delta, FAR))
    if head is None:
        return tab_ref[idx]
    return tab_ref[head, idx]


def _diff_kernel(qt_ref, k_ref, vt_ref, tab_ref, lam_ref, gn_ref, o_ref):
    tq = qt_ref.shape[1]
    tk = TOKEN_TILE
    qi = pl.program_id(2)
    qt = qt_ref[...]
    row = lax.broadcasted_iota(jnp.int32, qt.shape, 0)
    q12 = jnp.concatenate([jnp.where(row < HEAD_DIM, qt, jnp.zeros_like(qt)),
                           jnp.where(row >= HEAD_DIM, qt, jnp.zeros_like(qt))], axis=1)
    nqb = tq // LANES
    nkb = tk // LANES

    def body(c, carry):
        m, l, acc = carry
        k0 = pl.multiple_of(c * tk, tk)
        kc = k_ref[pl.ds(k0, tk), :]
        bias = jnp.concatenate(
            [jnp.concatenate([_bias_tile(tab_ref, None, qi * nqb + iq, c * nkb + jk)
                              for iq in range(nqb)] * 2, axis=1) for jk in range(nkb)], axis=0)
        s = jnp.dot(kc, q12, preferred_element_type=F32) + bias
        m_new = jnp.maximum(m, _reduce_rows(s, jnp.max))
        alpha = jnp.exp(m - m_new)
        p = jnp.exp(s - m_new)
        l = alpha * l + _reduce_rows(p, jnp.sum)
        acc = alpha * acc + jnp.dot(vt_ref[c], p.astype(BF16), preferred_element_type=F32)
        return m_new, l, acc

    nch = ((qi + 1) * tq + tk - 1) // tk
    m, l, acc = lax.fori_loop(0, nch, body, (jnp.full((1, 2 * tq), M_INIT, F32),
                                             jnp.zeros((1, 2 * tq), F32),
                                             jnp.zeros((B_V_DIM, 2 * tq), F32)))
    a1, a2 = acc[:, :tq], acc[:, tq:]
    l1, l2 = l[:, :tq], l[:, tq:]

    lv = lam_ref[...]
    lam = (jnp.exp(jnp.sum(lv[0:1] * lv[1:2], axis=1, keepdims=True))
           - jnp.exp(jnp.sum(lv[2:3] * lv[3:4], axis=1, keepdims=True)) + lv[4:5, 0:1])
    o = a1 / l1 - lam * (a2 / l2)
    ms = jnp.mean(o * o, axis=0, keepdims=True)
    o = o * lax.rsqrt(ms + RMS_EPS) * gn_ref[...]
    o_ref[...] = o.T.astype(BF16)


def _diff_attention(bqt, bk, bvt, tab, lam_rows, gn, batch, t):
    tq = B_Q_TILE
    tk = TOKEN_TILE
    per = tk // tq
    nkt = t // tk
    bqt = bqt.reshape(batch, nkt, B_HEADS * LANES, tk)
    bvt = bvt.reshape(batch, nkt, B_OUT, tk)
    bk = bk.reshape(batch, t, B_HEADS * LANES)
    return pl.pallas_call(
        _diff_kernel,
        out_shape=jax.ShapeDtypeStruct((batch, t, B_OUT), BF16),
        grid=(batch, B_HEADS, t // tq),
        in_specs=[pl.BlockSpec((None, None, LANES, tq), lambda b, h, i: (b, i // per, h, i % per)),
                  pl.BlockSpec((None, t, LANES), lambda b, h, i: (b, 0, h)),
                  pl.BlockSpec((None, nkt, B_V_DIM, tk), lambda b, h, i: (b, 0, h, 0)),
                  pl.BlockSpec((None, FAR + 2, LANES, LANES), lambda b, h, i: (h, 0, 0, 0)),
                  _const_spec(lam_rows.shape), _const_spec(gn.shape)],
        out_specs=pl.BlockSpec((None, tq, B_V_DIM), lambda b, h, i: (b, i, h)),
        compiler_params=_cparams(("parallel", "parallel", "arbitrary")),
        name="diff_attention",
    )(bqt, bk, bvt, tab, lam_rows, gn).reshape(batch * t, B_OUT)


def _dsa_kernel(iqt_ref, iwt_ref, cqt_ref, k_ref, vt_ref, tab_ref, tri_ref, o_ref, s_ref, *, k_sel):
    tk = TOKEN_TILE
    nkb = tk // LANES
    qi = pl.program_id(1)
    nch = qi // nkb + 1
    qpos = qi * LANES + lax.broadcasted_iota(jnp.int32, (1, LANES), 1)
    zeros = jnp.zeros((HEAD_DIM, LANES), BF16)
    iq = iqt_ref[...]
    w = iwt_ref[...]
    iq_all = jnp.concatenate([jnp.concatenate([zeros, iq[h * IDX_DIM:(h + 1) * IDX_DIM]], axis=0)
                              for h in range(IDX_HEADS)], axis=1)
    cq = cqt_ref[...]
    cq_all = jnp.concatenate([jnp.concatenate([cq[h * HEAD_DIM:(h + 1) * HEAD_DIM], zeros], axis=0)
                              for h in range(C_HEADS)], axis=1)

    def chunk(c):
        return pl.ds(pl.multiple_of(c * tk, tk), tk)

    def score_chunk(c, mn, mx, last):
        raw = jnp.dot(k_ref[chunk(c), :], iq_all, preferred_element_type=F32)
        acc = w[0:1, :] * jnp.maximum(raw[:, 0:LANES], 0.0)
        for h in range(1, IDX_HEADS):
            acc = acc + w[h:h + 1, :] * jnp.maximum(raw[:, h * LANES:(h + 1) * LANES], 0.0)
        if last:
            kpos = c * tk + lax.broadcasted_iota(jnp.int32, (tk, LANES), 0)
            causal = kpos <= qpos
            s_ref[chunk(c), :] = jnp.where(causal, acc, NEG)
            mn = jnp.minimum(mn, _fold_rows(jnp.where(causal, acc, BIG), jnp.min))
            mx = jnp.maximum(mx, _fold_rows(jnp.where(causal, acc, NEG), jnp.max))
        else:
            s_ref[chunk(c), :] = acc
            mn = jnp.minimum(mn, _fold_rows(acc, jnp.min))
            mx = jnp.maximum(mx, _fold_rows(acc, jnp.max))
        return mn, mx

    mn, mx = lax.fori_loop(0, nch - 1, lambda c, carry: score_chunk(c, *carry, last=False),
                           (jnp.full((64, LANES), BIG, F32), jnp.full((64, LANES), NEG, F32)))
    mn, mx = score_chunk(nch - 1, mn, mx, last=True)
    lo, hi = _reduce_rows(mn, jnp.min), _reduce_rows(mx, jnp.max)

    def count(thr):
        def body(c, carry):
            gt, ge = carry
            s = s_ref[chunk(c), :]
            gt = gt + _fold_rows(jnp.where(s > thr, 1.0, 0.0), jnp.sum)
            ge = ge + _fold_rows(jnp.where(s >= thr, 1.0, 0.0), jnp.sum)
            return gt, ge
        z = jnp.zeros((64, LANES), F32)
        gt, ge = lax.fori_loop(0, nch, body, (z, z))
        return _reduce_rows(gt, jnp.sum), _reduce_rows(ge, jnp.sum)

    def count_gt(thr):
        def body(c, gt):
            return gt + _fold_rows(jnp.where(s_ref[chunk(c), :] > thr, 1.0, 0.0), jnp.sum)
        return _reduce_rows(lax.fori_loop(0, nch, body, jnp.zeros((64, LANES), F32)), jnp.sum)

    def max_below(bound, strict):
        def body(c, mx):
            s = s_ref[chunk(c), :]
            keep = (s < bound) if strict else (s <= bound)
            return jnp.maximum(mx, _fold_rows(jnp.where(keep, s, NEG), jnp.max))
        return _reduce_rows(lax.fori_loop(0, nch, body, jnp.full((64, LANES), NEG, F32)), jnp.max)

    kf = float(k_sel)
    need = qpos >= k_sel

    def bisect(_, carry):
        lo, hi = carry
        mid = lo + (hi - lo) * 0.5
        below = count_gt(mid) < kf
        return jnp.where(below, lo, mid), jnp.where(below, mid, hi)

    lo, hi = lax.fori_loop(0, N_BISECT, bisect, (lo, hi))

    cand = max_below(hi, strict=False)
    gt, ge = count(cand)

    def walk_cond(carry):
        _, _, ge = carry
        return jnp.max(jnp.where(need & (ge < kf), 1.0, 0.0)) > 0.0

    def walk_body(carry):
        cand, gt, ge = carry
        nxt = max_below(cand, strict=True)
        cand = jnp.where(ge < kf, nxt, cand)
        gt, ge = count(cand)
        return cand, gt, ge

    cand, gt, ge = lax.while_loop(walk_cond, walk_body, (cand, gt, ge))
    thr = jnp.where(need, cand, THR_ALL)
    want_eq = jnp.where(need, kf - gt, 0.0)

    any_tie = jnp.max(jnp.where(need & (ge > kf), 1.0, 0.0)) > 0.0

    def attn_body(c, carry):
        eq_seen, m, l, acc = carry
        s = s_ref[chunk(c), :]

        def with_ties(eq_seen):
            eq = jnp.where(s == thr, 1.0, 0.0)
            rank = eq_seen + jnp.dot(tri_ref[...], eq.astype(BF16), preferred_element_type=F32)
            keep = jnp.where(s > thr, 1.0, jnp.where(rank <= want_eq, eq, 0.0))
            return jnp.where(keep > 0.5, 0.0, NEG), eq_seen + _reduce_rows(eq, jnp.sum)

        def no_ties(eq_seen):
            return jnp.where(s >= thr, 0.0, NEG), eq_seen

        sel_bias, eq_seen = lax.cond(any_tie, with_ties, no_ties, eq_seen)
        bias = jnp.concatenate(
            [jnp.concatenate([_bias_tile(tab_ref, h, qi, c * nkb + jk) for jk in range(nkb)], axis=0) + sel_bias
             for h in range(C_HEADS)], axis=1)
        lg = jnp.dot(k_ref[chunk(c), :], cq_all, preferred_element_type=F32) + bias
        m_new = jnp.maximum(m, _reduce_rows(lg, jnp.max))
        alpha = jnp.exp(m - m_new)
        p = jnp.exp(lg - m_new)
        l = alpha * l + _reduce_rows(p, jnp.sum)
        acc = alpha * acc + jnp.dot(vt_ref[c], p.astype(BF16), preferred_element_type=F32)
        return eq_seen, m_new, l, acc

    wide = C_HEADS * LANES
    init = (jnp.zeros((1, LANES), F32), jnp.full((1, wide), M_INIT, F32), jnp.zeros((1, wide), F32),
            jnp.zeros((HEAD_DIM, wide), F32))
    _, _, l, acc = lax.fori_loop(0, nch, attn_body, init)
    o = acc / l
    ot = jnp.concatenate([o[:, h * LANES:(h + 1) * LANES] for h in range(C_HEADS)], axis=0)
    o_ref[...] = ot.T.astype(BF16)


def _dsa_attention(iqt, iwt, cqt, ck, cvt, tab, batch, t):
    tk = TOKEN_TILE
    per = tk // LANES
    nkt = t // tk
    k_sel = min(TOPK_MAX, t // 4)
    iqt = iqt.reshape(batch, nkt, IDX_HEADS * IDX_DIM, tk)
    iwt = iwt.reshape(batch, nkt, 16, tk)
    cqt = cqt.reshape(batch, nkt, C_OUT, tk)
    cvt = cvt.reshape(batch, nkt, HEAD_DIM, tk)
    ck = ck.reshape(batch, t, LANES)
    r = np.arange(tk)
    tri = jnp.asarray(r[:, None] >= r[None, :], BF16)
    qblock = lambda rows: pl.BlockSpec((None, None, rows, LANES), lambda b, i: (b, i // per, 0, i % per))
    return pl.pallas_call(
        functools.partial(_dsa_kernel, k_sel=k_sel),
        out_shape=jax.ShapeDtypeStruct((batch, t, C_OUT), BF16),
        grid=(batch, t // LANES),
        in_specs=[qblock(IDX_HEADS * IDX_DIM), qblock(16), qblock(C_OUT),
                  pl.BlockSpec((None, t, LANES), lambda b, i: (b, 0, 0)),
                  pl.BlockSpec((None, nkt, HEAD_DIM, tk), lambda b, i: (b, 0, 0, 0)),
                  _const_spec(tab.shape), _const_spec(tri.shape)],
        out_specs=pl.BlockSpec((None, LANES, C_OUT), lambda b, i: (b, i, 0)),
        scratch_shapes=[pltpu.VMEM((t, LANES), F32)],
        compiler_params=_cparams(("parallel", "arbitrary")),
        name="dsa_attention",
    )(iqt, iwt, cqt, ck, cvt, tab, tri).reshape(batch * t, C_OUT)


def _merge_kernel(x_ref, *refs):
    ng = len(DIL_GROUPS)
    a_refs = refs[:2 * ng]
    ob_ref, oc_ref, gate_ref, wa_ref, wb_ref, wc_ref, wo_ref, out_ref = refs[2 * ng:-1]
    shuffle_ref = refs[-1]
    tm, d = x_ref.shape

    def token_order(ref, slot):
        dil = ref.shape[0]
        if dil == 1:
            return ref[0]
        halves = range(A_OUT // LANES)
        for r in range(dil):
            for half in halves:
                shuffle_ref[slot, half, pl.ds(r, tm // dil, stride=dil), :] = ref[r, :, half * LANES:(half + 1) * LANES]
        return jnp.concatenate([shuffle_ref[slot, half] for half in halves], axis=1)

    outs = [token_order(a_refs[2 * g], 2 * g) for g in range(ng)]
    lses = [token_order(a_refs[2 * g + 1], 2 * g + 1) for g in range(ng)]
    top = functools.reduce(jnp.maximum, lses)
    es = [jnp.exp(lse - top) for lse in lses]
    num = sum(e * o for e, o in zip(es, outs))
    oa = (num / sum(es)).astype(BF16)
    y = gate_ref[:, 0:d].astype(F32) * jnp.dot(oa, wa_ref[...], preferred_element_type=F32)
    y = y + gate_ref[:, d:2 * d].astype(F32) * jnp.dot(ob_ref[...], wb_ref[...], preferred_element_type=F32)
    y = y + gate_ref[:, 2 * d:3 * d].astype(F32) * jnp.dot(oc_ref[...], wc_ref[...], preferred_element_type=F32)
    out_ref[...] = x_ref[...] + jnp.dot(y.astype(BF16), wo_ref[...], preferred_element_type=F32)


def _merge(x, a_parts, ob, oc, gates, wa, wb, wc, wo, t):
    n, d = x.shape
    tm = TOKEN_TILE
    per_batch = t // tm
    tok = lambda c: pl.BlockSpec((tm, c), lambda i: (i, 0))
    by_residue = lambda dil: pl.BlockSpec((None, dil, tm // dil, A_OUT),
                                          lambda i: (i // per_batch, 0, i % per_batch, 0))
    ws = [w.astype(BF16) for w in (wa, wb, wc, wo)]
    return pl.pallas_call(
        _merge_kernel,
        out_shape=jax.ShapeDtypeStruct((n, d), F32),
        grid=(n // tm,),
        in_specs=[tok(d)] + [by_residue(z.shape[1]) for z in a_parts] + [tok(B_OUT), tok(C_OUT), tok(3 * d)]
                 + [_const_spec(w.shape) for w in ws],
        out_specs=tok(d),
        scratch_shapes=[pltpu.VMEM((len(a_parts), A_OUT // LANES, tm, LANES), F32)],
        compiler_params=_cparams(("parallel",)),
        name="merge",
    )(x, *a_parts, ob, oc, gates, *ws)


def _token_mixer(x, batch, t, layer, mix_norm, w_in, qk_gain, diff_lambda, diff_out_norm,
                 w_branch_a, w_branch_b, w_branch_c, w_out, band_tabs, tab_b, tab_c):
    ng = len(DIL_GROUPS)
    outs = _project(x, mix_norm, w_in, qk_gain, batch, t)
    a_in, (bk, ck, gates, bqt, bvt, cqt, cvt, iqt, iwt) = outs[:3 * ng], outs[3 * ng:]
    a_parts = []
    for g, (_, dilation) in enumerate(DIL_GROUPS):
        a_parts += _dilated_group(a_in[g], a_in[ng + g], a_in[2 * ng + g], band_tabs[g], dilation)
    lam_init = 0.8 - 0.6 * np.exp(-0.3 * layer)
    lam_rows = jnp.concatenate([diff_lambda.astype(F32), jnp.full((4, HEAD_DIM), lam_init, F32)], axis=0)
    gn = jnp.broadcast_to((diff_out_norm.astype(F32) * (1.0 - lam_init))[:, None], (B_V_DIM, B_Q_TILE))
    ob = _diff_attention(bqt, bk, bvt, tab_b, lam_rows, gn, batch, t)
    oc = _dsa_attention(iqt, iwt, cqt, ck, cvt, tab_c, batch, t)
    return _merge(x, a_parts, ob, oc, gates, w_branch_a, w_branch_b, w_branch_c, w_out, t)


def kernel(x, rel_bias, ffn1_norm, ffn1_w_gate, ffn1_w_up, ffn1_w_down, mix_norm, w_in, qk_gain,
           diff_lambda, diff_out_norm, w_branch_a, w_branch_b, w_branch_c, w_out,
           ffn2_norm, ffn2_w_gate, ffn2_w_up, ffn2_w_down):
    batch, t, d = x.shape
    depth = w_in.shape[0]
    assert t % (DIL_GROUPS[-1][1] * LANES) == 0 and t % TOKEN_TILE == 0
    band_tabs = [_band_tables(rel_bias[:, g * A_GROUP_HEADS:(g + 1) * A_GROUP_HEADS], dil)
                 for g, (_, dil) in enumerate(DIL_GROUPS)]
    tab_b = _toeplitz_tables(rel_bias[:, A_HEADS:A_HEADS + B_HEADS])
    tab_c = _toeplitz_tables(rel_bias[:, A_HEADS + B_HEADS:])
    h = x.reshape(batch * t, d).astype(F32)
    for i in range(depth):
        h = _ffn(h, ffn1_norm[i], ffn1_w_gate[i], ffn1_w_up[i], ffn1_w_down[i])
        h = _token_mixer(h, batch, t, i, mix_norm[i], w_in[i], qk_gain[i], diff_lambda[i], diff_out_norm[i],
                         w_branch_a[i], w_branch_b[i], w_branch_c[i], w_out[i], band_tabs, tab_b, tab_c)
        h = _ffn(h, ffn2_norm[i], ffn2_w_gate[i], ffn2_w_up[i], ffn2_w_down[i])
    return h.reshape(batch, t, d).astype(x.dtype)
```

```python
import functools

import numpy as np
import jax
import jax.numpy as jnp
from jax import lax
from jax.experimental import pallas as pl
from jax.experimental.pallas import tpu as pltpu

F32 = jnp.float32
BF16 = jnp.bfloat16

HEAD_DIM = 64
DIL_GROUPS = ((128, 1), (512, 4), (2048, 16))
A_GROUP_HEADS = 4
A_HEADS = A_GROUP_HEADS * len(DIL_GROUPS)
A_OUT = A_GROUP_HEADS * HEAD_DIM
B_HEADS = 4
B_V_DIM = 2 * HEAD_DIM
B_OUT = B_HEADS * B_V_DIM
C_HEADS = 4
C_OUT = C_HEADS * HEAD_DIM
IDX_HEADS = 8
IDX_DIM = 64
TOPK_MAX = 256
NUM_BUCKETS = 32
MAX_DISTANCE = 2048
RMS_EPS = 1e-6

LANES = 128
TOKEN_TILE = 512
B_Q_TILE = 512
MAX_LAG = 60.0
FFN_CHUNK = 256
VMEM_LIMIT = 58 * 1024 * 1024

NEG = -1e30
M_INIT = -1e29
BIG = 1e30
THR_ALL = -1e29
N_BISECT = 20


def _cparams(sem):
    return pltpu.CompilerParams(dimension_semantics=sem, vmem_limit_bytes=VMEM_LIMIT)


def _const_spec(shape):
    nd = len(shape)
    return pl.BlockSpec(shape, lambda *_: (0,) * nd, pipeline_mode=pl.Buffered(1))


def _rel_bucket_np(dist):
    n = np.maximum(dist, 0)
    max_exact = NUM_BUCKETS // 2
    nf = np.maximum(n, 1).astype(np.float64)
    large = max_exact + (np.log(nf / max_exact) / np.log(MAX_DISTANCE / max_exact)
                         * (NUM_BUCKETS - max_exact)).astype(np.int64)
    large = np.minimum(large, NUM_BUCKETS - 1)
    return np.where(n < max_exact, n, large)


def _far_delta():
    d = 1
    while not np.all(_rel_bucket_np(np.arange(d * LANES - LANES + 1, d * LANES + LANES)) == NUM_BUCKETS - 1):
        d += 1
    return d


FAR = _far_delta()
MASKED = FAR + 1


def _toeplitz(w, n_rows, n_cols):
    period = n_rows + n_cols
    w = jnp.pad(w, ((0, 0), (0, period - w.shape[1])))
    m = jnp.tile(w, (1, n_rows))[:, :n_rows * (period - 1)].reshape(-1, n_rows, period - 1)
    return m[:, :, n_rows - 1:n_rows - 1 + n_cols]


def _bias_by_distance(bias_heads, dist, valid):
    vals = jnp.take(bias_heads.astype(F32), jnp.asarray(_rel_bucket_np(dist), jnp.int32), axis=0).T
    return jnp.where(jnp.asarray(valid)[None], vals, NEG)


def _toeplitz_tables(bias_heads):
    n_cols = (FAR + 1) * LANES
    dist = np.arange(LANES - 1 + n_cols) - (LANES - 1)
    tiles = _toeplitz(_bias_by_distance(bias_heads, dist, dist >= 0), LANES, n_cols)
    tiles = tiles.reshape(-1, LANES, FAR + 1, LANES).transpose(0, 2, 1, 3)
    masked = jnp.full((tiles.shape[0], 1, LANES, LANES), NEG, F32)
    return jnp.concatenate([tiles, masked], axis=1)


def _band_tables(bias_heads, dilation):
    wn = LANES
    sub = np.arange(3 * wn - 1) - (wn - 1)
    w = _bias_by_distance(bias_heads, sub * dilation, (sub >= 0) & (sub <= wn))
    later = jnp.flip(_toeplitz(w, wn, 2 * wn), axis=(1, 2))
    first = jnp.where(jnp.asarray(np.arange(2 * wn) >= wn)[None, None], later, NEG)
    return jnp.stack([first, later])


def _ffn_kernel(x_ref, g_ref, wg_ref, wu_ref, wd_ref, o_ref, acc_ref):
    x = x_ref[...]
    ms = jnp.mean(x * x, axis=-1, keepdims=True)
    h = (x * lax.rsqrt(ms + RMS_EPS) * g_ref[...]).astype(BF16)
    acc_ref[...] = jnp.zeros_like(acc_ref)

    def body(c, carry):
        g = jnp.dot(h, wg_ref[c], preferred_element_type=F32)
        u = jnp.dot(h, wu_ref[c], preferred_element_type=F32)
        a = (g * jax.nn.sigmoid(g) * u).astype(BF16)
        acc_ref[...] += jnp.dot(a, wd_ref[c], preferred_element_type=F32)
        return carry

    lax.fori_loop(0, wg_ref.shape[0], body, 0)
    o_ref[...] = x + 0.5 * acc_ref[...]


def _ffn(x, gain, w_gate, w_up, w_down):
    n, d = x.shape
    f = w_gate.shape[1]
    nck = f // FFN_CHUNK
    wg = w_gate.astype(BF16).reshape(d, nck, FFN_CHUNK).transpose(1, 0, 2)
    wu = w_up.astype(BF16).reshape(d, nck, FFN_CHUNK).transpose(1, 0, 2)
    wd = w_down.astype(BF16).reshape(nck, FFN_CHUNK, d)
    tm = TOKEN_TILE
    return pl.pallas_call(
        _ffn_kernel,
        out_shape=jax.ShapeDtypeStruct((n, d), F32),
        grid=(n // tm,),
        in_specs=[pl.BlockSpec((tm, d), lambda i: (i, 0)),
                  _const_spec((1, d)),
                  _const_spec((nck, d, FFN_CHUNK)),
                  _const_spec((nck, d, FFN_CHUNK)),
                  _const_spec((nck, FFN_CHUNK, d))],
        out_specs=pl.BlockSpec((tm, d), lambda i: (i, 0)),
        scratch_shapes=[pltpu.VMEM((tm, d), F32)],
        compiler_params=_cparams(("parallel",)),
        name="ffn",
    )(x, gain.reshape(1, d).astype(F32), wg, wu, wd)


S_AQ, S_AK, S_AV = 0, 768, 1536
S_BK, S_CK, S_GATE, S_END = 2304, 2816, 2944, 6016
T_BQ, T_BV, T_CQ, T_CV, T_IQ, T_IW, T_END = 0, 512, 1024, 1280, 1344, 1856, 1872


def _proj_kernel(x_ref, g_ref, ws_ref, wt_ref, bd_ref, gs_ref, gt_ref, *refs):
    ng = len(DIL_GROUPS)
    a_refs = refs[:3 * ng]
    bk_ref, ck_ref, gate_ref, bqt_ref, bvt_ref, cqt_ref, cvt_ref, iqt_ref, iwt_ref = refs[3 * ng:-1]
    shuffle_ref = refs[-1]
    tm = x_ref.shape[0]

    def store_by_residue(y, which):
        for g, (_, dil) in enumerate(DIL_GROUPS):
            out = a_refs[which * ng + g]
            part = y[:, g * A_OUT:(g + 1) * A_OUT]
            if dil == 1:
                out[0] = part.astype(BF16)
            else:
                for half in range(A_OUT // LANES):
                    shuffle_ref[half] = part[:, half * LANES:(half + 1) * LANES]
                for r in range(dil):
                    out[r] = jnp.concatenate(
                        [shuffle_ref[half, pl.ds(r, tm // dil, stride=dil), :] for half in range(A_OUT // LANES)],
                        axis=1).astype(BF16)

    x = x_ref[...]
    ms = jnp.mean(x * x, axis=-1, keepdims=True)
    h = (x * lax.rsqrt(ms + RMS_EPS) * g_ref[...]).astype(BF16)
    bd = bd_ref[...]

    def dot_s(c0, c1):
        return jnp.dot(h, ws_ref[:, c0:c1], preferred_element_type=F32)

    def head_inv_rms(y):
        outs = []
        for c in range(y.shape[1] // LANES):
            sq = y[:, c * LANES:(c + 1) * LANES]
            sq = sq * sq
            hi = sq.astype(BF16)
            lo = (sq - hi.astype(F32)).astype(BF16)
            msq = (jnp.dot(hi, bd, preferred_element_type=F32)
                   + jnp.dot(lo, bd, preferred_element_type=F32))
            outs.append(lax.rsqrt(msq + RMS_EPS))
        return outs[0] if len(outs) == 1 else jnp.concatenate(outs, axis=1)

    y = dot_s(S_AQ, S_AK)
    store_by_residue(y * head_inv_rms(y) * gs_ref[:, 0:768], 0)
    y = dot_s(S_AK, S_AV)
    store_by_residue(y * head_inv_rms(y) * gs_ref[:, 768:1536], 1)
    store_by_residue(dot_s(S_AV, S_BK), 2)
    y = dot_s(S_BK, S_CK)
    bk_ref[...] = (y * head_inv_rms(y) * gs_ref[:, 1536:2048]).astype(BF16)
    y = dot_s(S_CK, S_GATE)
    lane = lax.broadcasted_iota(jnp.int32, y.shape, 1)
    inv = jnp.where(lane < HEAD_DIM, head_inv_rms(y), 1.0)
    ck_ref[...] = (y * inv * gs_ref[:, 2048:2176]).astype(BF16)
    for c in range(3):
        y = dot_s(S_GATE + c * 1024, S_GATE + (c + 1) * 1024)
        gate_ref[:, c * 1024:(c + 1) * 1024] = jax.nn.sigmoid(y).astype(BF16)

    def dot_t(r0, r1):
        return lax.dot_general(wt_ref[r0:r1, :], h, (((1,), (1,)), ((), ())),
                               preferred_element_type=F32)

    def norm_t(y, gain):
        r = y.shape[0] // HEAD_DIM
        y3 = y.reshape(r, HEAD_DIM, tm)
        msq = jnp.mean(y3 * y3, axis=1, keepdims=True)
        return (y3 * lax.rsqrt(msq + RMS_EPS)).reshape(r * HEAD_DIM, tm) * gain

    bqt_ref[...] = norm_t(dot_t(T_BQ, T_BV), gt_ref[0:512, :]).astype(BF16)
    bvt_ref[...] = dot_t(T_BV, T_CQ).astype(BF16)
    cqt_ref[...] = norm_t(dot_t(T_CQ, T_CV), gt_ref[512:768, :]).astype(BF16)
    cvt_ref[...] = dot_t(T_CV, T_IQ).astype(BF16)
    iqt_ref[...] = dot_t(T_IQ, T_IW).astype(BF16)
    iwt_ref[...] = dot_t(T_IW, T_END) * (IDX_HEADS ** -0.5 * IDX_DIM ** -0.5)


def _proj_weights(w_in, qk_gain, tm):
    d = w_in.shape[0]
    o = 0
    a_qkv = w_in[:, o:o + 3 * A_HEADS * HEAD_DIM].reshape(d, 3, A_HEADS * HEAD_DIM)
    o += 3 * A_HEADS * HEAD_DIM
    b_qk = w_in[:, o:o + 4 * B_HEADS * HEAD_DIM].reshape(d, 4, B_HEADS, HEAD_DIM)
    o += 4 * B_HEADS * HEAD_DIM
    b_v = w_in[:, o:o + B_OUT]
    o += B_OUT
    c_q = w_in[:, o:o + C_OUT]
    c_k = w_in[:, o + C_OUT:o + C_OUT + HEAD_DIM]
    c_v = w_in[:, o + C_OUT + HEAD_DIM:o + C_OUT + 2 * HEAD_DIM]
    o += C_OUT + 2 * HEAD_DIM
    i_q = w_in[:, o:o + IDX_HEADS * IDX_DIM]
    i_k = w_in[:, o + IDX_HEADS * IDX_DIM:o + IDX_HEADS * IDX_DIM + IDX_DIM]
    i_w = w_in[:, o + IDX_HEADS * IDX_DIM + IDX_DIM:o + IDX_HEADS * IDX_DIM + IDX_DIM + IDX_HEADS]
    o += IDX_HEADS * IDX_DIM + IDX_DIM + IDX_HEADS
    gates = w_in[:, o:]
    b_k = jnp.stack([b_qk[:, 2], b_qk[:, 3]], axis=2).reshape(d, 2 * B_HEADS * HEAD_DIM)
    b_q = jnp.stack([b_qk[:, 0], b_qk[:, 1]], axis=2).reshape(d, 2 * B_HEADS * HEAD_DIM)
    w_s = jnp.concatenate([a_qkv[:, 0], a_qkv[:, 1], a_qkv[:, 2], b_k, c_k, i_k, gates], axis=1)
    w_t = jnp.concatenate([b_q, b_v, c_q, c_v, i_q, i_w, jnp.zeros((d, 8), w_in.dtype)], axis=1).T
    assert w_s.shape[1] == S_END and w_t.shape[0] == T_END
    scale = HEAD_DIM ** -0.5
    g = qk_gain.astype(F32)
    gs = jnp.concatenate([jnp.tile(g[0, 0] * scale, A_HEADS), jnp.tile(g[0, 1], A_HEADS),
                          jnp.tile(g[1, 1], 2 * B_HEADS), g[2, 1], jnp.ones((IDX_DIM,), F32)])[None]
    gt = jnp.concatenate([jnp.tile(g[1, 0] * scale, 2 * B_HEADS), jnp.tile(g[2, 0] * scale, C_HEADS)])
    gt = jnp.broadcast_to(gt[:, None], (gt.shape[0], tm))
    return w_s.astype(BF16), w_t.astype(BF16), gs, gt


def _head_block_diag():
    r = np.arange(LANES)
    return jnp.asarray((r[:, None] // HEAD_DIM == r[None, :] // HEAD_DIM) / HEAD_DIM, BF16)


def _project(x, gain, w_in, qk_gain, batch, t):
    n, d = x.shape
    tm = TOKEN_TILE
    nt = n // tm
    per_batch = t // tm
    w_s, w_t, gs, gt = _proj_weights(w_in, qk_gain, tm)
    tok = lambda c: pl.BlockSpec((tm, c), lambda i: (i, 0))
    feat = lambda r: pl.BlockSpec((None, r, tm), lambda i: (i, 0, 0))
    a_shapes, a_specs = [], []
    for _ in range(3):
        for _, dil in DIL_GROUPS:
            a_shapes.append(jax.ShapeDtypeStruct((batch, dil, t // dil, A_OUT), BF16))
            a_specs.append(pl.BlockSpec((None, dil, tm // dil, A_OUT),
                                        lambda i: (i // per_batch, 0, i % per_batch, 0)))
    out_shape = a_shapes + [
        jax.ShapeDtypeStruct((n, 512), BF16), jax.ShapeDtypeStruct((n, 128), BF16),
        jax.ShapeDtypeStruct((n, 3072), BF16),
        jax.ShapeDtypeStruct((nt, 512, tm), BF16), jax.ShapeDtypeStruct((nt, 512, tm), BF16),
        jax.ShapeDtypeStruct((nt, 256, tm), BF16), jax.ShapeDtypeStruct((nt, 64, tm), BF16),
        jax.ShapeDtypeStruct((nt, 512, tm), BF16), jax.ShapeDtypeStruct((nt, 16, tm), F32)]
    out_specs = a_specs + [tok(512), tok(128), tok(3072),
                           feat(512), feat(512), feat(256), feat(64), feat(512), feat(16)]
    return pl.pallas_call(
        _proj_kernel,
        out_shape=out_shape,
        grid=(nt,),
        in_specs=[tok(d), _const_spec((1, d)), _const_spec(w_s.shape), _const_spec(w_t.shape),
                  _const_spec((LANES, LANES)), _const_spec(gs.shape), _const_spec(gt.shape)],
        out_specs=out_specs,
        scratch_shapes=[pltpu.VMEM((A_OUT // LANES, tm, LANES), F32)],
        compiler_params=_cparams(("parallel",)),
        name="proj",
    )(x, gain.reshape(1, d).astype(F32), w_s, w_t, _head_block_diag(), gs, gt)


def _dil_kernel(q_ref, kp_ref, kc_ref, vp_ref, vc_ref, bias_ref, o_ref, lse_ref):
    nq = q_ref.shape[0] // LANES
    qi = pl.program_id(2)
    lane = lax.broadcasted_iota(jnp.int32, (LANES, A_OUT), 1) // HEAD_DIM
    for jb in range(nq):
        rows = slice(jb * LANES, (jb + 1) * LANES)
        q = q_ref[rows, :]
        if jb == 0:
            kband = jnp.concatenate([kp_ref[...], kc_ref[rows, :]], axis=0)
            vband = jnp.concatenate([vp_ref[...], vc_ref[rows, :]], axis=0)
            variant = jnp.minimum(qi, 1)
        else:
            band = slice((jb - 1) * LANES, (jb + 1) * LANES)
            kband = kc_ref[band, :]
            vband = vc_ref[band, :]
            variant = 1
        o = jnp.zeros((LANES, A_OUT), F32)
        lse = jnp.zeros((LANES, A_OUT), F32)
        for h in range(A_GROUP_HEADS):
            mine = lane == h
            qm = jnp.where(mine, q, jnp.zeros_like(q))
            s = lax.dot_general(qm, kband, (((1,), (1,)), ((), ())), preferred_element_type=F32)
            s = s + bias_ref[variant, h]
            m = jnp.max(s, axis=1, keepdims=True)
            p = jnp.exp(s - m)
            ssum = jnp.sum(p, axis=1, keepdims=True)
            pv = jnp.dot(p.astype(BF16), vband, preferred_element_type=F32)
            o = jnp.where(mine, pv / ssum, o)
            lse = jnp.where(mine, m + jnp.log(ssum), lse)
        o_ref[rows, :] = o
        lse_ref[rows, :] = lse


def _dilated_group(aq, ak, av, bias, dilation):
    batch, _, n, _ = aq.shape
    nblk = n // LANES
    nq = min(nblk, 4)
    qt = nq * LANES
    cur = pl.BlockSpec((None, None, qt, A_OUT), lambda b, r, i: (b, r, i, 0))
    prev = pl.BlockSpec((None, None, LANES, A_OUT), lambda b, r, i: (b, r, jnp.maximum(i * nq - 1, 0), 0))
    shp = jax.ShapeDtypeStruct((batch, dilation, n, A_OUT), F32)
    return pl.pallas_call(
        _dil_kernel,
        out_shape=[shp, shp],
        grid=(batch, dilation, nblk // nq),
        in_specs=[cur, prev, cur, prev, cur, _const_spec(bias.shape)],
        out_specs=[cur, cur],
        compiler_params=_cparams(("parallel", "parallel", "parallel")),
        name=f"dilated_d{dilation}",
    )(aq, ak, ak, av, av, bias)


def _fold_rows(x, op):
    r, c = x.shape
    return op(x.reshape(r // 64, 64, c), axis=0) if r > 64 else x


def _reduce_rows(x, op):
    x = _fold_rows(x, op)
    x = op(x.reshape(8, 8, x.shape[1]), axis=0)
    return op(x, axis=0, keepdims=True)


def _bias_tile(tab_ref, head, qblk, kblk):
    delta = qblk - kblk
    idx = jnp.where(delta < 0, MASKED, jnp.minimum(delta, FAR))
    if head is None:
        return tab_ref[idx]
    return tab_ref[head, idx]


def _diff_kernel(qt_ref, k_ref, vt_ref, tab_ref, lam_ref, gn_ref, o_ref):
    tq = qt_ref.shape[1]
    tk = TOKEN_TILE
    qi = pl.program_id(2)
    qt = qt_ref[...]
    row = lax.broadcasted_iota(jnp.int32, qt.shape, 0)
    q12 = jnp.concatenate([jnp.where(row < HEAD_DIM, qt, jnp.zeros_like(qt)),
                           jnp.where(row >= HEAD_DIM, qt, jnp.zeros_like(qt))], axis=1)
    nqb = tq // LANES
    nkb = tk // LANES

    def logits(c):
        kc = k_ref[pl.ds(pl.multiple_of(c * tk, tk), tk), :]
        bias = jnp.concatenate(
            [jnp.concatenate([_bias_tile(tab_ref, None, qi * nqb + iq, c * nkb + jk)
                              for iq in range(nqb)] * 2, axis=1) for jk in range(nkb)], axis=0)
        return jnp.dot(kc, q12, preferred_element_type=F32) + bias

    def exact_step(c, carry):
        m, l, acc = carry
        s = logits(c)
        m_new = jnp.maximum(m, _reduce_rows(s, jnp.max))
        alpha = jnp.exp(m - m_new)
        p = jnp.exp(s - m_new)
        l = alpha * l + _reduce_rows(p, jnp.sum)
        acc = alpha * acc + jnp.dot(vt_ref[c], p.astype(BF16), preferred_element_type=F32)
        return m_new, l, acc

    def lagged_update(c, s, m, l, acc, jump):
        p = jnp.exp(s - m)
        top = _reduce_rows(s, jnp.max)
        l = l + _reduce_rows(p, jnp.sum)
        acc = acc + jnp.dot(vt_ref[c], p.astype(BF16), preferred_element_type=F32)
        m_new = jnp.maximum(m, top)
        alpha = jnp.exp(m - m_new)
        return m_new, alpha * l, alpha * acc, jnp.maximum(jump, top - m)

    nch = ((qi + 1) * tq + tk - 1) // tk
    zero = jnp.zeros((1, 2 * tq), F32)
    acc0 = jnp.zeros((B_V_DIM, 2 * tq), F32)
    s0 = logits(0)
    m0 = jnp.max(s0[0:8], axis=0, keepdims=True)
    state = lagged_update(0, s0, m0, zero, acc0, zero)
    _, l, acc, jump = lax.fori_loop(1, nch, lambda c, carry: lagged_update(c, logits(c), *carry), state)
    l, acc = lax.cond(jnp.max(jump) > MAX_LAG,
                      lambda: lax.fori_loop(0, nch, exact_step, (jnp.full((1, 2 * tq), M_INIT, F32), zero, acc0))[1:],
                      lambda: (l, acc))
    a1, a2 = acc[:, :tq], acc[:, tq:]
    l1, l2 = l[:, :tq], l[:, tq:]

    lv = lam_ref[...]
    lam = (jnp.exp(jnp.sum(lv[0:1] * lv[1:2], axis=1, keepdims=True))
           - jnp.exp(jnp.sum(lv[2:3] * lv[3:4], axis=1, keepdims=True)) + lv[4:5, 0:1])
    o = a1 / l1 - lam * (a2 / l2)
    ms = jnp.mean(o * o, axis=0, keepdims=True)
    o = o * lax.rsqrt(ms + RMS_EPS) * gn_ref[...]
    o_ref[...] = o.T.astype(BF16)


def _diff_attention(bqt, bk, bvt, tab, lam_rows, gn, batch, t):
    tq = B_Q_TILE
    tk = TOKEN_TILE
    per = tk // tq
    nkt = t // tk
    bqt = bqt.reshape(batch, nkt, B_HEADS * LANES, tk)
    bvt = bvt.reshape(batch, nkt, B_OUT, tk)
    bk = bk.reshape(batch, t, B_HEADS * LANES)
    return pl.pallas_call(
        _diff_kernel,
        out_shape=jax.ShapeDtypeStruct((batch, t, B_OUT), BF16),
        grid=(batch, B_HEADS, t // tq),
        in_specs=[pl.BlockSpec((None, None, LANES, tq), lambda b, h, i: (b, i // per, h, i % per)),
                  pl.BlockSpec((None, t, LANES), lambda b, h, i: (b, 0, h)),
                  pl.BlockSpec((None, nkt, B_V_DIM, tk), lambda b, h, i: (b, 0, h, 0)),
                  pl.BlockSpec((None, FAR + 2, LANES, LANES), lambda b, h, i: (h, 0, 0, 0)),
                  _const_spec(lam_rows.shape), _const_spec(gn.shape)],
        out_specs=pl.BlockSpec((None, tq, B_V_DIM), lambda b, h, i: (b, i, h)),
        compiler_params=_cparams(("parallel", "parallel", "arbitrary")),
        name="diff_attention",
    )(bqt, bk, bvt, tab, lam_rows, gn).reshape(batch * t, B_OUT)


def _dsa_kernel(iqt_ref, iwt_ref, cqt_ref, k_ref, vt_ref, tab_ref, tri_ref, o_ref, s_ref, *, k_sel):
    tk = TOKEN_TILE
    nkb = tk // LANES
    qi = pl.program_id(1)
    nch = qi // nkb + 1
    qpos = qi * LANES + lax.broadcasted_iota(jnp.int32, (1, LANES), 1)
    zeros = jnp.zeros((HEAD_DIM, LANES), BF16)
    iq = iqt_ref[...]
    w = iwt_ref[...]
    iq_all = jnp.concatenate([jnp.concatenate([zeros, iq[h * IDX_DIM:(h + 1) * IDX_DIM]], axis=0)
                              for h in range(IDX_HEADS)], axis=1)
    cq = cqt_ref[...]
    cq_all = jnp.concatenate([jnp.concatenate([cq[h * HEAD_DIM:(h + 1) * HEAD_DIM], zeros], axis=0)
                              for h in range(C_HEADS)], axis=1)

    def chunk(c):
        return pl.ds(pl.multiple_of(c * tk, tk), tk)

    def score_chunk(c, mn, mx, last):
        raw = jnp.dot(k_ref[chunk(c), :], iq_all, preferred_element_type=F32)
        acc = w[0:1, :] * jnp.maximum(raw[:, 0:LANES], 0.0)
        for h in range(1, IDX_HEADS):
            acc = acc + w[h:h + 1, :] * jnp.maximum(raw[:, h * LANES:(h + 1) * LANES], 0.0)
        if last:
            kpos = c * tk + lax.broadcasted_iota(jnp.int32, (tk, LANES), 0)
            causal = kpos <= qpos
            s_ref[chunk(c), :] = jnp.where(causal, acc, NEG)
            mn = jnp.minimum(mn, _fold_rows(jnp.where(causal, acc, BIG), jnp.min))
            mx = jnp.maximum(mx, _fold_rows(jnp.where(causal, acc, NEG), jnp.max))
        else:
            s_ref[chunk(c), :] = acc
            mn = jnp.minimum(mn, _fold_rows(acc, jnp.min))
            mx = jnp.maximum(mx, _fold_rows(acc, jnp.max))
        return mn, mx

    mn, mx = lax.fori_loop(0, nch - 1, lambda c, carry: score_chunk(c, *carry, last=False),
                           (jnp.full((64, LANES), BIG, F32), jnp.full((64, LANES), NEG, F32)))
    mn, mx = score_chunk(nch - 1, mn, mx, last=True)
    lo, hi = _reduce_rows(mn, jnp.min), _reduce_rows(mx, jnp.max)

    def count(thr):
        def body(c, carry):
            gt, ge = carry
            s = s_ref[chunk(c), :]
            gt = gt + _fold_rows(jnp.where(s > thr, 1.0, 0.0), jnp.sum)
            ge = ge + _fold_rows(jnp.where(s >= thr, 1.0, 0.0), jnp.sum)
            return gt, ge
        z = jnp.zeros((64, LANES), F32)
        gt, ge = lax.fori_loop(0, nch, body, (z, z))
        return _reduce_rows(gt, jnp.sum), _reduce_rows(ge, jnp.sum)

    def count_gt(thr):
        def body(c, gt):
            return gt + _fold_rows(jnp.where(s_ref[chunk(c), :] > thr, 1.0, 0.0), jnp.sum)
        return _reduce_rows(lax.fori_loop(0, nch, body, jnp.zeros((64, LANES), F32)), jnp.sum)

    def max_below(bound, strict):
        def body(c, mx):
            s = s_ref[chunk(c), :]
            keep = (s < bound) if strict else (s <= bound)
            return jnp.maximum(mx, _fold_rows(jnp.where(keep, s, NEG), jnp.max))
        return _reduce_rows(lax.fori_loop(0, nch, body, jnp.full((64, LANES), NEG, F32)), jnp.max)

    kf = float(k_sel)
    need = qpos >= k_sel

    def bisect(_, carry):
        lo, hi = carry
        mid = lo + (hi - lo) * 0.5
        below = count_gt(mid) < kf
        return jnp.where(below, lo, mid), jnp.where(below, mid, hi)

    lo, hi = lax.fori_loop(0, N_BISECT, bisect, (lo, hi))

    cand = max_below(hi, strict=False)
    gt, ge = count(cand)

    def walk_cond(carry):
        _, _, ge = carry
        return jnp.max(jnp.where(need & (ge < kf), 1.0, 0.0)) > 0.0

    def walk_body(carry):
        cand, gt, ge = carry
        nxt = max_below(cand, strict=True)
        cand = jnp.where(ge < kf, nxt, cand)
        gt, ge = count(cand)
        return cand, gt, ge

    cand, gt, ge = lax.while_loop(walk_cond, walk_body, (cand, gt, ge))
    thr = jnp.where(need, cand, THR_ALL)
    want_eq = jnp.where(need, kf - gt, 0.0)

    any_tie = jnp.max(jnp.where(need & (ge > kf), 1.0, 0.0)) > 0.0

    def mark_with_ties(c, eq_seen):
        s = s_ref[chunk(c), :]
        eq = jnp.where(s == thr, 1.0, 0.0)
        rank = eq_seen + jnp.dot(tri_ref[...], eq.astype(BF16), preferred_element_type=F32)
        keep = jnp.where(s > thr, 1.0, jnp.where(rank <= want_eq, eq, 0.0))
        s_ref[chunk(c), :] = jnp.where(keep > 0.5, 0.0, NEG)
        return eq_seen + _reduce_rows(eq, jnp.sum)

    def mark_no_ties(c, carry):
        s_ref[chunk(c), :] = jnp.where(s_ref[chunk(c), :] >= thr, 0.0, NEG)
        return carry

    @pl.when(any_tie)
    def _():
        lax.fori_loop(0, nch, mark_with_ties, jnp.zeros((1, LANES), F32))

    @pl.when(jnp.logical_not(any_tie))
    def _():
        lax.fori_loop(0, nch, mark_no_ties, 0)

    def masked_logits(c):
        sel = s_ref[chunk(c), :]
        bias = jnp.concatenate(
            [jnp.concatenate([_bias_tile(tab_ref, h, qi, c * nkb + jk) for jk in range(nkb)], axis=0) + sel
             for h in range(C_HEADS)], axis=1)
        return jnp.dot(k_ref[chunk(c), :], cq_all, preferred_element_type=F32) + bias

    def exact_step(c, carry):
        m, l, acc = carry
        lg = masked_logits(c)
        m_new = jnp.maximum(m, _reduce_rows(lg, jnp.max))
        alpha = jnp.exp(m - m_new)
        p = jnp.exp(lg - m_new)
        l = alpha * l + _reduce_rows(p, jnp.sum)
        acc = alpha * acc + jnp.dot(vt_ref[c], p.astype(BF16), preferred_element_type=F32)
        return m_new, l, acc

    def lagged_step(c, carry):
        m, l, acc, jump = carry
        lg = masked_logits(c)
        p = jnp.exp(lg - m)
        top = _reduce_rows(lg, jnp.max)
        l = l + _reduce_rows(p, jnp.sum)
        acc = acc + jnp.dot(vt_ref[c], p.astype(BF16), preferred_element_type=F32)
        m_new = jnp.maximum(m, top)
        alpha = jnp.exp(m - m_new)
        return m_new, alpha * l, alpha * acc, jnp.maximum(jump, top - m)

    wide = C_HEADS * LANES
    init = (jnp.full((1, wide), M_INIT, F32), jnp.zeros((1, wide), F32), jnp.zeros((HEAD_DIM, wide), F32))
    first = exact_step(0, init)
    _, l, acc, jump = lax.fori_loop(1, nch, lagged_step, first + (jnp.zeros((1, wide), F32),))
    l, acc = lax.cond(jnp.max(jump) > MAX_LAG,
                      lambda: lax.fori_loop(1, nch, exact_step, first)[1:],
                      lambda: (l, acc))
    o = acc / l
    ot = jnp.concatenate([o[:, h * LANES:(h + 1) * LANES] for h in range(C_HEADS)], axis=0)
    o_ref[...] = ot.T.astype(BF16)


def _dsa_attention(iqt, iwt, cqt, ck, cvt, tab, batch, t):
    tk = TOKEN_TILE
    per = tk // LANES
    nkt = t // tk
    k_sel = min(TOPK_MAX, t // 4)
    iqt = iqt.reshape(batch, nkt, IDX_HEADS * IDX_DIM, tk)
    iwt = iwt.reshape(batch, nkt, 16, tk)
    cqt = cqt.reshape(batch, nkt, C_OUT, tk)
    cvt = cvt.reshape(batch, nkt, HEAD_DIM, tk)
    ck = ck.reshape(batch, t, LANES)
    r = np.arange(tk)
    tri = jnp.asarray(r[:, None] >= r[None, :], BF16)
    qblock = lambda rows: pl.BlockSpec((None, None, rows, LANES), lambda b, i: (b, i // per, 0, i % per))
    return pl.pallas_call(
        functools.partial(_dsa_kernel, k_sel=k_sel),
        out_shape=jax.ShapeDtypeStruct((batch, t, C_OUT), BF16),
        grid=(batch, t // LANES),
        in_specs=[qblock(IDX_HEADS * IDX_DIM), qblock(16), qblock(C_OUT),
                  pl.BlockSpec((None, t, LANES), lambda b, i: (b, 0, 0)),
                  pl.BlockSpec((None, nkt, HEAD_DIM, tk), lambda b, i: (b, 0, 0, 0)),
                  _const_spec(tab.shape), _const_spec(tri.shape)],
        out_specs=pl.BlockSpec((None, LANES, C_OUT), lambda b, i: (b, i, 0)),
        scratch_shapes=[pltpu.VMEM((t, LANES), F32)],
        compiler_params=_cparams(("parallel", "arbitrary")),
        name="dsa_attention",
    )(iqt, iwt, cqt, ck, cvt, tab, tri).reshape(batch * t, C_OUT)


def _merge_kernel(x_ref, *refs):
    ng = len(DIL_GROUPS)
    a_refs = refs[:2 * ng]
    ob_ref, oc_ref, gate_ref, wa_ref, wb_ref, wc_ref, wo_ref, out_ref = refs[2 * ng:-1]
    shuffle_ref = refs[-1]
    tm, d = x_ref.shape

    def token_order(ref, slot):
        dil = ref.shape[0]
        if dil == 1:
            return ref[0]
        halves = range(A_OUT // LANES)
        for r in range(dil):
            for half in halves:
                shuffle_ref[slot, half, pl.ds(r, tm // dil, stride=dil), :] = ref[r, :, half * LANES:(half + 1) * LANES]
        return jnp.concatenate([shuffle_ref[slot, half] for half in halves], axis=1)

    outs = [token_order(a_refs[2 * g], 2 * g) for g in range(ng)]
    lses = [token_order(a_refs[2 * g + 1], 2 * g + 1) for g in range(ng)]
    top = functools.reduce(jnp.maximum, lses)
    es = [jnp.exp(lse - top) for lse in lses]
    num = sum(e * o for e, o in zip(es, outs))
    oa = (num / sum(es)).astype(BF16)
    y = gate_ref[:, 0:d].astype(F32) * jnp.dot(oa, wa_ref[...], preferred_element_type=F32)
    y = y + gate_ref[:, d:2 * d].astype(F32) * jnp.dot(ob_ref[...], wb_ref[...], preferred_element_type=F32)
    y = y + gate_ref[:, 2 * d:3 * d].astype(F32) * jnp.dot(oc_ref[...], wc_ref[...], preferred_element_type=F32)
    out_ref[...] = x_ref[...] + jnp.dot(y.astype(BF16), wo_ref[...], preferred_element_type=F32)


def _merge(x, a_parts, ob, oc, gates, wa, wb, wc, wo, t):
    n, d = x.shape
    tm = TOKEN_TILE
    per_batch = t // tm
    tok = lambda c: pl.BlockSpec((tm, c), lambda i: (i, 0))
    by_residue = lambda dil: pl.BlockSpec((None, dil, tm // dil, A_OUT),
                                          lambda i: (i // per_batch, 0, i % per_batch, 0))
    ws = [w.astype(BF16) for w in (wa, wb, wc, wo)]
    return pl.pallas_call(
        _merge_kernel,
        out_shape=jax.ShapeDtypeStruct((n, d), F32),
        grid=(n // tm,),
        in_specs=[tok(d)] + [by_residue(z.shape[1]) for z in a_parts] + [tok(B_OUT), tok(C_OUT), tok(3 * d)]
                 + [_const_spec(w.shape) for w in ws],
        out_specs=tok(d),
        scratch_shapes=[pltpu.VMEM((len(a_parts), A_OUT // LANES, tm, LANES), F32)],
        compiler_params=_cparams(("parallel",)),
        name="merge",
    )(x, *a_parts, ob, oc, gates, *ws)


def _token_mixer(x, batch, t, layer, mix_norm, w_in, qk_gain, diff_lambda, diff_out_norm,
                 w_branch_a, w_branch_b, w_branch_c, w_out, band_tabs, tab_b, tab_c):
    ng = len(DIL_GROUPS)
    outs = _project(x, mix_norm, w_in, qk_gain, batch, t)
    a_in, (bk, ck, gates, bqt, bvt, cqt, cvt, iqt, iwt) = outs[:3 * ng], outs[3 * ng:]
    a_parts = []
    for g, (_, dilation) in enumerate(DIL_GROUPS):
        a_parts += _dilated_group(a_in[g], a_in[ng + g], a_in[2 * ng + g], band_tabs[g], dilation)
    lam_init = 0.8 - 0.6 * np.exp(-0.3 * layer)
    lam_rows = jnp.concatenate([diff_lambda.astype(F32), jnp.full((4, HEAD_DIM), lam_init, F32)], axis=0)
    gn = jnp.broadcast_to((diff_out_norm.astype(F32) * (1.0 - lam_init))[:, None], (B_V_DIM, B_Q_TILE))
    ob = _diff_attention(bqt, bk, bvt, tab_b, lam_rows, gn, batch, t)
    oc = _dsa_attention(iqt, iwt, cqt, ck, cvt, tab_c, batch, t)
    return _merge(x, a_parts, ob, oc, gates, w_branch_a, w_branch_b, w_branch_c, w_out, t)


def kernel(x, rel_bias, ffn1_norm, ffn1_w_gate, ffn1_w_up, ffn1_w_down, mix_norm, w_in, qk_gain,
           diff_lambda, diff_out_norm, w_branch_a, w_branch_b, w_branch_c, w_out,
           ffn2_norm, ffn2_w_gate, ffn2_w_up, ffn2_w_down):
    batch, t, d = x.shape
    depth = w_in.shape[0]
    assert t % (DIL_GROUPS[-1][1] * LANES) == 0 and t % TOKEN_TILE == 0
    band_tabs = [_band_tables(rel_bias[:, g * A_GROUP_HEADS:(g + 1) * A_GROUP_HEADS], dil)
                 for g, (_, dil) in enumerate(DIL_GROUPS)]
    tab_b = _toeplitz_tables(rel_bias[:, A_HEADS:A_HEADS + B_HEADS])
    tab_c = _toeplitz_tables(rel_bias[:, A_HEADS + B_HEADS:])
    h = x.reshape(batch * t, d).astype(F32)
    for i in range(depth):
        h = _ffn(h, ffn1_norm[i], ffn1_w_gate[i], ffn1_w_up[i], ffn1_w_down[i])
        h = _token_mixer(h, batch, t, i, mix_norm[i], w_in[i], qk_gain[i], diff_lambda[i], diff_out_norm[i],
                         w_branch_a[i], w_branch_b[i], w_branch_c[i], w_out[i], band_tabs, tab_b, tab_c)
        h = _ffn(h, ffn2_norm[i], ffn2_w_gate[i], ffn2_w_up[i], ffn2_w_down[i])
    return h.reshape(batch, t, d).astype(x.dtype)
```

```python
import functools

import numpy as np
import jax
import jax.numpy as jnp
from jax import lax
from jax.experimental import pallas as pl
from jax.experimental.pallas import tpu as pltpu

F32 = jnp.float32
BF16 = jnp.bfloat16

HEAD_DIM = 64
DIL_GROUPS = ((128, 1), (512, 4), (2048, 16))
A_GROUP_HEADS = 4
A_HEADS = A_GROUP_HEADS * len(DIL_GROUPS)
A_OUT = A_GROUP_HEADS * HEAD_DIM
B_HEADS = 4
B_V_DIM = 2 * HEAD_DIM
B_OUT = B_HEADS * B_V_DIM
C_HEADS = 4
C_OUT = C_HEADS * HEAD_DIM
IDX_HEADS = 8
IDX_DIM = 64
TOPK_MAX = 256
NUM_BUCKETS = 32
MAX_DISTANCE = 2048
RMS_EPS = 1e-6
LOG2E = 1.4426950408889634

LANES = 128
TOKEN_TILE = 512
B_Q_TILE = 512
MAX_LAG = 60.0
FFN_CHUNK = 256
VMEM_LIMIT = 58 * 1024 * 1024

NEG = -1e30
M_INIT = -1e29
BIG = 1e30
THR_ALL = -1e29
N_BISECT = 20


def _cparams(sem):
    return pltpu.CompilerParams(dimension_semantics=sem, vmem_limit_bytes=VMEM_LIMIT)


def _const_spec(shape):
    nd = len(shape)
    return pl.BlockSpec(shape, lambda *_: (0,) * nd, pipeline_mode=pl.Buffered(1))


def _rel_bucket_np(dist):
    n = np.maximum(dist, 0)
    max_exact = NUM_BUCKETS // 2
    nf = np.maximum(n, 1).astype(np.float64)
    large = max_exact + (np.log(nf / max_exact) / np.log(MAX_DISTANCE / max_exact)
                         * (NUM_BUCKETS - max_exact)).astype(np.int64)
    large = np.minimum(large, NUM_BUCKETS - 1)
    return np.where(n < max_exact, n, large)


def _far_delta():
    d = 1
    while not np.all(_rel_bucket_np(np.arange(d * LANES - LANES + 1, d * LANES + LANES)) == NUM_BUCKETS - 1):
        d += 1
    return d


FAR = _far_delta()
MASKED = FAR + 1


def _toeplitz(w, n_rows, n_cols):
    period = n_rows + n_cols
    w = jnp.pad(w, ((0, 0), (0, period - w.shape[1])))
    m = jnp.tile(w, (1, n_rows))[:, :n_rows * (period - 1)].reshape(-1, n_rows, period - 1)
    return m[:, :, n_rows - 1:n_rows - 1 + n_cols]


def _bias_by_distance(bias_heads, dist, valid):
    vals = jnp.take(bias_heads.astype(F32), jnp.asarray(_rel_bucket_np(dist), jnp.int32), axis=0).T
    return jnp.where(jnp.asarray(valid)[None], vals, NEG)


def _toeplitz_tables(bias_heads):
    n_cols = (FAR + 1) * LANES
    dist = np.arange(LANES - 1 + n_cols) - (LANES - 1)
    tiles = _toeplitz(_bias_by_distance(bias_heads, dist, dist >= 0), LANES, n_cols)
    tiles = tiles.reshape(-1, LANES, FAR + 1, LANES).transpose(0, 2, 1, 3)
    masked = jnp.full((tiles.shape[0], 1, LANES, LANES), NEG, F32)
    return jnp.concatenate([tiles, masked], axis=1)


def _band_tables(bias_heads, dilation):
    wn = LANES
    sub = np.arange(3 * wn - 1) - (wn - 1)
    w = _bias_by_distance(bias_heads, sub * dilation, (sub >= 0) & (sub <= wn))
    later = jnp.flip(_toeplitz(w, wn, 2 * wn), axis=(1, 2))
    first = jnp.where(jnp.asarray(np.arange(2 * wn) >= wn)[None, None], later, NEG)
    return jnp.stack([first, later])


def _ffn_kernel(x_ref, g_ref, wg_ref, wu_ref, wd_ref, o_ref, acc_ref):
    x = x_ref[...]
    ms = jnp.mean(x * x, axis=-1, keepdims=True)
    h = (x * lax.rsqrt(ms + RMS_EPS) * g_ref[...]).astype(BF16)
    acc_ref[...] = jnp.zeros_like(acc_ref)

    def body(c, carry):
        cols = pl.ds(pl.multiple_of(c * FFN_CHUNK, FFN_CHUNK), FFN_CHUNK)
        g = jnp.dot(h, wg_ref[:, cols], preferred_element_type=F32)
        u = jnp.dot(h, wu_ref[:, cols], preferred_element_type=F32)
        a = (g * jax.nn.sigmoid(g) * u).astype(BF16)
        acc_ref[...] += jnp.dot(a, wd_ref[cols, :], preferred_element_type=F32)
        return carry

    lax.fori_loop(0, wg_ref.shape[1] // FFN_CHUNK, body, 0)
    o_ref[...] = x + 0.5 * acc_ref[...]


def _ffn(x, gain, w_gate, w_up, w_down):
    n, d = x.shape
    f = w_gate.shape[1]
    tm = TOKEN_TILE
    return pl.pallas_call(
        _ffn_kernel,
        out_shape=jax.ShapeDtypeStruct((n, d), F32),
        grid=(n // tm,),
        in_specs=[pl.BlockSpec((tm, d), lambda i: (i, 0)),
                  _const_spec((1, d)), _const_spec((d, f)), _const_spec((d, f)), _const_spec((f, d))],
        out_specs=pl.BlockSpec((tm, d), lambda i: (i, 0)),
        scratch_shapes=[pltpu.VMEM((tm, d), F32)],
        compiler_params=_cparams(("parallel",)),
        name="ffn",
    )(x, gain.reshape(1, d).astype(F32), w_gate.astype(BF16), w_up.astype(BF16), w_down.astype(BF16))


S_AQ, S_AK, S_AV = 0, 768, 1536
S_BK, S_CK, S_GATE, S_END = 2304, 2816, 2944, 6016
T_BQ, T_BV, T_CQ, T_CV, T_IQ, T_IW, T_END = 0, 512, 1024, 1280, 1344, 1856, 1872


def _proj_kernel(x_ref, g_ref, ws_ref, wt_ref, bd_ref, gs_ref, gt_ref, *refs):
    ng = len(DIL_GROUPS)
    a_refs = refs[:3 * ng]
    bk_ref, ck_ref, gate_ref, bqt_ref, bvt_ref, cqt_ref, cvt_ref, iqt_ref, iwt_ref = refs[3 * ng:-1]
    shuffle_ref = refs[-1]
    tm = x_ref.shape[0]

    def store_by_residue(y, which):
        for g, (_, dil) in enumerate(DIL_GROUPS):
            out = a_refs[which * ng + g]
            part = y[:, g * A_OUT:(g + 1) * A_OUT]
            if dil == 1:
                out[0] = part.astype(BF16)
            else:
                for half in range(A_OUT // LANES):
                    shuffle_ref[half] = part[:, half * LANES:(half + 1) * LANES]
                for r in range(dil):
                    out[r] = jnp.concatenate(
                        [shuffle_ref[half, pl.ds(r, tm // dil, stride=dil), :] for half in range(A_OUT // LANES)],
                        axis=1).astype(BF16)

    x = x_ref[...]
    ms = jnp.mean(x * x, axis=-1, keepdims=True)
    h = (x * lax.rsqrt(ms + RMS_EPS) * g_ref[...]).astype(BF16)
    bd = bd_ref[...]

    def dot_s(c0, c1):
        return jnp.dot(h, ws_ref[:, c0:c1], preferred_element_type=F32)

    def head_inv_rms(y):
        outs = []
        for c in range(y.shape[1] // LANES):
            sq = y[:, c * LANES:(c + 1) * LANES]
            sq = sq * sq
            hi = sq.astype(BF16)
            lo = (sq - hi.astype(F32)).astype(BF16)
            msq = (jnp.dot(hi, bd, preferred_element_type=F32)
                   + jnp.dot(lo, bd, preferred_element_type=F32))
            outs.append(lax.rsqrt(msq + RMS_EPS))
        return outs[0] if len(outs) == 1 else jnp.concatenate(outs, axis=1)

    y = dot_s(S_AQ, S_AK)
    store_by_residue(y * head_inv_rms(y) * gs_ref[:, 0:768], 0)
    y = dot_s(S_AK, S_AV)
    store_by_residue(y * head_inv_rms(y) * gs_ref[:, 768:1536], 1)
    store_by_residue(dot_s(S_AV, S_BK), 2)
    y = dot_s(S_BK, S_CK)
    bk_ref[...] = (y * head_inv_rms(y) * gs_ref[:, 1536:2048]).astype(BF16)
    y = dot_s(S_CK, S_GATE)
    lane = lax.broadcasted_iota(jnp.int32, y.shape, 1)
    inv = jnp.where(lane < HEAD_DIM, head_inv_rms(y), 1.0)
    ck_ref[...] = (y * inv * gs_ref[:, 2048:2176]).astype(BF16)
    for c in range(3):
        y = dot_s(S_GATE + c * 1024, S_GATE + (c + 1) * 1024)
        gate_ref[:, c * 1024:(c + 1) * 1024] = jax.nn.sigmoid(y).astype(BF16)

    def dot_t(r0, r1):
        return lax.dot_general(wt_ref[r0:r1, :], h, (((1,), (1,)), ((), ())),
                               preferred_element_type=F32)

    def norm_t(y, gain):
        r = y.shape[0] // HEAD_DIM
        y3 = y.reshape(r, HEAD_DIM, tm)
        msq = jnp.mean(y3 * y3, axis=1, keepdims=True)
        return (y3 * lax.rsqrt(msq + RMS_EPS)).reshape(r * HEAD_DIM, tm) * gain

    bqt_ref[...] = norm_t(dot_t(T_BQ, T_BV), gt_ref[0:512, :]).astype(BF16)
    bvt_ref[...] = dot_t(T_BV, T_CQ).astype(BF16)
    cqt_ref[...] = norm_t(dot_t(T_CQ, T_CV), gt_ref[512:768, :]).astype(BF16)
    cvt_ref[...] = dot_t(T_CV, T_IQ).astype(BF16)
    iqt_ref[...] = dot_t(T_IQ, T_IW).astype(BF16)
    iwt_ref[...] = dot_t(T_IW, T_END) * (IDX_HEADS ** -0.5 * IDX_DIM ** -0.5)


def _proj_weights(w_in, qk_gain, tm):
    d = w_in.shape[0]
    o = 0
    a_qkv = w_in[:, o:o + 3 * A_HEADS * HEAD_DIM].reshape(d, 3, A_HEADS * HEAD_DIM)
    o += 3 * A_HEADS * HEAD_DIM
    b_qk = w_in[:, o:o + 4 * B_HEADS * HEAD_DIM].reshape(d, 4, B_HEADS, HEAD_DIM)
    o += 4 * B_HEADS * HEAD_DIM
    b_v = w_in[:, o:o + B_OUT]
    o += B_OUT
    c_q = w_in[:, o:o + C_OUT]
    c_k = w_in[:, o + C_OUT:o + C_OUT + HEAD_DIM]
    c_v = w_in[:, o + C_OUT + HEAD_DIM:o + C_OUT + 2 * HEAD_DIM]
    o += C_OUT + 2 * HEAD_DIM
    i_q = w_in[:, o:o + IDX_HEADS * IDX_DIM]
    i_k = w_in[:, o + IDX_HEADS * IDX_DIM:o + IDX_HEADS * IDX_DIM + IDX_DIM]
    i_w = w_in[:, o + IDX_HEADS * IDX_DIM + IDX_DIM:o + IDX_HEADS * IDX_DIM + IDX_DIM + IDX_HEADS]
    o += IDX_HEADS * IDX_DIM + IDX_DIM + IDX_HEADS
    gates = w_in[:, o:]
    b_k = jnp.stack([b_qk[:, 2], b_qk[:, 3]], axis=2).reshape(d, 2 * B_HEADS * HEAD_DIM)
    b_q = jnp.stack([b_qk[:, 0], b_qk[:, 1]], axis=2).reshape(d, 2 * B_HEADS * HEAD_DIM)
    w_s = jnp.concatenate([a_qkv[:, 0], a_qkv[:, 1], a_qkv[:, 2], b_k, c_k, i_k, gates], axis=1)
    w_t = jnp.concatenate([b_q, b_v, c_q, c_v, i_q, i_w, jnp.zeros((d, 8), w_in.dtype)], axis=1).T
    assert w_s.shape[1] == S_END and w_t.shape[0] == T_END
    scale = HEAD_DIM ** -0.5
    g = qk_gain.astype(F32)
    gs = jnp.concatenate([jnp.tile(g[0, 0] * scale, A_HEADS), jnp.tile(g[0, 1], A_HEADS),
                          jnp.tile(g[1, 1], 2 * B_HEADS), g[2, 1], jnp.ones((IDX_DIM,), F32)])[None]
    gt = jnp.concatenate([jnp.tile(g[1, 0] * (scale * LOG2E), 2 * B_HEADS),
                          jnp.tile(g[2, 0] * (scale * LOG2E), C_HEADS)])
    gt = jnp.broadcast_to(gt[:, None], (gt.shape[0], tm))
    return w_s.astype(BF16), w_t.astype(BF16), gs, gt


def _head_block_diag():
    r = np.arange(LANES)
    return jnp.asarray((r[:, None] // HEAD_DIM == r[None, :] // HEAD_DIM) / HEAD_DIM, BF16)


def _project(x, gain, w_in, qk_gain, batch, t):
    n, d = x.shape
    tm = TOKEN_TILE
    nt = n // tm
    per_batch = t // tm
    w_s, w_t, gs, gt = _proj_weights(w_in, qk_gain, tm)
    tok = lambda c: pl.BlockSpec((tm, c), lambda i: (i, 0))
    feat = lambda r: pl.BlockSpec((None, r, tm), lambda i: (i, 0, 0))
    a_shapes, a_specs = [], []
    for _ in range(3):
        for _, dil in DIL_GROUPS:
            a_shapes.append(jax.ShapeDtypeStruct((batch, dil, t // dil, A_OUT), BF16))
            a_specs.append(pl.BlockSpec((None, dil, tm // dil, A_OUT),
                                        lambda i: (i // per_batch, 0, i % per_batch, 0)))
    out_shape = a_shapes + [
        jax.ShapeDtypeStruct((n, 512), BF16), jax.ShapeDtypeStruct((n, 128), BF16),
        jax.ShapeDtypeStruct((n, 3072), BF16),
        jax.ShapeDtypeStruct((nt, 512, tm), BF16), jax.ShapeDtypeStruct((nt, 512, tm), BF16),
        jax.ShapeDtypeStruct((nt, 256, tm), BF16), jax.ShapeDtypeStruct((nt, 64, tm), BF16),
        jax.ShapeDtypeStruct((nt, 512, tm), BF16), jax.ShapeDtypeStruct((nt, 16, tm), F32)]
    out_specs = a_specs + [tok(512), tok(128), tok(3072),
                           feat(512), feat(512), feat(256), feat(64), feat(512), feat(16)]
    return pl.pallas_call(
        _proj_kernel,
        out_shape=out_shape,
        grid=(nt,),
        in_specs=[tok(d), _const_spec((1, d)), _const_spec(w_s.shape), _const_spec(w_t.shape),
                  _const_spec((LANES, LANES)), _const_spec(gs.shape), _const_spec(gt.shape)],
        out_specs=out_specs,
        scratch_shapes=[pltpu.VMEM((A_OUT // LANES, tm, LANES), F32)],
        compiler_params=_cparams(("parallel",)),
        name="proj",
    )(x, gain.reshape(1, d).astype(F32), w_s, w_t, _head_block_diag(), gs, gt)


def _dil_kernel(q_ref, kp_ref, kc_ref, vp_ref, vc_ref, bias_ref, o_ref, lse_ref):
    nq = q_ref.shape[0] // LANES
    qi = pl.program_id(2)
    lane = lax.broadcasted_iota(jnp.int32, (LANES, A_OUT), 1) // HEAD_DIM
    mine = [lane == h for h in range(A_GROUP_HEADS)]
    blocks = [slice(jb * LANES, (jb + 1) * LANES) for jb in range(nq)]

    def band(prev_ref, cur_ref, jb):
        if jb == 0:
            return jnp.concatenate([prev_ref[...], cur_ref[blocks[0], :]], axis=0)
        return cur_ref[(jb - 1) * LANES:(jb + 1) * LANES, :]

    logits = []
    for jb in range(nq):
        q = q_ref[blocks[jb], :]
        q4 = jnp.concatenate([jnp.where(mine[h], q, jnp.zeros_like(q)) for h in range(A_GROUP_HEADS)], axis=0)
        s = lax.dot_general(q4, band(kp_ref, kc_ref, jb), (((1,), (1,)), ((), ())), preferred_element_type=F32)
        bias = bias_ref[jnp.minimum(qi, 1) if jb == 0 else 1]
        logits.append(s + bias.reshape(A_GROUP_HEADS * LANES, 2 * LANES))
    probs, stats = [], []
    for s in logits:
        m = jnp.max(s, axis=1, keepdims=True)
        p = jnp.exp(s - m)
        ssum = jnp.sum(p, axis=1, keepdims=True)
        probs.append(p.astype(BF16))
        stats.append((1.0 / ssum, m + jnp.log(ssum)))
    for jb in range(nq):
        pv = jnp.dot(probs[jb], band(vp_ref, vc_ref, jb), preferred_element_type=F32)
        inv, lse4 = stats[jb]
        o = jnp.zeros((LANES, A_OUT), F32)
        lse = jnp.zeros((LANES, A_OUT), F32)
        for h in range(A_GROUP_HEADS):
            head = slice(h * LANES, (h + 1) * LANES)
            o = jnp.where(mine[h], pv[head] * inv[head], o)
            lse = jnp.where(mine[h], lse4[head], lse)
        o_ref[blocks[jb], :] = o
        lse_ref[blocks[jb], :] = lse


def _dilated_group(aq, ak, av, bias, dilation):
    batch, _, n, _ = aq.shape
    nblk = n // LANES
    nq = min(nblk, 4)
    qt = nq * LANES
    cur = pl.BlockSpec((None, None, qt, A_OUT), lambda b, r, i: (b, r, i, 0))
    prev = pl.BlockSpec((None, None, LANES, A_OUT), lambda b, r, i: (b, r, jnp.maximum(i * nq - 1, 0), 0))
    shp = jax.ShapeDtypeStruct((batch, dilation, n, A_OUT), F32)
    return pl.pallas_call(
        _dil_kernel,
        out_shape=[shp, shp],
        grid=(batch, dilation, nblk // nq),
        in_specs=[cur, prev, cur, prev, cur, _const_spec(bias.shape)],
        out_specs=[cur, cur],
        compiler_params=_cparams(("parallel", "parallel", "parallel")),
        name=f"dilated_d{dilation}",
    )(aq, ak, ak, av, av, bias)


def _fold_rows(x, op):
    r, c = x.shape
    return op(x.reshape(r // 64, 64, c), axis=0) if r > 64 else x


def _reduce_rows(x, op):
    x = _fold_rows(x, op)
    x = op(x.reshape(8, 8, x.shape[1]), axis=0)
    return op(x, axis=0, keepdims=True)


def _bias_tile(tab_ref, head, qblk, kblk):
    delta = qblk - kblk
    idx = jnp.where(delta < 0, MASKED, jnp.minimum(delta, FAR))
    if head is None:
        return tab_ref[idx]
    return tab_ref[head, idx]


def _diff_kernel(qt_ref, k_ref, vt_ref, tab_ref, lam_ref, gn_ref, o_ref):
    tq = qt_ref.shape[1]
    tk = TOKEN_TILE
    qi = pl.program_id(2)
    qt = qt_ref[...]
    row = lax.broadcasted_iota(jnp.int32, qt.shape, 0)
    q12 = jnp.concatenate([jnp.where(row < HEAD_DIM, qt, jnp.zeros_like(qt)),
                           jnp.where(row >= HEAD_DIM, qt, jnp.zeros_like(qt))], axis=1)
    nqb = tq // LANES
    nkb = tk // LANES

    def logits(c):
        kc = k_ref[pl.ds(pl.multiple_of(c * tk, tk), tk), :]
        bias = jnp.concatenate(
            [jnp.concatenate([_bias_tile(tab_ref, None, qi * nqb + iq, c * nkb + jk)
                              for iq in range(nqb)] * 2, axis=1) for jk in range(nkb)], axis=0)
        return jnp.dot(kc, q12, preferred_element_type=F32) + bias

    def exact_step(c, carry):
        m, l, acc = carry
        s = logits(c)
        m_new = jnp.maximum(m, _reduce_rows(s, jnp.max))
        alpha = jnp.exp2(m - m_new)
        p = jnp.exp2(s - m_new)
        l = alpha * l + _reduce_rows(p, jnp.sum)
        acc = alpha * acc + jnp.dot(vt_ref[c], p.astype(BF16), preferred_element_type=F32)
        return m_new, l, acc

    def lagged_update(c, s, m, l, acc, jump):
        p = jnp.exp2(s - m)
        top =_reduce_rows(s, jnp.max)
        l = l + _reduce_rows(p, jnp.sum)
        acc = acc + jnp.dot(vt_ref[c], p.astype(BF16), preferred_element_type=F32)
        m_new = jnp.maximum(m, top)
        alpha = jnp.exp2(m - m_new)
        return m_new, alpha * l, alpha * acc, jnp.maximum(jump, top - m)

    nch = ((qi + 1) * tq + tk - 1) // tk
    zero = jnp.zeros((1, 2 * tq), F32)
    acc0 = jnp.zeros((B_V_DIM, 2 * tq), F32)
    s0 = logits(0)
    m0 = jnp.max(s0[0:8], axis=0, keepdims=True)
    state = lagged_update(0, s0, m0, zero, acc0, zero)
    _, l, acc, jump = lax.fori_loop(1, nch, lambda c, carry: lagged_update(c, logits(c), *carry), state)
    l, acc = lax.cond(jnp.max(jump) > MAX_LAG,
                      lambda: lax.fori_loop(0, nch, exact_step, (jnp.full((1, 2 * tq), M_INIT, F32), zero, acc0))[1:],
                      lambda: (l, acc))
    a1, a2 = acc[:, :tq], acc[:, tq:]
    l1, l2 = l[:, :tq], l[:, tq:]

    lv = lam_ref[...]
    lam = (jnp.exp(jnp.sum(lv[0:1] * lv[1:2], axis=1, keepdims=True))
           - jnp.exp(jnp.sum(lv[2:3] * lv[3:4], axis=1, keepdims=True)) + lv[4:5, 0:1])
    o = a1 / l1 - lam * (a2 / l2)
    ms = jnp.mean(o * o, axis=0, keepdims=True)
    o = o * lax.rsqrt(ms + RMS_EPS) * gn_ref[...]
    o_ref[...] = o.T.astype(BF16)


def _diff_attention(bqt, bk, bvt, tab, lam_rows, gn, batch, t):
    tq = B_Q_TILE
    tk = TOKEN_TILE
    per = tk // tq
    nkt = t // tk
    bqt = bqt.reshape(batch, nkt, B_HEADS * LANES, tk)
    bvt = bvt.reshape(batch, nkt, B_OUT, tk)
    bk = bk.reshape(batch, t, B_HEADS * LANES)
    return pl.pallas_call(
        _diff_kernel,
        out_shape=jax.ShapeDtypeStruct((batch, t, B_OUT), BF16),
        grid=(batch, B_HEADS, t // tq),
        in_specs=[pl.BlockSpec((None, None, LANES, tq), lambda b, h, i: (b, i // per, h, i % per)),
                  pl.BlockSpec((None, t, LANES), lambda b, h, i: (b, 0, h)),
                  pl.BlockSpec((None, nkt, B_V_DIM, tk), lambda b, h, i: (b, 0, h, 0)),
                  pl.BlockSpec((None, FAR + 2, LANES, LANES), lambda b, h, i: (h, 0, 0, 0)),
                  _const_spec(lam_rows.shape), _const_spec(gn.shape)],
        out_specs=pl.BlockSpec((None, tq, B_V_DIM), lambda b, h, i: (b, i, h)),
        compiler_params=_cparams(("parallel", "parallel", "arbitrary")),
        name="diff_attention",
    )(bqt, bk, bvt, tab, lam_rows, gn).reshape(batch * t, B_OUT)


def _dsa_kernel(iqt_ref, iwt_ref, cqt_ref, k_ref, vt_ref, tab_ref, tri_ref, o_ref, s_ref, *, k_sel):
    tk = TOKEN_TILE
    nkb = tk // LANES
    qi = pl.program_id(1)
    nch = qi // nkb + 1
    qpos = qi * LANES + lax.broadcasted_iota(jnp.int32, (1, LANES), 1)
    zeros = jnp.zeros((HEAD_DIM, LANES), BF16)
    iq = iqt_ref[...]
    w = iwt_ref[...]
    iq_all = jnp.concatenate([jnp.concatenate([zeros, iq[h * IDX_DIM:(h + 1) * IDX_DIM]], axis=0)
                              for h in range(IDX_HEADS)], axis=1)
    cq = cqt_ref[...]
    cq_all = jnp.concatenate([jnp.concatenate([cq[h * HEAD_DIM:(h + 1) * HEAD_DIM], zeros], axis=0)
                              for h in range(C_HEADS)], axis=1)

    def chunk(c):
        return pl.ds(pl.multiple_of(c * tk, tk), tk)

    def score_chunk(c, mn, mx, last):
        raw = jnp.dot(k_ref[chunk(c), :], iq_all, preferred_element_type=F32)
        acc = w[0:1, :] * jnp.maximum(raw[:, 0:LANES], 0.0)
        for h in range(1, IDX_HEADS):
            acc = acc + w[h:h + 1, :] * jnp.maximum(raw[:, h * LANES:(h + 1) * LANES], 0.0)
        if last:
            kpos = c * tk + lax.broadcasted_iota(jnp.int32, (tk, LANES), 0)
            causal = kpos <= qpos
            s_ref[chunk(c), :] = jnp.where(causal, acc, NEG)
            mn = jnp.minimum(mn, _fold_rows(jnp.where(causal, acc, BIG), jnp.min))
            mx = jnp.maximum(mx, _fold_rows(jnp.where(causal, acc, NEG), jnp.max))
        else:
            s_ref[chunk(c), :] = acc
            mn = jnp.minimum(mn, _fold_rows(acc, jnp.min))
            mx = jnp.maximum(mx, _fold_rows(acc, jnp.max))
        return mn, mx

    mn, mx = lax.fori_loop(0, nch - 1, lambda c, carry: score_chunk(c, *carry, last=False),
                           (jnp.full((64, LANES), BIG, F32), jnp.full((64, LANES), NEG, F32)))
    mn, mx = score_chunk(nch - 1, mn, mx, last=True)
    lo, hi = _reduce_rows(mn, jnp.min), _reduce_rows(mx, jnp.max)

    def count(thr):
        def body(c, carry):
            gt, ge = carry
            s = s_ref[chunk(c), :]
            gt = gt + _fold_rows(jnp.where(s > thr, 1.0, 0.0), jnp.sum)
            ge = ge + _fold_rows(jnp.where(s >= thr, 1.0, 0.0), jnp.sum)
            return gt, ge
        z = jnp.zeros((64, LANES), F32)
        gt, ge = lax.fori_loop(0, nch, body, (z, z))
        return _reduce_rows(gt, jnp.sum), _reduce_rows(ge, jnp.sum)

    def count_gt(thr):
        def body(c, gt):
            return gt + _fold_rows(jnp.where(s_ref[chunk(c), :] > thr, 1.0, 0.0), jnp.sum)
        return _reduce_rows(lax.fori_loop(0, nch, body, jnp.zeros((64, LANES), F32)), jnp.sum)

    def max_below(bound, strict):
        def body(c, mx):
            s = s_ref[chunk(c), :]
            keep = (s < bound) if strict else (s <= bound)
            return jnp.maximum(mx, _fold_rows(jnp.where(keep, s, NEG), jnp.max))
        return _reduce_rows(lax.fori_loop(0, nch, body, jnp.full((64, LANES), NEG, F32)), jnp.max)

    kf = float(k_sel)
    need = qpos >= k_sel

    def bisect(_, carry):
        lo, hi = carry
        mid = lo + (hi - lo) * 0.5
        below = count_gt(mid) < kf
        return jnp.where(below, lo, mid), jnp.where(below, mid, hi)

    lo, hi = lax.fori_loop(0, N_BISECT, bisect, (lo, hi))

    cand = max_below(hi, strict=False)
    gt, ge = count(cand)

    def walk_cond(carry):
        _, _, ge = carry
        return jnp.max(jnp.where(need & (ge < kf), 1.0, 0.0)) > 0.0

    def walk_body(carry):
        cand, gt, ge = carry
        nxt = max_below(cand, strict=True)
        cand = jnp.where(ge < kf, nxt, cand)
        gt, ge = count(cand)
        return cand, gt, ge

    cand, gt, ge = lax.while_loop(walk_cond, walk_body, (cand, gt, ge))
    thr = jnp.where(need, cand, THR_ALL)
    want_eq = jnp.where(need, kf - gt, 0.0)

    any_tie = jnp.max(jnp.where(need & (ge > kf), 1.0, 0.0)) > 0.0

    def mark_with_ties(c, eq_seen):
        s = s_ref[chunk(c), :]
        eq = jnp.where(s == thr, 1.0, 0.0)
        rank = eq_seen + jnp.dot(tri_ref[...], eq.astype(BF16), preferred_element_type=F32)
        keep = jnp.where(s > thr, 1.0, jnp.where(rank <= want_eq, eq, 0.0))
        s_ref[chunk(c), :] = jnp.where(keep > 0.5, 0.0, NEG)
        return eq_seen + _reduce_rows(eq, jnp.sum)

    def mark_no_ties(c, carry):
        s_ref[chunk(c), :] = jnp.where(s_ref[chunk(c), :] >= thr, 0.0, NEG)
        return carry

    @pl.when(any_tie)
    def _():
        lax.fori_loop(0, nch, mark_with_ties, jnp.zeros((1, LANES), F32))

    @pl.when(jnp.logical_not(any_tie))
    def _():
        lax.fori_loop(0, nch, mark_no_ties, 0)

    def masked_logits(c):
        sel = s_ref[chunk(c), :]
        bias = jnp.concatenate(
            [jnp.concatenate([_bias_tile(tab_ref, h, qi, c * nkb + jk) for jk in range(nkb)], axis=0) + sel
             for h in range(C_HEADS)], axis=1)
        return jnp.dot(k_ref[chunk(c), :], cq_all, preferred_element_type=F32) + bias

    def exact_step(c, carry):
        m, l, acc = carry
        lg = masked_logits(c)
        m_new = jnp.maximum(m, _reduce_rows(lg, jnp.max))
        alpha = jnp.exp2(m - m_new)
        p = jnp.exp2(lg - m_new)
        l = alpha * l + _reduce_rows(p, jnp.sum)
        acc = alpha * acc + jnp.dot(vt_ref[c], p.astype(BF16), preferred_element_type=F32)
        return m_new, l, acc

    def lagged_step(c, carry):
        m, l, acc, jump = carry
        lg = masked_logits(c)
        p = jnp.exp2(lg - m)
        top = _reduce_rows(lg, jnp.max)
        l = l + _reduce_rows(p, jnp.sum)
        acc = acc + jnp.dot(vt_ref[c], p.astype(BF16), preferred_element_type=F32)
        m_new = jnp.maximum(m, top)
        alpha = jnp.exp2(m - m_new)
        return m_new, alpha * l, alpha * acc, jnp.maximum(jump, top - m)

    wide = C_HEADS * LANES
    init = (jnp.full((1, wide), M_INIT, F32), jnp.zeros((1, wide), F32), jnp.zeros((HEAD_DIM, wide), F32))
    first = exact_step(0, init)
    _, l, acc, jump = lax.fori_loop(1, nch, lagged_step, first + (jnp.zeros((1, wide), F32),))
    l, acc = lax.cond(jnp.max(jump) > MAX_LAG,
                      lambda: lax.fori_loop(1, nch, exact_step, first)[1:],
                      lambda: (l, acc))
    o = acc / l
    ot = jnp.concatenate([o[:, h * LANES:(h + 1) * LANES] for h in range(C_HEADS)], axis=0)
    o_ref[...] = ot.T.astype(BF16)


def _dsa_attention(iqt, iwt, cqt, ck, cvt, tab, batch, t):
    tk = TOKEN_TILE
    per = tk // LANES
    nkt = t // tk
    k_sel = min(TOPK_MAX, t // 4)
    iqt = iqt.reshape(batch, nkt, IDX_HEADS * IDX_DIM, tk)
    iwt = iwt.reshape(batch, nkt, 16, tk)
    cqt = cqt.reshape(batch, nkt, C_OUT, tk)
    cvt = cvt.reshape(batch, nkt, HEAD_DIM, tk)
    ck = ck.reshape(batch, t, LANES)
    r = np.arange(tk)
    tri = jnp.asarray(r[:, None] >= r[None, :], BF16)
    qblock = lambda rows: pl.BlockSpec((None, None, rows, LANES), lambda b, i: (b, i // per, 0, i % per))
    return pl.pallas_call(
        functools.partial(_dsa_kernel, k_sel=k_sel),
        out_shape=jax.ShapeDtypeStruct((batch, t, C_OUT), BF16),
        grid=(batch, t // LANES),
        in_specs=[qblock(IDX_HEADS * IDX_DIM), qblock(16), qblock(C_OUT),
                  pl.BlockSpec((None, t, LANES), lambda b, i: (b, 0, 0)),
                  pl.BlockSpec((None, nkt, HEAD_DIM, tk), lambda b, i: (b, 0, 0, 0)),
                  _const_spec(tab.shape), _const_spec(tri.shape)],
        out_specs=pl.BlockSpec((None, LANES, C_OUT), lambda b, i: (b, i, 0)),
        scratch_shapes=[pltpu.VMEM((t, LANES), F32)],
        compiler_params=_cparams(("parallel", "arbitrary")),
        name="dsa_attention",
    )(iqt, iwt, cqt, ck, cvt, tab, tri).reshape(batch * t, C_OUT)


def _merge_kernel(x_ref, *refs):
    ng = len(DIL_GROUPS)
    a_refs = refs[:2 * ng]
    ob_ref, oc_ref, gate_ref, wa_ref, wb_ref, wc_ref, wo_ref, out_ref = refs[2 * ng:-1]
    shuffle_ref = refs[-1]
    tm, d = x_ref.shape

    def token_order(ref, slot):
        dil = ref.shape[0]
        if dil == 1:
            return ref[0]
        halves = range(A_OUT // LANES)
        for r in range(dil):
            for half in halves:
                shuffle_ref[slot, half, pl.ds(r, tm // dil, stride=dil), :] = ref[r, :, half * LANES:(half + 1) * LANES]
        return jnp.concatenate([shuffle_ref[slot, half] for half in halves], axis=1)

    outs = [token_order(a_refs[2 * g], 2 * g) for g in range(ng)]
    lses = [token_order(a_refs[2 * g + 1], 2 * g + 1) for g in range(ng)]
    top = functools.reduce(jnp.maximum, lses)
    es = [jnp.exp(lse - top) for lse in lses]
    num = sum(e * o for e, o in zip(es, outs))
    oa = (num / sum(es)).astype(BF16)
    y = gate_ref[:, 0:d].astype(F32) * jnp.dot(oa, wa_ref[...], preferred_element_type=F32)
    y = y + gate_ref[:, d:2 * d].astype(F32) * jnp.dot(ob_ref[...], wb_ref[...], preferred_element_type=F32)
    y = y + gate_ref[:, 2 * d:3 * d].astype(F32) * jnp.dot(oc_ref[...], wc_ref[...], preferred_element_type=F32)
    out_ref[...] = x_ref[...] + jnp.dot(y.astype(BF16), wo_ref[...], preferred_element_type=F32)


def _merge(x, a_parts, ob, oc, gates, wa, wb, wc, wo, t):
    n, d = x.shape
    tm = TOKEN_TILE
    per_batch = t // tm
    tok = lambda c: pl.BlockSpec((tm, c), lambda i: (i, 0))
    by_residue = lambda dil: pl.BlockSpec((None, dil, tm // dil, A_OUT),
                                          lambda i: (i // per_batch, 0, i % per_batch, 0))
    ws = [w.astype(BF16) for w in (wa, wb, wc, wo)]
    return pl.pallas_call(
        _merge_kernel,
        out_shape=jax.ShapeDtypeStruct((n, d), F32),
        grid=(n // tm,),
        in_specs=[tok(d)] + [by_residue(z.shape[1]) for z in a_parts] + [tok(B_OUT), tok(C_OUT), tok(3 * d)]
                 + [_const_spec(w.shape) for w in ws],
        out_specs=tok(d),
        scratch_shapes=[pltpu.VMEM((len(a_parts), A_OUT // LANES, tm, LANES), F32)],
        compiler_params=_cparams(("parallel",)),
        name="merge",
    )(x, *a_parts, ob, oc, gates, *ws)


def _token_mixer(x, batch, t, layer, mix_norm, w_in, qk_gain, diff_lambda, diff_out_norm,
                 w_branch_a, w_branch_b, w_branch_c, w_out, band_tabs, tab_b, tab_c):
    ng = len(DIL_GROUPS)
    outs = _project(x, mix_norm, w_in, qk_gain, batch, t)
    a_in, (bk, ck, gates, bqt, bvt, cqt, cvt, iqt, iwt) = outs[:3 * ng], outs[3 * ng:]
    a_parts = []
    for g, (_, dilation) in enumerate(DIL_GROUPS):
        a_parts += _dilated_group(a_in[g], a_in[ng + g], a_in[2 * ng + g], band_tabs[g], dilation)
    lam_init = 0.8 - 0.6 * np.exp(-0.3 * layer)
    lam_rows = jnp.concatenate([diff_lambda.astype(F32), jnp.full((4, HEAD_DIM), lam_init, F32)], axis=0)
    gn = jnp.broadcast_to((diff_out_norm.astype(F32) * (1.0 - lam_init))[:, None], (B_V_DIM, B_Q_TILE))
    ob = _diff_attention(bqt, bk, bvt, tab_b, lam_rows, gn, batch, t)
    oc = _dsa_attention(iqt, iwt, cqt, ck, cvt, tab_c, batch, t)
    return _merge(x, a_parts, ob, oc, gates, w_branch_a, w_branch_b, w_branch_c, w_out, t)


def kernel(x, rel_bias, ffn1_norm, ffn1_w_gate, ffn1_w_up, ffn1_w_down, mix_norm, w_in, qk_gain,
           diff_lambda, diff_out_norm, w_branch_a, w_branch_b, w_branch_c, w_out,
           ffn2_norm, ffn2_w_gate, ffn2_w_up, ffn2_w_down):
    batch, t, d = x.shape
    depth = w_in.shape[0]
    assert t % (DIL_GROUPS[-1][1] * LANES) == 0 and t % TOKEN_TILE == 0
    band_tabs = [_band_tables(rel_bias[:, g * A_GROUP_HEADS:(g + 1) * A_GROUP_HEADS], dil)
                 for g, (_, dil) in enumerate(DIL_GROUPS)]
    tab_b = _toeplitz_tables(rel_bias[:, A_HEADS:A_HEADS + B_HEADS] * LOG2E)
    tab_c = _toeplitz_tables(rel_bias[:, A_HEADS + B_HEADS:] * LOG2E)
    h = x.reshape(batch * t, d).astype(F32)
    for i in range(depth):
        h = _ffn(h, ffn1_norm[i], ffn1_w_gate[i], ffn1_w_up[i], ffn1_w_down[i])
        h = _token_mixer(h, batch, t, i, mix_norm[i], w_in[i], qk_gain[i], diff_lambda[i], diff_out_norm[i],
                         w_branch_a[i], w_branch_b[i], w_branch_c[i], w_out[i], band_tabs, tab_b, tab_c)
        h = _ffn(h, ffn2_norm[i], ffn2_w_gate[i], ffn2_w_up[i], ffn2_w_down[i])
    return h.reshape(batch, t, d).astype(x.dtype)
```

```python
import functools

import numpy as np
import jax
import jax.numpy as jnp
from jax import lax
from jax.experimental import pallas as pl
from jax.experimental.pallas import tpu as pltpu

F32 = jnp.float32
BF16 = jnp.bfloat16

HEAD_DIM = 64
DIL_GROUPS = ((128, 1), (512, 4), (2048, 16))
A_GROUP_HEADS = 4
A_HEADS = A_GROUP_HEADS * len(DIL_GROUPS)
A_OUT = A_GROUP_HEADS * HEAD_DIM
B_HEADS = 4
B_V_DIM = 2 * HEAD_DIM
B_OUT = B_HEADS * B_V_DIM
C_HEADS = 4
C_OUT = C_HEADS * HEAD_DIM
IDX_HEADS = 8
IDX_DIM = 64
TOPK_MAX = 256
NUM_BUCKETS = 32
MAX_DISTANCE = 2048
RMS_EPS = 1e-6
LOG2E = 1.4426950408889634

LANES = 128
TOKEN_TILE = 512
FFN_TILE = 1024
B_Q_TILE = 512
MAX_LAG = 60.0
FFN_CHUNK = 256
VMEM_LIMIT = 58 * 1024 * 1024

NEG = -1e30
M_INIT = -1e29
BIG = 1e30
THR_ALL = -1e29
N_BISECT = 20


def _cparams(sem):
    return pltpu.CompilerParams(dimension_semantics=sem, vmem_limit_bytes=VMEM_LIMIT)


def _const_spec(shape):
    nd = len(shape)
    return pl.BlockSpec(shape, lambda *_: (0,) * nd, pipeline_mode=pl.Buffered(1))


def _rel_bucket_np(dist):
    n = np.maximum(dist, 0)
    max_exact = NUM_BUCKETS // 2
    nf = np.maximum(n, 1).astype(np.float64)
    large = max_exact + (np.log(nf / max_exact) / np.log(MAX_DISTANCE / max_exact)
                         * (NUM_BUCKETS - max_exact)).astype(np.int64)
    large = np.minimum(large, NUM_BUCKETS - 1)
    return np.where(n < max_exact, n, large)


def _far_delta():
    d = 1
    while not np.all(_rel_bucket_np(np.arange(d * LANES - LANES + 1, d * LANES + LANES)) == NUM_BUCKETS - 1):
        d += 1
    return d


FAR = _far_delta()
MASKED = FAR + 1


def _toeplitz(w, n_rows, n_cols):
    period = n_rows + n_cols
    w = jnp.pad(w, ((0, 0), (0, period - w.shape[1])))
    m = jnp.tile(w, (1, n_rows))[:, :n_rows * (period - 1)].reshape(-1, n_rows, period - 1)
    return m[:, :, n_rows - 1:n_rows - 1 + n_cols]


def _bias_by_distance(bias_heads, dist, valid):
    vals = jnp.take(bias_heads.astype(F32), jnp.asarray(_rel_bucket_np(dist), jnp.int32), axis=0).T
    return jnp.where(jnp.asarray(valid)[None], vals, NEG)


def _toeplitz_tables(bias_heads):
    n_cols = (FAR + 1) * LANES
    dist = np.arange(LANES - 1 + n_cols) - (LANES - 1)
    tiles = _toeplitz(_bias_by_distance(bias_heads, dist, dist >= 0), LANES, n_cols)
    tiles = tiles.reshape(-1, LANES, FAR + 1, LANES).transpose(0, 2, 1, 3)
    masked = jnp.full((tiles.shape[0], 1, LANES, LANES), NEG, F32)
    return jnp.concatenate([tiles, masked], axis=1)


def _band_tables(bias_heads, dilation):
    wn = LANES
    sub = np.arange(3 * wn - 1) - (wn - 1)
    w = _bias_by_distance(bias_heads, sub * dilation, (sub >= 0) & (sub <= wn))
    later = jnp.flip(_toeplitz(w, wn, 2 * wn), axis=(1, 2))
    first = jnp.where(jnp.asarray(np.arange(2 * wn) >= wn)[None, None], later, NEG)
    return jnp.stack([first, later])


def _ffn_kernel(x_ref, g_ref, wg_ref, wu_ref, wd_ref, o_ref, acc_ref):
    x = x_ref[...]
    ms = jnp.mean(x * x, axis=-1, keepdims=True)
    h = (x * lax.rsqrt(ms + RMS_EPS) * g_ref[...]).astype(BF16)
    acc_ref[...] = jnp.zeros_like(acc_ref)

    def body(c, carry):
        cols = pl.ds(pl.multiple_of(c * FFN_CHUNK, FFN_CHUNK), FFN_CHUNK)
        g = jnp.dot(h, wg_ref[:, cols], preferred_element_type=F32)
        u = jnp.dot(h, wu_ref[:, cols], preferred_element_type=F32)
        a = (g * jax.nn.sigmoid(g) * u).astype(BF16)
        acc_ref[...] += jnp.dot(a, wd_ref[cols, :], preferred_element_type=F32)
        return carry

    lax.fori_loop(0, wg_ref.shape[1] // FFN_CHUNK, body, 0)
    o_ref[...] = x + 0.5 * acc_ref[...]


def _ffn(x, gain, w_gate, w_up, w_down):
    n, d = x.shape
    f = w_gate.shape[1]
    tm = FFN_TILE
    return pl.pallas_call(
        _ffn_kernel,
        out_shape=jax.ShapeDtypeStruct((n, d), F32),
        grid=(n // tm,),
        in_specs=[pl.BlockSpec((tm, d), lambda i: (i, 0)),
                  _const_spec((1, d)), _const_spec((d, f)), _const_spec((d, f)), _const_spec((f, d))],
        out_specs=pl.BlockSpec((tm, d), lambda i: (i, 0)),
        scratch_shapes=[pltpu.VMEM((tm, d), F32)],
        compiler_params=_cparams(("parallel",)),
        name="ffn",
    )(x, gain.reshape(1, d).astype(F32), w_gate.astype(BF16), w_up.astype(BF16), w_down.astype(BF16))


S_AQ, S_AK, S_AV = 0, 768, 1536
S_BK, S_CK, S_GATE, S_END = 2304, 2816, 2944, 6016
T_BQ, T_BV, T_CQ, T_CV, T_IQ, T_IW, T_END = 0, 512, 1024, 1280, 1344, 1856, 1872


def _proj_kernel(x_ref, g_ref, ws_ref, wt_ref, bd_ref, gs_ref, gt_ref, *refs):
    ng = len(DIL_GROUPS)
    a_refs = refs[:3 * ng]
    bk_ref, ck_ref, gate_ref, bqt_ref, bvt_ref, cqt_ref, cvt_ref, iqt_ref, iwt_ref = refs[3 * ng:-1]
    shuffle_ref = refs[-1]
    tm = x_ref.shape[0]

    def store_by_residue(y, which):
        for g, (_, dil) in enumerate(DIL_GROUPS):
            out = a_refs[which * ng + g]
            part = y[:, g * A_OUT:(g + 1) * A_OUT]
            if dil == 1:
                out[0] = part.astype(BF16)
            else:
                for half in range(A_OUT // LANES):
                    shuffle_ref[half] = part[:, half * LANES:(half + 1) * LANES]
                for r in range(dil):
                    out[r] = jnp.concatenate(
                        [shuffle_ref[half, pl.ds(r, tm // dil, stride=dil), :] for half in range(A_OUT // LANES)],
                        axis=1).astype(BF16)

    x = x_ref[...]
    ms = jnp.mean(x * x, axis=-1, keepdims=True)
    h = (x * lax.rsqrt(ms + RMS_EPS) * g_ref[...]).astype(BF16)
    bd = bd_ref[...]

    def dot_s(c0, c1):
        return jnp.dot(h, ws_ref[:, c0:c1], preferred_element_type=F32)

    def head_inv_rms(y):
        outs = []
        for c in range(y.shape[1] // LANES):
            sq = y[:, c * LANES:(c + 1) * LANES]
            sq = sq * sq
            hi = sq.astype(BF16)
            lo = (sq - hi.astype(F32)).astype(BF16)
            msq = (jnp.dot(hi, bd, preferred_element_type=F32)
                   + jnp.dot(lo, bd, preferred_element_type=F32))
            outs.append(lax.rsqrt(msq + RMS_EPS))
        return outs[0] if len(outs) == 1 else jnp.concatenate(outs, axis=1)

    y = dot_s(S_AQ, S_AK)
    store_by_residue(y * head_inv_rms(y) * gs_ref[:, 0:768], 0)
    y = dot_s(S_AK, S_AV)
    store_by_residue(y * head_inv_rms(y) * gs_ref[:, 768:1536], 1)
    store_by_residue(dot_s(S_AV, S_BK), 2)
    y = dot_s(S_BK, S_CK)
    bk_ref[...] = (y * head_inv_rms(y) * gs_ref[:, 1536:2048]).astype(BF16)
    y = dot_s(S_CK, S_GATE)
    lane = lax.broadcasted_iota(jnp.int32, y.shape, 1)
    inv = jnp.where(lane < HEAD_DIM, head_inv_rms(y), 1.0)
    ck_ref[...] = (y * inv * gs_ref[:, 2048:2176]).astype(BF16)
    for c in range(3):
        y = dot_s(S_GATE + c * 1024, S_GATE + (c + 1) * 1024)
        gate_ref[:, c * 1024:(c + 1) * 1024] = jax.nn.sigmoid(y).astype(BF16)

    def dot_t(r0, r1):
        return lax.dot_general(wt_ref[r0:r1, :], h, (((1,), (1,)), ((), ())),
                               preferred_element_type=F32)

    def norm_t(y, gain):
        r = y.shape[0] // HEAD_DIM
        y3 = y.reshape(r, HEAD_DIM, tm)
        msq = jnp.mean(y3 * y3, axis=1, keepdims=True)
        return (y3 * lax.rsqrt(msq + RMS_EPS)).reshape(r * HEAD_DIM, tm) * gain

    bqt_ref[...] = norm_t(dot_t(T_BQ, T_BV), gt_ref[0:512, :]).astype(BF16)
    bvt_ref[...] = dot_t(T_BV, T_CQ).astype(BF16)
    cqt_ref[...] = norm_t(dot_t(T_CQ, T_CV), gt_ref[512:768, :]).astype(BF16)
    cvt_ref[...] = dot_t(T_CV, T_IQ).astype(BF16)
    iqt_ref[...] = dot_t(T_IQ, T_IW).astype(BF16)
    iwt_ref[...] = dot_t(T_IW, T_END) * (IDX_HEADS ** -0.5 * IDX_DIM ** -0.5)


def _proj_weights(w_in, qk_gain, tm):
    d = w_in.shape[0]
    o = 0
    a_qkv = w_in[:, o:o + 3 * A_HEADS * HEAD_DIM].reshape(d, 3, A_HEADS * HEAD_DIM)
    o += 3 * A_HEADS * HEAD_DIM
    b_qk = w_in[:, o:o + 4 * B_HEADS * HEAD_DIM].reshape(d, 4, B_HEADS, HEAD_DIM)
    o += 4 * B_HEADS * HEAD_DIM
    b_v = w_in[:, o:o + B_OUT]
    o += B_OUT
    c_q = w_in[:, o:o + C_OUT]
    c_k = w_in[:, o + C_OUT:o + C_OUT + HEAD_DIM]
    c_v = w_in[:, o + C_OUT + HEAD_DIM:o + C_OUT + 2 * HEAD_DIM]
    o += C_OUT + 2 * HEAD_DIM
    i_q = w_in[:, o:o + IDX_HEADS * IDX_DIM]
    i_k = w_in[:, o + IDX_HEADS * IDX_DIM:o + IDX_HEADS * IDX_DIM + IDX_DIM]
    i_w = w_in[:, o + IDX_HEADS * IDX_DIM + IDX_DIM:o + IDX_HEADS * IDX_DIM + IDX_DIM + IDX_HEADS]
    o += IDX_HEADS * IDX_DIM + IDX_DIM + IDX_HEADS
    gates = w_in[:, o:]
    b_k = jnp.stack([b_qk[:, 2], b_qk[:, 3]], axis=2).reshape(d, 2 * B_HEADS * HEAD_DIM)
    b_q = jnp.stack([b_qk[:, 0], b_qk[:, 1]], axis=2).reshape(d, 2 * B_HEADS * HEAD_DIM)
    w_s = jnp.concatenate([a_qkv[:, 0], a_qkv[:, 1], a_qkv[:, 2], b_k, c_k, i_k, gates], axis=1)
    w_t = jnp.concatenate([b_q, b_v, c_q, c_v, i_q, i_w, jnp.zeros((d, 8), w_in.dtype)], axis=1).T
    assert w_s.shape[1] == S_END and w_t.shape[0] == T_END
    scale = HEAD_DIM ** -0.5
    g = qk_gain.astype(F32)
    gs = jnp.concatenate([jnp.tile(g[0, 0] * scale, A_HEADS), jnp.tile(g[0, 1], A_HEADS),
                          jnp.tile(g[1, 1], 2 * B_HEADS), g[2, 1], jnp.ones((IDX_DIM,), F32)])[None]
    gt = jnp.concatenate([jnp.tile(g[1, 0] * (scale * LOG2E), 2 * B_HEADS),
                          jnp.tile(g[2, 0] * (scale * LOG2E), C_HEADS)])
    gt = jnp.broadcast_to(gt[:, None], (gt.shape[0], tm))
    return w_s.astype(BF16), w_t.astype(BF16), gs, gt


def _head_block_diag():
    r = np.arange(LANES)
    return jnp.asarray((r[:, None] // HEAD_DIM == r[None, :] // HEAD_DIM) / HEAD_DIM, BF16)


def _project(x, gain, w_in, qk_gain, batch, t):
    n, d = x.shape
    tm = TOKEN_TILE
    nt = n // tm
    per_batch = t // tm
    w_s, w_t, gs, gt = _proj_weights(w_in, qk_gain, tm)
    tok = lambda c: pl.BlockSpec((tm, c), lambda i: (i, 0))
    feat = lambda r: pl.BlockSpec((None, r, tm), lambda i: (i, 0, 0))
    a_shapes, a_specs = [], []
    for _ in range(3):
        for _, dil in DIL_GROUPS:
            a_shapes.append(jax.ShapeDtypeStruct((batch, dil, t // dil, A_OUT), BF16))
            a_specs.append(pl.BlockSpec((None, dil, tm // dil, A_OUT),
                                        lambda i: (i // per_batch, 0, i % per_batch, 0)))
    out_shape = a_shapes + [
        jax.ShapeDtypeStruct((n, 512), BF16), jax.ShapeDtypeStruct((n, 128), BF16),
        jax.ShapeDtypeStruct((n, 3072), BF16),
        jax.ShapeDtypeStruct((nt, 512, tm), BF16), jax.ShapeDtypeStruct((nt, 512, tm), BF16),
        jax.ShapeDtypeStruct((nt, 256, tm), BF16), jax.ShapeDtypeStruct((nt, 64, tm), BF16),
        jax.ShapeDtypeStruct((nt, 512, tm), BF16), jax.ShapeDtypeStruct((nt, 16, tm), F32)]
    out_specs = a_specs + [tok(512), tok(128), tok(3072),
                           feat(512), feat(512), feat(256), feat(64), feat(512), feat(16)]
    return pl.pallas_call(
        _proj_kernel,
        out_shape=out_shape,
        grid=(nt,),
        in_specs=[tok(d), _const_spec((1, d)), _const_spec(w_s.shape), _const_spec(w_t.shape),
                  _const_spec((LANES, LANES)), _const_spec(gs.shape), _const_spec(gt.shape)],
        out_specs=out_specs,
        scratch_shapes=[pltpu.VMEM((A_OUT // LANES, tm, LANES), F32)],
        compiler_params=_cparams(("parallel",)),
        name="proj",
    )(x, gain.reshape(1, d).astype(F32), w_s, w_t, _head_block_diag(), gs, gt)


def _dil_kernel(q_ref, kp_ref, kc_ref, vp_ref, vc_ref, bias_ref, o_ref, lse_ref):
    nq = q_ref.shape[0] // LANES
    qi = pl.program_id(2)
    lane = lax.broadcasted_iota(jnp.int32, (LANES, A_OUT), 1) // HEAD_DIM
    mine = [lane == h for h in range(A_GROUP_HEADS)]
    blocks = [slice(jb * LANES, (jb + 1) * LANES) for jb in range(nq)]

    def band(prev_ref, cur_ref, jb):
        if jb == 0:
            return jnp.concatenate([prev_ref[...], cur_ref[blocks[0], :]], axis=0)
        return cur_ref[(jb - 1) * LANES:(jb + 1) * LANES, :]

    logits = []
    for jb in range(nq):
        q = q_ref[blocks[jb], :]
        q4 = jnp.concatenate([jnp.where(mine[h], q, jnp.zeros_like(q)) for h in range(A_GROUP_HEADS)], axis=0)
        s = lax.dot_general(q4, band(kp_ref, kc_ref, jb), (((1,), (1,)), ((), ())), preferred_element_type=F32)
        bias = bias_ref[jnp.minimum(qi, 1) if jb == 0 else 1]
        logits.append(s + bias.reshape(A_GROUP_HEADS * LANES, 2 * LANES))
    probs, stats = [], []
    for s in logits:
        m = jnp.max(s, axis=1, keepdims=True)
        p = jnp.exp(s - m)
        ssum = jnp.sum(p, axis=1, keepdims=True)
        probs.append(p.astype(BF16))
        stats.append((1.0 / ssum, m + jnp.log(ssum)))
    for jb in range(nq):
        pv = jnp.dot(probs[jb], band(vp_ref, vc_ref, jb), preferred_element_type=F32)
        inv, lse4 = stats[jb]
        o = jnp.zeros((LANES, A_OUT), F32)
        lse = jnp.zeros((LANES, A_OUT), F32)
        for h in range(A_GROUP_HEADS):
            head = slice(h * LANES, (h + 1) * LANES)
            o = jnp.where(mine[h], pv[head] * inv[head], o)
            lse = jnp.where(mine[h], lse4[head], lse)
        o_ref[blocks[jb], :] = o
        lse_ref[blocks[jb], :] = lse


def _dilated_group(aq, ak, av, bias, dilation):
    batch, _, n, _ = aq.shape
    nblk = n // LANES
    nq = min(nblk, 4)
    qt = nq * LANES
    cur = pl.BlockSpec((None, None, qt, A_OUT), lambda b, r, i: (b, r, i, 0))
    prev = pl.BlockSpec((None, None, LANES, A_OUT), lambda b, r, i: (b, r, jnp.maximum(i * nq - 1, 0), 0))
    shp = jax.ShapeDtypeStruct((batch, dilation, n, A_OUT), F32)
    return pl.pallas_call(
        _dil_kernel,
        out_shape=[shp, shp],
        grid=(batch, dilation, nblk // nq),
        in_specs=[cur, prev, cur, prev, cur, _const_spec(bias.shape)],
        out_specs=[cur, cur],
        compiler_params=_cparams(("parallel", "parallel", "parallel")),
        name=f"dilated_d{dilation}",
    )(aq, ak, ak, av, av, bias)


def _fold_rows(x, op):
    r, c = x.shape
    return op(x.reshape(r // 64, 64, c), axis=0) if r > 64 else x


def _reduce_rows(x, op):
    x = _fold_rows(x, op)
    x = op(x.reshape(8, 8, x.shape[1]), axis=0)
    return op(x, axis=0, keepdims=True)


def _bias_tile(tab_ref, head, qblk, kblk):
    delta = qblk - kblk
    idx = jnp.where(delta < 0, MASKED, jnp.minimum(delta, FAR))
    if head is None:
        return tab_ref[idx]
    return tab_ref[head, idx]


def _diff_kernel(qt_ref, k_ref, vt_ref, tab_ref, lam_ref, gn_ref, o_ref):
    tq = qt_ref.shape[1]
    tk = TOKEN_TILE
    qi = pl.program_id(2)
    qt = qt_ref[...]
    row = lax.broadcasted_iota(jnp.int32, qt.shape, 0)
    q12 = jnp.concatenate([jnp.where(row < HEAD_DIM, qt, jnp.zeros_like(qt)),
                           jnp.where(row >= HEAD_DIM, qt, jnp.zeros_like(qt))], axis=1)
    nqb = tq // LANES
    nkb = tk // LANES

    def logits(c):
        kc = k_ref[pl.ds(pl.multiple_of(c * tk, tk), tk), :]
        bias = jnp.concatenate(
            [jnp.concatenate([_bias_tile(tab_ref, None, qi * nqb + iq, c * nkb + jk)
                              for iq in range(nqb)] * 2, axis=1) for jk in range(nkb)], axis=0)
        return jnp.dot(kc, q12, preferred_element_type=F32) + bias

    def exact_step(c, carry):
        m, l, acc = carry
        s = logits(c)
        m_new = jnp.maximum(m, _reduce_rows(s, jnp.max))
        alpha = jnp.exp2(m - m_new)
        p = jnp.exp2(s - m_new)
        l = alpha * l + _reduce_rows(p, jnp.sum)
        acc = alpha * acc + jnp.dot(vt_ref[c], p.astype(BF16), preferred_element_type=F32)
        return m_new, l, acc

    def lagged_update(c, s, m, l, acc, jump):
        p = jnp.exp2(s - m)
        top =_reduce_rows(s, jnp.max)
        l = l + _reduce_rows(p, jnp.sum)
        acc = acc + jnp.dot(vt_ref[c], p.astype(BF16), preferred_element_type=F32)
        m_new = jnp.maximum(m, top)
        alpha = jnp.exp2(m - m_new)
        return m_new, alpha * l, alpha * acc, jnp.maximum(jump, top - m)

    nch = ((qi + 1) * tq + tk - 1) // tk
    zero = jnp.zeros((1, 2 * tq), F32)
    acc0 = jnp.zeros((B_V_DIM, 2 * tq), F32)
    s0 = logits(0)
    m0 = jnp.max(s0[0:8], axis=0, keepdims=True)
    state = lagged_update(0, s0, m0, zero, acc0, zero)
    def lagged_pair(i, carry):
        c = 1 + 2 * i
        s_a, s_b = logits(c), logits(c + 1)
        return lagged_update(c + 1, s_b, *lagged_update(c, s_a, *carry))

    state = lax.fori_loop(0, (nch - 1) // 2, lagged_pair, state)
    state = lax.cond((nch - 1) % 2 == 1,
                     lambda st: lagged_update(nch - 1, logits(nch - 1), *st), lambda st: st, state)
    _, l, acc, jump = state
    l, acc = lax.cond(jnp.max(jump) > MAX_LAG,
                      lambda: lax.fori_loop(0, nch, exact_step, (jnp.full((1, 2 * tq), M_INIT, F32), zero, acc0))[1:],
                      lambda: (l, acc))
    a1, a2 = acc[:, :tq], acc[:, tq:]
    l1, l2 = l[:, :tq], l[:, tq:]

    lv = lam_ref[...]
    lam = (jnp.exp(jnp.sum(lv[0:1] * lv[1:2], axis=1, keepdims=True))
           - jnp.exp(jnp.sum(lv[2:3] * lv[3:4], axis=1, keepdims=True)) + lv[4:5, 0:1])
    o = a1 / l1 - lam * (a2 / l2)
    ms = jnp.mean(o * o, axis=0, keepdims=True)
    o = o * lax.rsqrt(ms + RMS_EPS) * gn_ref[...]
    o_ref[...] = o.T.astype(BF16)


def _diff_attention(bqt, bk, bvt, tab, lam_rows, gn, batch, t):
    tq = B_Q_TILE
    tk = TOKEN_TILE
    per = tk // tq
    nkt = t // tk
    bqt = bqt.reshape(batch, nkt, B_HEADS * LANES, tk)
    bvt = bvt.reshape(batch, nkt, B_OUT, tk)
    bk = bk.reshape(batch, t, B_HEADS * LANES)
    return pl.pallas_call(
        _diff_kernel,
        out_shape=jax.ShapeDtypeStruct((batch, t, B_OUT), BF16),
        grid=(batch, B_HEADS, t // tq),
        in_specs=[pl.BlockSpec((None, None, LANES, tq), lambda b, h, i: (b, i // per, h, i % per)),
                  pl.BlockSpec((None, t, LANES), lambda b, h, i: (b, 0, h)),
                  pl.BlockSpec((None, nkt, B_V_DIM, tk), lambda b, h, i: (b, 0, h, 0)),
                  pl.BlockSpec((None, FAR + 2, LANES, LANES), lambda b, h, i: (h, 0, 0, 0)),
                  _const_spec(lam_rows.shape), _const_spec(gn.shape)],
        out_specs=pl.BlockSpec((None, tq, B_V_DIM), lambda b, h, i: (b, i, h)),
        compiler_params=_cparams(("parallel", "parallel", "arbitrary")),
        name="diff_attention",
    )(bqt, bk, bvt, tab, lam_rows, gn).reshape(batch * t, B_OUT)


def _dsa_kernel(iqt_ref, iwt_ref, cqt_ref, k_ref, vt_ref, tab_ref, tri_ref, o_ref, s_ref, *, k_sel):
    tk = TOKEN_TILE
    nkb = tk // LANES
    qi = pl.program_id(1)
    nch = qi // nkb + 1
    qpos = qi * LANES + lax.broadcasted_iota(jnp.int32, (1, LANES), 1)
    zeros = jnp.zeros((HEAD_DIM, LANES), BF16)
    iq = iqt_ref[...]
    w = iwt_ref[...]
    iq_all = jnp.concatenate([jnp.concatenate([zeros, iq[h * IDX_DIM:(h + 1) * IDX_DIM]], axis=0)
                              for h in range(IDX_HEADS)], axis=1)
    cq = cqt_ref[...]
    cq_all = jnp.concatenate([jnp.concatenate([cq[h * HEAD_DIM:(h + 1) * HEAD_DIM], zeros], axis=0)
                              for h in range(C_HEADS)], axis=1)

    def chunk(c):
        return pl.ds(pl.multiple_of(c * tk, tk), tk)

    def raw_scores(c):
        return jnp.dot(k_ref[chunk(c), :], iq_all, preferred_element_type=F32)

    def score_chunk(c, raw, mn, mx, last):
        acc = w[0:1, :] * jnp.maximum(raw[:, 0:LANES], 0.0)
        for h in range(1, IDX_HEADS):
            acc = acc + w[h:h + 1, :] * jnp.maximum(raw[:, h * LANES:(h + 1) * LANES], 0.0)
        if last:
            kpos = c * tk + lax.broadcasted_iota(jnp.int32, (tk, LANES), 0)
            causal = kpos <= qpos
            s_ref[chunk(c), :] = jnp.where(causal, acc, NEG)
            mn = jnp.minimum(mn, _fold_rows(jnp.where(causal, acc, BIG), jnp.min))
            mx = jnp.maximum(mx, _fold_rows(jnp.where(causal, acc, NEG), jnp.max))
        else:
            s_ref[chunk(c), :] = acc
            mn = jnp.minimum(mn, _fold_rows(acc, jnp.min))
            mx = jnp.maximum(mx, _fold_rows(acc, jnp.max))
        return mn, mx

    def score_pair(c, carry, last):
        raw_a, raw_b = raw_scores(c), raw_scores(c + 1)
        return score_chunk(c + 1, raw_b, *score_chunk(c, raw_a, *carry, last=False), last=last)

    carry = lax.fori_loop(0, (nch - 1) // 2, lambda i, carry: score_pair(2 * i, carry, last=False),
                          (jnp.full((64, LANES), BIG, F32), jnp.full((64, LANES), NEG, F32)))
    mn, mx = lax.cond((nch - 1) % 2 == 1,
                      lambda st: score_pair(nch - 2, st, last=True),
                      lambda st: score_chunk(nch - 1, raw_scores(nch - 1), *st, last=True), carry)
    lo, hi = _reduce_rows(mn, jnp.min), _reduce_rows(mx, jnp.max)

    def count(thr):
        def body(c, carry):
            gt, ge = carry
            s = s_ref[chunk(c), :]
            gt = gt + _fold_rows(jnp.where(s > thr, 1.0, 0.0), jnp.sum)
            ge = ge + _fold_rows(jnp.where(s >= thr, 1.0, 0.0), jnp.sum)
            return gt, ge
        z = jnp.zeros((64, LANES), F32)
        gt, ge = lax.fori_loop(0, nch, body, (z, z))
        return _reduce_rows(gt, jnp.sum), _reduce_rows(ge, jnp.sum)

    def count_gt(thr):
        def body(c, gt):
            return gt + _fold_rows(jnp.where(s_ref[chunk(c), :] > thr, 1.0, 0.0), jnp.sum)
        return _reduce_rows(lax.fori_loop(0, nch, body, jnp.zeros((64, LANES), F32)), jnp.sum)

    def max_below(bound, strict):
        def body(c, mx):
            s = s_ref[chunk(c), :]
            keep = (s < bound) if strict else (s <= bound)
            return jnp.maximum(mx, _fold_rows(jnp.where(keep, s, NEG), jnp.max))
        return _reduce_rows(lax.fori_loop(0, nch, body, jnp.full((64, LANES), NEG, F32)), jnp.max)

    kf = float(k_sel)
    need = qpos >= k_sel

    def bisect(_, carry):
        lo, hi = carry
        mid = lo + (hi - lo) * 0.5
        below = count_gt(mid) < kf
        return jnp.where(below, lo, mid), jnp.where(below, mid, hi)

    lo, hi = lax.fori_loop(0, N_BISECT, bisect, (lo, hi))

    cand = max_below(hi, strict=False)
    gt, ge = count(cand)

    def walk_cond(carry):
        _, _, ge = carry
        return jnp.max(jnp.where(need & (ge < kf), 1.0, 0.0)) > 0.0

    def walk_body(carry):
        cand, gt, ge = carry
        nxt = max_below(cand, strict=True)
        cand = jnp.where(ge < kf, nxt, cand)
        gt, ge = count(cand)
        return cand, gt, ge

    cand, gt, ge = lax.while_loop(walk_cond, walk_body, (cand, gt, ge))
    thr = jnp.where(need, cand, THR_ALL)
    want_eq = jnp.where(need, kf - gt, 0.0)

    any_tie = jnp.max(jnp.where(need & (ge > kf), 1.0, 0.0)) > 0.0

    def mark_with_ties(c, eq_seen):
        s = s_ref[chunk(c), :]
        eq = jnp.where(s == thr, 1.0, 0.0)
        rank = eq_seen + jnp.dot(tri_ref[...], eq.astype(BF16), preferred_element_type=F32)
        keep = jnp.where(s > thr, 1.0, jnp.where(rank <= want_eq, eq, 0.0))
        s_ref[chunk(c), :] = jnp.where(keep > 0.5, 0.0, NEG)
        return eq_seen + _reduce_rows(eq, jnp.sum)

    def mark_no_ties(c, carry):
        s_ref[chunk(c), :] = jnp.where(s_ref[chunk(c), :] >= thr, 0.0, NEG)
        return carry

    @pl.when(any_tie)
    def _():
        lax.fori_loop(0, nch, mark_with_ties, jnp.zeros((1, LANES), F32))

    @pl.when(jnp.logical_not(any_tie))
    def _():
        lax.fori_loop(0, nch, mark_no_ties, 0)

    def masked_logits(c):
        sel = s_ref[chunk(c), :]
        bias = jnp.concatenate(
            [jnp.concatenate([_bias_tile(tab_ref, h, qi, c * nkb + jk) for jk in range(nkb)], axis=0) + sel
             for h in range(C_HEADS)], axis=1)
        return jnp.dot(k_ref[chunk(c), :], cq_all, preferred_element_type=F32) + bias

    def exact_step(c, carry):
        m, l, acc = carry
        lg = masked_logits(c)
        m_new = jnp.maximum(m, _reduce_rows(lg, jnp.max))
        alpha = jnp.exp2(m - m_new)
        p = jnp.exp2(lg - m_new)
        l = alpha * l + _reduce_rows(p, jnp.sum)
        acc = alpha * acc + jnp.dot(vt_ref[c], p.astype(BF16), preferred_element_type=F32)
        return m_new, l, acc

    def lagged_update(c, lg, m, l, acc, jump):
        p = jnp.exp2(lg - m)
        top = _reduce_rows(lg, jnp.max)
        l = l + _reduce_rows(p, jnp.sum)
        acc = acc + jnp.dot(vt_ref[c], p.astype(BF16), preferred_element_type=F32)
        m_new = jnp.maximum(m, top)
        alpha = jnp.exp2(m - m_new)
        return m_new, alpha * l, alpha * acc, jnp.maximum(jump, top - m)

    wide = C_HEADS * LANES
    init = (jnp.full((1, wide), M_INIT, F32), jnp.zeros((1, wide), F32), jnp.zeros((HEAD_DIM, wide), F32))
    first = exact_step(0, init)

    def lagged_pair(i, carry):
        c = 1 + 2 * i
        lg_a, lg_b = masked_logits(c), masked_logits(c + 1)
        return lagged_update(c + 1, lg_b, *lagged_update(c, lg_a, *carry))

    state = lax.fori_loop(0, (nch - 1) // 2, lagged_pair, first + (jnp.zeros((1, wide), F32),))
    state = lax.cond((nch - 1) % 2 == 1,
                     lambda st: lagged_update(nch - 1, masked_logits(nch - 1), *st), lambda st: st, state)
    _, l, acc, jump = state
    l, acc = lax.cond(jnp.max(jump) > MAX_LAG,
                      lambda: lax.fori_loop(1, nch, exact_step, first)[1:],
                      lambda: (l, acc))
    o = acc / l
    ot = jnp.concatenate([o[:, h * LANES:(h + 1) * LANES] for h in range(C_HEADS)], axis=0)
    o_ref[...] = ot.T.astype(BF16)


def _dsa_attention(iqt, iwt, cqt, ck, cvt, tab, batch, t):
    tk = TOKEN_TILE
    per = tk // LANES
    nkt = t // tk
    k_sel = min(TOPK_MAX, t // 4)
    iqt = iqt.reshape(batch, nkt, IDX_HEADS * IDX_DIM, tk)
    iwt = iwt.reshape(batch, nkt, 16, tk)
    cqt = cqt.reshape(batch, nkt, C_OUT, tk)
    cvt = cvt.reshape(batch, nkt, HEAD_DIM, tk)
    ck = ck.reshape(batch, t, LANES)
    r = np.arange(tk)
    tri = jnp.asarray(r[:, None] >= r[None, :], BF16)
    qblock = lambda rows: pl.BlockSpec((None, None, rows, LANES), lambda b, i: (b, i // per, 0, i % per))
    return pl.pallas_call(
        functools.partial(_dsa_kernel, k_sel=k_sel),
        out_shape=jax.ShapeDtypeStruct((batch, t, C_OUT), BF16),
        grid=(batch, t // LANES),
        in_specs=[qblock(IDX_HEADS * IDX_DIM), qblock(16), qblock(C_OUT),
                  pl.BlockSpec((None, t, LANES), lambda b, i: (b, 0, 0)),
                  pl.BlockSpec((None, nkt, HEAD_DIM, tk), lambda b, i: (b, 0, 0, 0)),
                  _const_spec(tab.shape), _const_spec(tri.shape)],
        out_specs=pl.BlockSpec((None, LANES, C_OUT), lambda b, i: (b, i, 0)),
        scratch_shapes=[pltpu.VMEM((t, LANES), F32)],
        compiler_params=_cparams(("parallel", "arbitrary")),
        name="dsa_attention",
    )(iqt, iwt, cqt, ck, cvt, tab, tri).reshape(batch * t, C_OUT)


def _merge_kernel(x_ref, *refs):
    ng = len(DIL_GROUPS)
    a_refs = refs[:2 * ng]
    ob_ref, oc_ref, gate_ref, wa_ref, wb_ref, wc_ref, wo_ref, out_ref = refs[2 * ng:-1]
    shuffle_ref = refs[-1]
    tm, d = x_ref.shape

    def token_order(ref, slot):
        dil = ref.shape[0]
        if dil == 1:
            return ref[0]
        halves = range(A_OUT // LANES)
        for r in range(dil):
            for half in halves:
                shuffle_ref[slot, half, pl.ds(r, tm // dil, stride=dil), :] = ref[r, :, half * LANES:(half + 1) * LANES]
        return jnp.concatenate([shuffle_ref[slot, half] for half in halves], axis=1)

    outs = [token_order(a_refs[2 * g], 2 * g) for g in range(ng)]
    lses = [token_order(a_refs[2 * g + 1], 2 * g + 1) for g in range(ng)]
    top = functools.reduce(jnp.maximum, lses)
    es = [jnp.exp(lse - top) for lse in lses]
    num = sum(e * o for e, o in zip(es, outs))
    oa = (num / sum(es)).astype(BF16)
    y = gate_ref[:, 0:d].astype(F32) * jnp.dot(oa, wa_ref[...], preferred_element_type=F32)
    y = y + gate_ref[:, d:2 * d].astype(F32) * jnp.dot(ob_ref[...], wb_ref[...], preferred_element_type=F32)
    y = y + gate_ref[:, 2 * d:3 * d].astype(F32) * jnp.dot(oc_ref[...], wc_ref[...], preferred_element_type=F32)
    out_ref[...] = x_ref[...] + jnp.dot(y.astype(BF16), wo_ref[...], preferred_element_type=F32)


def _merge(x, a_parts, ob, oc, gates, wa, wb, wc, wo, t):
    n, d = x.shape
    tm = TOKEN_TILE
    per_batch = t // tm
    tok = lambda c: pl.BlockSpec((tm, c), lambda i: (i, 0))
    by_residue = lambda dil: pl.BlockSpec((None, dil, tm // dil, A_OUT),
                                          lambda i: (i // per_batch, 0, i % per_batch, 0))
    ws = [w.astype(BF16) for w in (wa, wb, wc, wo)]
    return pl.pallas_call(
        _merge_kernel,
        out_shape=jax.ShapeDtypeStruct((n, d), F32),
        grid=(n // tm,),
        in_specs=[tok(d)] + [by_residue(z.shape[1]) for z in a_parts] + [tok(B_OUT), tok(C_OUT), tok(3 * d)]
                 + [_const_spec(w.shape) for w in ws],
        out_specs=tok(d),
        scratch_shapes=[pltpu.VMEM((len(a_parts), A_OUT // LANES, tm, LANES), F32)],
        compiler_params=_cparams(("parallel",)),
        name="merge",
    )(x, *a_parts, ob, oc, gates, *ws)


def _token_mixer(x, batch, t, layer, mix_norm, w_in, qk_gain, diff_lambda, diff_out_norm,
                 w_branch_a, w_branch_b, w_branch_c, w_out, band_tabs, tab_b, tab_c):
    ng = len(DIL_GROUPS)
    outs = _project(x, mix_norm, w_in, qk_gain, batch, t)
    a_in, (bk, ck, gates, bqt, bvt, cqt, cvt, iqt, iwt) = outs[:3 * ng], outs[3 * ng:]
    a_parts = []
    for g, (_, dilation) in enumerate(DIL_GROUPS):
        a_parts += _dilated_group(a_in[g], a_in[ng + g], a_in[2 * ng + g], band_tabs[g], dilation)
    lam_init = 0.8 - 0.6 * np.exp(-0.3 * layer)
    lam_rows = jnp.concatenate([diff_lambda.astype(F32), jnp.full((4, HEAD_DIM), lam_init, F32)], axis=0)
    gn = jnp.broadcast_to((diff_out_norm.astype(F32) * (1.0 - lam_init))[:, None], (B_V_DIM, B_Q_TILE))
    ob = _diff_attention(bqt, bk, bvt, tab_b, lam_rows, gn, batch, t)
    oc = _dsa_attention(iqt, iwt, cqt, ck, cvt, tab_c, batch, t)
    return _merge(x, a_parts, ob, oc, gates, w_branch_a, w_branch_b, w_branch_c, w_out, t)


def kernel(x, rel_bias, ffn1_norm, ffn1_w_gate, ffn1_w_up, ffn1_w_down, mix_norm, w_in, qk_gain,
           diff_lambda, diff_out_norm, w_branch_a, w_branch_b, w_branch_c, w_out,
           ffn2_norm, ffn2_w_gate, ffn2_w_up, ffn2_w_down):
    batch, t, d = x.shape
    depth = w_in.shape[0]
    assert t % (DIL_GROUPS[-1][1] * LANES) == 0 and t % TOKEN_TILE == 0
    band_tabs = [_band_tables(rel_bias[:, g * A_GROUP_HEADS:(g + 1) * A_GROUP_HEADS], dil)
                 for g, (_, dil) in enumerate(DIL_GROUPS)]
    tab_b = _toeplitz_tables(rel_bias[:, A_HEADS:A_HEADS + B_HEADS] * LOG2E)
    tab_c = _toeplitz_tables(rel_bias[:, A_HEADS + B_HEADS:] * LOG2E)
    h = x.reshape(batch * t, d).astype(F32)
    for i in range(depth):
        h = _ffn(h, ffn1_norm[i], ffn1_w_gate[i], ffn1_w_up[i], ffn1_w_down[i])
        h = _token_mixer(h, batch, t, i, mix_norm[i], w_in[i], qk_gain[i], diff_lambda[i], diff_out_norm[i],
                         w_branch_a[i], w_branch_b[i], w_branch_c[i], w_out[i], band_tabs, tab_b, tab_c)
        h = _ffn(h, ffn2_norm[i], ffn2_w_gate[i], ffn2_w_up[i], ffn2_w_down[i])
    return h.reshape(batch, t, d).astype(x.dtype)
```

```python
import functools

import numpy as np
import jax
import jax.numpy as jnp
from jax import lax
from jax.experimental import pallas as pl
from jax.experimental.pallas import tpu as pltpu

F32 = jnp.float32
BF16 = jnp.bfloat16

HEAD_DIM = 64
DIL_GROUPS = ((128, 1), (512, 4), (2048, 16))
A_GROUP_HEADS = 4
A_HEADS = A_GROUP_HEADS * len(DIL_GROUPS)
A_OUT = A_GROUP_HEADS * HEAD_DIM
B_HEADS = 4
B_V_DIM = 2 * HEAD_DIM
B_OUT = B_HEADS * B_V_DIM
C_HEADS = 4
C_OUT = C_HEADS * HEAD_DIM
IDX_HEADS = 8
IDX_DIM = 64
TOPK_MAX = 256
NUM_BUCKETS = 32
MAX_DISTANCE = 2048
RMS_EPS = 1e-6
LOG2E = 1.4426950408889634

LANES = 128
TOKEN_TILE = 512
FFN_TILE = 1024
B_Q_TILE = 512
MAX_LAG = 60.0
FFN_CHUNK = 256
VMEM_LIMIT = 58 * 1024 * 1024

NEG = -1e30
M_INIT = -1e29
BIG = 1e30
THR_ALL = -1e29
N_BISECT = 14


def _cparams(sem):
    return pltpu.CompilerParams(dimension_semantics=sem, vmem_limit_bytes=VMEM_LIMIT)


def _const_spec(shape):
    nd = len(shape)
    return pl.BlockSpec(shape, lambda *_: (0,) * nd, pipeline_mode=pl.Buffered(1))


def _rel_bucket_np(dist):
    n = np.maximum(dist, 0)
    max_exact = NUM_BUCKETS // 2
    nf = np.maximum(n, 1).astype(np.float64)
    large = max_exact + (np.log(nf / max_exact) / np.log(MAX_DISTANCE / max_exact)
                         * (NUM_BUCKETS - max_exact)).astype(np.int64)
    large = np.minimum(large, NUM_BUCKETS - 1)
    return np.where(n < max_exact, n, large)


def _far_delta():
    d = 1
    while not np.all(_rel_bucket_np(np.arange(d * LANES - LANES + 1, d * LANES + LANES)) == NUM_BUCKETS - 1):
        d += 1
    return d


FAR = _far_delta()
MASKED = FAR + 1


def _toeplitz(w, n_rows, n_cols):
    period = n_rows + n_cols
    w = jnp.pad(w, ((0, 0), (0, period - w.shape[1])))
    m = jnp.tile(w, (1, n_rows))[:, :n_rows * (period - 1)].reshape(-1, n_rows, period - 1)
    return m[:, :, n_rows - 1:n_rows - 1 + n_cols]


def _bias_by_distance(bias_heads, dist, valid):
    vals = jnp.take(bias_heads.astype(F32), jnp.asarray(_rel_bucket_np(dist), jnp.int32), axis=0).T
    return jnp.where(jnp.asarray(valid)[None], vals, NEG)


def _toeplitz_tables(bias_heads):
    n_cols = (FAR + 1) * LANES
    dist = np.arange(LANES - 1 + n_cols) - (LANES - 1)
    tiles = _toeplitz(_bias_by_distance(bias_heads, dist, dist >= 0), LANES, n_cols)
    tiles = tiles.reshape(-1, LANES, FAR + 1, LANES).transpose(0, 2, 1, 3)
    masked = jnp.full((tiles.shape[0], 1, LANES, LANES), NEG, F32)
    return jnp.concatenate([tiles, masked], axis=1)


def _band_tables(bias_heads, dilation):
    wn = LANES
    sub = np.arange(3 * wn - 1) - (wn - 1)
    w = _bias_by_distance(bias_heads, sub * dilation, (sub >= 0) & (sub <= wn))
    later = jnp.flip(_toeplitz(w, wn, 2 * wn), axis=(1, 2))
    first = jnp.where(jnp.asarray(np.arange(2 * wn) >= wn)[None, None], later, NEG)
    return jnp.stack([first, later])


def _ffn_kernel(x_ref, g_ref, wg_ref, wu_ref, wd_ref, o_ref, acc_ref):
    x = x_ref[...]
    ms = jnp.mean(x * x, axis=-1, keepdims=True)
    h = (x * lax.rsqrt(ms + RMS_EPS) * g_ref[...]).astype(BF16)
    acc_ref[...] = jnp.zeros_like(acc_ref)

    def body(c, carry):
        cols = pl.ds(pl.multiple_of(c * FFN_CHUNK, FFN_CHUNK), FFN_CHUNK)
        g = jnp.dot(h, wg_ref[:, cols], preferred_element_type=F32)
        u = jnp.dot(h, wu_ref[:, cols], preferred_element_type=F32)
        a = (g * jax.nn.sigmoid(g) * u).astype(BF16)
        acc_ref[...] += jnp.dot(a, wd_ref[cols, :], preferred_element_type=F32)
        return carry

    lax.fori_loop(0, wg_ref.shape[1] // FFN_CHUNK, body, 0)
    o_ref[...] = x + 0.5 * acc_ref[...]


def _ffn(x, gain, w_gate, w_up, w_down):
    n, d = x.shape
    f = w_gate.shape[1]
    tm = FFN_TILE
    return pl.pallas_call(
        _ffn_kernel,
        out_shape=jax.ShapeDtypeStruct((n, d), F32),
        grid=(n // tm,),
        in_specs=[pl.BlockSpec((tm, d), lambda i: (i, 0)),
                  _const_spec((1, d)), _const_spec((d, f)), _const_spec((d, f)), _const_spec((f, d))],
        out_specs=pl.BlockSpec((tm, d), lambda i: (i, 0)),
        scratch_shapes=[pltpu.VMEM((tm, d), F32)],
        compiler_params=_cparams(("parallel",)),
        name="ffn",
    )(x, gain.reshape(1, d).astype(F32), w_gate.astype(BF16), w_up.astype(BF16), w_down.astype(BF16))


S_AQ, S_AK, S_AV = 0, 768, 1536
S_BK, S_CK, S_GATE, S_END = 2304, 2816, 2944, 6016
T_BQ, T_BV, T_CQ, T_CV, T_IQ, T_IW, T_END = 0, 512, 1024, 1280, 1344, 1856, 1872


def _proj_kernel(x_ref, g_ref, ws_ref, wt_ref, bd_ref, gs_ref, gt_ref, *refs):
    ng = len(DIL_GROUPS)
    a_refs = refs[:3 * ng]
    bk_ref, ck_ref, gate_ref, bqt_ref, bvt_ref, cqt_ref, cvt_ref, iqt_ref, iwt_ref = refs[3 * ng:-1]
    shuffle_ref = refs[-1]
    tm = x_ref.shape[0]

    def store_by_residue(y, which):
        for g, (_, dil) in enumerate(DIL_GROUPS):
            out = a_refs[which * ng + g]
            part = y[:, g * A_OUT:(g + 1) * A_OUT]
            if dil == 1:
                out[0] = part.astype(BF16)
            else:
                for half in range(A_OUT // LANES):
                    shuffle_ref[half] = part[:, half * LANES:(half + 1) * LANES]
                for r in range(dil):
                    out[r] = jnp.concatenate(
                        [shuffle_ref[half, pl.ds(r, tm // dil, stride=dil), :] for half in range(A_OUT // LANES)],
                        axis=1).astype(BF16)

    x = x_ref[...]
    ms = jnp.mean(x * x, axis=-1, keepdims=True)
    h = (x * lax.rsqrt(ms + RMS_EPS) * g_ref[...]).astype(BF16)
    bd = bd_ref[...]

    def dot_s(c0, c1):
        return jnp.dot(h, ws_ref[:, c0:c1], preferred_element_type=F32)

    def head_inv_rms(y):
        outs = []
        for c in range(y.shape[1] // LANES):
            sq = y[:, c * LANES:(c + 1) * LANES]
            sq = sq * sq
            hi = sq.astype(BF16)
            lo = (sq - hi.astype(F32)).astype(BF16)
            msq = (jnp.dot(hi, bd, preferred_element_type=F32)
                   + jnp.dot(lo, bd, preferred_element_type=F32))
            outs.append(lax.rsqrt(msq + RMS_EPS))
        return outs[0] if len(outs) == 1 else jnp.concatenate(outs, axis=1)

    y = dot_s(S_AQ, S_AK)
    store_by_residue(y * head_inv_rms(y) * gs_ref[:, 0:768], 0)
    y = dot_s(S_AK, S_AV)
    store_by_residue(y * head_inv_rms(y) * gs_ref[:, 768:1536], 1)
    store_by_residue(dot_s(S_AV, S_BK), 2)
    y = dot_s(S_BK, S_CK)
    bk_ref[...] = (y * head_inv_rms(y) * gs_ref[:, 1536:2048]).astype(BF16)
    y = dot_s(S_CK, S_GATE)
    lane = lax.broadcasted_iota(jnp.int32, y.shape, 1)
    inv = jnp.where(lane < HEAD_DIM, head_inv_rms(y), 1.0)
    ck_ref[...] = (y * inv * gs_ref[:, 2048:2176]).astype(BF16)
    for c in range(3):
        y = dot_s(S_GATE + c * 1024, S_GATE + (c + 1) * 1024)
        gate_ref[:, c * 1024:(c + 1) * 1024] = jax.nn.sigmoid(y).astype(BF16)

    def dot_t(r0, r1):
        return lax.dot_general(wt_ref[r0:r1, :], h, (((1,), (1,)), ((), ())),
                               preferred_element_type=F32)

    def norm_t(y, gain):
        r = y.shape[0] // HEAD_DIM
        y3 = y.reshape(r, HEAD_DIM, tm)
        msq = jnp.mean(y3 * y3, axis=1, keepdims=True)
        return (y3 * lax.rsqrt(msq + RMS_EPS)).reshape(r * HEAD_DIM, tm) * gain

    bqt_ref[...] = norm_t(dot_t(T_BQ, T_BV), gt_ref[0:512, :]).astype(BF16)
    bvt_ref[...] = dot_t(T_BV, T_CQ).astype(BF16)
    cqt_ref[...] = norm_t(dot_t(T_CQ, T_CV), gt_ref[512:768, :]).astype(BF16)
    cvt_ref[...] = dot_t(T_CV, T_IQ).astype(BF16)
    iqt_ref[...] = dot_t(T_IQ, T_IW).astype(BF16)
    iwt_ref[...] = dot_t(T_IW, T_END) * (IDX_HEADS ** -0.5 * IDX_DIM ** -0.5)


def _proj_weights(w_in, qk_gain, tm):
    d = w_in.shape[0]
    o = 0
    a_qkv = w_in[:, o:o + 3 * A_HEADS * HEAD_DIM].reshape(d, 3, A_HEADS * HEAD_DIM)
    o += 3 * A_HEADS * HEAD_DIM
    b_qk = w_in[:, o:o + 4 * B_HEADS * HEAD_DIM].reshape(d, 4, B_HEADS, HEAD_DIM)
    o += 4 * B_HEADS * HEAD_DIM
    b_v = w_in[:, o:o + B_OUT]
    o += B_OUT
    c_q = w_in[:, o:o + C_OUT]
    c_k = w_in[:, o + C_OUT:o + C_OUT + HEAD_DIM]
    c_v = w_in[:, o + C_OUT + HEAD_DIM:o + C_OUT + 2 * HEAD_DIM]
    o += C_OUT + 2 * HEAD_DIM
    i_q = w_in[:, o:o + IDX_HEADS * IDX_DIM]
    i_k = w_in[:, o + IDX_HEADS * IDX_DIM:o + IDX_HEADS * IDX_DIM + IDX_DIM]
    i_w = w_in[:, o + IDX_HEADS * IDX_DIM + IDX_DIM:o + IDX_HEADS * IDX_DIM + IDX_DIM + IDX_HEADS]
    o += IDX_HEADS * IDX_DIM + IDX_DIM + IDX_HEADS
    gates = w_in[:, o:]
    b_k = jnp.stack([b_qk[:, 2], b_qk[:, 3]], axis=2).reshape(d, 2 * B_HEADS * HEAD_DIM)
    b_q = jnp.stack([b_qk[:, 0], b_qk[:, 1]], axis=2).reshape(d, 2 * B_HEADS * HEAD_DIM)
    w_s = jnp.concatenate([a_qkv[:, 0], a_qkv[:, 1], a_qkv[:, 2], b_k, c_k, i_k, gates], axis=1)
    w_t = jnp.concatenate([b_q, b_v, c_q, c_v, i_q, i_w, jnp.zeros((d, 8), w_in.dtype)], axis=1).T
    assert w_s.shape[1] == S_END and w_t.shape[0] == T_END
    scale = HEAD_DIM ** -0.5
    g = qk_gain.astype(F32)
    gs = jnp.concatenate([jnp.tile(g[0, 0] * scale, A_HEADS), jnp.tile(g[0, 1], A_HEADS),
                          jnp.tile(g[1, 1], 2 * B_HEADS), g[2, 1], jnp.ones((IDX_DIM,), F32)])[None]
    gt = jnp.concatenate([jnp.tile(g[1, 0] * (scale * LOG2E), 2 * B_HEADS),
                          jnp.tile(g[2, 0] * (scale * LOG2E), C_HEADS)])
    gt = jnp.broadcast_to(gt[:, None], (gt.shape[0], tm))
    return w_s.astype(BF16), w_t.astype(BF16), gs, gt


def _head_block_diag():
    r = np.arange(LANES)
    return jnp.asarray((r[:, None] // HEAD_DIM == r[None, :] // HEAD_DIM) / HEAD_DIM, BF16)


def _project(x, gain, w_in, qk_gain, batch, t):
    n, d = x.shape
    tm = TOKEN_TILE
    nt = n // tm
    per_batch = t // tm
    w_s, w_t, gs, gt = _proj_weights(w_in, qk_gain, tm)
    tok = lambda c: pl.BlockSpec((tm, c), lambda i: (i, 0))
    feat = lambda r: pl.BlockSpec((None, r, tm), lambda i: (i, 0, 0))
    a_shapes, a_specs = [], []
    for _ in range(3):
        for _, dil in DIL_GROUPS:
            a_shapes.append(jax.ShapeDtypeStruct((batch, dil, t // dil, A_OUT), BF16))
            a_specs.append(pl.BlockSpec((None, dil, tm // dil, A_OUT),
                                        lambda i: (i // per_batch, 0, i % per_batch, 0)))
    out_shape = a_shapes + [
        jax.ShapeDtypeStruct((n, 512), BF16), jax.ShapeDtypeStruct((n, 128), BF16),
        jax.ShapeDtypeStruct((n, 3072), BF16),
        jax.ShapeDtypeStruct((nt, 512, tm), BF16), jax.ShapeDtypeStruct((nt, 512, tm), BF16),
        jax.ShapeDtypeStruct((nt, 256, tm), BF16), jax.ShapeDtypeStruct((nt, 64, tm), BF16),
        jax.ShapeDtypeStruct((nt, 512, tm), BF16), jax.ShapeDtypeStruct((nt, 16, tm), F32)]
    out_specs = a_specs + [tok(512), tok(128), tok(3072),
                           feat(512), feat(512), feat(256), feat(64), feat(512), feat(16)]
    return pl.pallas_call(
        _proj_kernel,
        out_shape=out_shape,
        grid=(nt,),
        in_specs=[tok(d), _const_spec((1, d)), _const_spec(w_s.shape), _const_spec(w_t.shape),
                  _const_spec((LANES, LANES)), _const_spec(gs.shape), _const_spec(gt.shape)],
        out_specs=out_specs,
        scratch_shapes=[pltpu.VMEM((A_OUT // LANES, tm, LANES), F32)],
        compiler_params=_cparams(("parallel",)),
        name="proj",
    )(x, gain.reshape(1, d).astype(F32), w_s, w_t, _head_block_diag(), gs, gt)


def _dil_kernel(q_ref, kp_ref, kc_ref, vp_ref, vc_ref, bias_ref, o_ref, lse_ref):
    nq = q_ref.shape[0] // LANES
    qi = pl.program_id(2)
    lane = lax.broadcasted_iota(jnp.int32, (LANES, A_OUT), 1) // HEAD_DIM
    mine = [lane == h for h in range(A_GROUP_HEADS)]
    blocks = [slice(jb * LANES, (jb + 1) * LANES) for jb in range(nq)]

    def band(prev_ref, cur_ref, jb):
        if jb == 0:
            return jnp.concatenate([prev_ref[...], cur_ref[blocks[0], :]], axis=0)
        return cur_ref[(jb - 1) * LANES:(jb + 1) * LANES, :]

    logits = []
    for jb in range(nq):
        q = q_ref[blocks[jb], :]
        q4 = jnp.concatenate([jnp.where(mine[h], q, jnp.zeros_like(q)) for h in range(A_GROUP_HEADS)], axis=0)
        s = lax.dot_general(q4, band(kp_ref, kc_ref, jb), (((1,), (1,)), ((), ())), preferred_element_type=F32)
        bias = bias_ref[jnp.minimum(qi, 1) if jb == 0 else 1]
        logits.append(s + bias.reshape(A_GROUP_HEADS * LANES, 2 * LANES))
    probs, stats = [], []
    for s in logits:
        m = jnp.max(s, axis=1, keepdims=True)
        p = jnp.exp(s - m)
        ssum = jnp.sum(p, axis=1, keepdims=True)
        probs.append(p.astype(BF16))
        stats.append((1.0 / ssum, m + jnp.log(ssum)))
    for jb in range(nq):
        pv = jnp.dot(probs[jb], band(vp_ref, vc_ref, jb), preferred_element_type=F32)
        inv, lse4 = stats[jb]
        o = jnp.zeros((LANES, A_OUT), F32)
        lse = jnp.zeros((LANES, A_OUT), F32)
        for h in range(A_GROUP_HEADS):
            head = slice(h * LANES, (h + 1) * LANES)
            o = jnp.where(mine[h], pv[head] * inv[head], o)
            lse = jnp.where(mine[h], lse4[head], lse)
        o_ref[blocks[jb], :] = o
        lse_ref[blocks[jb], :] = lse


def _dilated_group(aq, ak, av, bias, dilation):
    batch, _, n, _ = aq.shape
    nblk = n // LANES
    nq = min(nblk, 4)
    qt = nq * LANES
    cur = pl.BlockSpec((None, None, qt, A_OUT), lambda b, r, i: (b, r, i, 0))
    prev = pl.BlockSpec((None, None, LANES, A_OUT), lambda b, r, i: (b, r, jnp.maximum(i * nq - 1, 0), 0))
    shp = jax.ShapeDtypeStruct((batch, dilation, n, A_OUT), F32)
    return pl.pallas_call(
        _dil_kernel,
        out_shape=[shp, shp],
        grid=(batch, dilation, nblk // nq),
        in_specs=[cur, prev, cur, prev, cur, _const_spec(bias.shape)],
        out_specs=[cur, cur],
        compiler_params=_cparams(("parallel", "parallel", "parallel")),
        name=f"dilated_d{dilation}",
    )(aq, ak, ak, av, av, bias)


def _fold_rows(x, op):
    r, c = x.shape
    return op(x.reshape(r // 64, 64, c), axis=0) if r > 64 else x


def _reduce_rows(x, op):
    x = _fold_rows(x, op)
    x = op(x.reshape(8, 8, x.shape[1]), axis=0)
    return op(x, axis=0, keepdims=True)


def _bias_tile(tab_ref, head, qblk, kblk):
    delta = qblk - kblk
    idx = jnp.where(delta < 0, MASKED, jnp.minimum(delta, FAR))
    if head is None:
        return tab_ref[idx]
    return tab_ref[head, idx]


def _diff_kernel(qt_ref, k_ref, vt_ref, tab_ref, lam_ref, gn_ref, o_ref):
    tq = qt_ref.shape[1]
    tk = TOKEN_TILE
    qi = pl.program_id(2)
    qt = qt_ref[...]
    row = lax.broadcasted_iota(jnp.int32, qt.shape, 0)
    q12 = jnp.concatenate([jnp.where(row < HEAD_DIM, qt, jnp.zeros_like(qt)),
                           jnp.where(row >= HEAD_DIM, qt, jnp.zeros_like(qt))], axis=1)
    nqb = tq // LANES
    nkb = tk // LANES

    def logits(c):
        kc = k_ref[pl.ds(pl.multiple_of(c * tk, tk), tk), :]
        bias = jnp.concatenate(
            [jnp.concatenate([_bias_tile(tab_ref, None, qi * nqb + iq, c * nkb + jk)
                              for iq in range(nqb)] * 2, axis=1) for jk in range(nkb)], axis=0)
        return jnp.dot(kc, q12, preferred_element_type=F32) + bias

    def exact_step(c, carry):
        m, l, acc = carry
        s = logits(c)
        m_new = jnp.maximum(m, _reduce_rows(s, jnp.max))
        alpha = jnp.exp2(m - m_new)
        p = jnp.exp2(s - m_new)
        l = alpha * l + _reduce_rows(p, jnp.sum)
        acc = alpha * acc + jnp.dot(vt_ref[c], p.astype(BF16), preferred_element_type=F32)
        return m_new, l, acc

    def lagged_update(c, s, m, l, acc, jump):
        p = jnp.exp2(s - m)
        top =_reduce_rows(s, jnp.max)
        l = l + _reduce_rows(p, jnp.sum)
        acc = acc + jnp.dot(vt_ref[c], p.astype(BF16), preferred_element_type=F32)
        m_new = jnp.maximum(m, top)
        alpha = jnp.exp2(m - m_new)
        return m_new, alpha * l, alpha * acc, jnp.maximum(jump, top - m)

    nch = ((qi + 1) * tq + tk - 1) // tk
    zero = jnp.zeros((1, 2 * tq), F32)
    acc0 = jnp.zeros((B_V_DIM, 2 * tq), F32)
    s0 = logits(0)
    m0 = jnp.max(s0[0:8], axis=0, keepdims=True)
    state = lagged_update(0, s0, m0, zero, acc0, zero)
    def lagged_pair(i, carry):
        c = 1 + 2 * i
        s_a, s_b = logits(c), logits(c + 1)
        return lagged_update(c + 1, s_b, *lagged_update(c, s_a, *carry))

    state = lax.fori_loop(0, (nch - 1) // 2, lagged_pair, state)
    state = lax.cond((nch - 1) % 2 == 1,
                     lambda st: lagged_update(nch - 1, logits(nch - 1), *st), lambda st: st, state)
    _, l, acc, jump = state
    l, acc = lax.cond(jnp.max(jump) > MAX_LAG,
                      lambda: lax.fori_loop(0, nch, exact_step, (jnp.full((1, 2 * tq), M_INIT, F32), zero, acc0))[1:],
                      lambda: (l, acc))
    a1, a2 = acc[:, :tq], acc[:, tq:]
    l1, l2 = l[:, :tq], l[:, tq:]

    lv = lam_ref[...]
    lam = (jnp.exp(jnp.sum(lv[0:1] * lv[1:2], axis=1, keepdims=True))
           - jnp.exp(jnp.sum(lv[2:3] * lv[3:4], axis=1, keepdims=True)) + lv[4:5, 0:1])
    o = a1 / l1 - lam * (a2 / l2)
    ms = jnp.mean(o * o, axis=0, keepdims=True)
    o = o * lax.rsqrt(ms + RMS_EPS) * gn_ref[...]
    o_ref[...] = o.T.astype(BF16)


def _diff_attention(bqt, bk, bvt, tab, lam_rows, gn, batch, t):
    tq = B_Q_TILE
    tk = TOKEN_TILE
    per = tk // tq
    nkt = t // tk
    bqt = bqt.reshape(batch, nkt, B_HEADS * LANES, tk)
    bvt = bvt.reshape(batch, nkt, B_OUT, tk)
    bk = bk.reshape(batch, t, B_HEADS * LANES)
    return pl.pallas_call(
        _diff_kernel,
        out_shape=jax.ShapeDtypeStruct((batch, t, B_OUT), BF16),
        grid=(batch, B_HEADS, t // tq),
        in_specs=[pl.BlockSpec((None, None, LANES, tq), lambda b, h, i: (b, i // per, h, i % per)),
                  pl.BlockSpec((None, t, LANES), lambda b, h, i: (b, 0, h)),
                  pl.BlockSpec((None, nkt, B_V_DIM, tk), lambda b, h, i: (b, 0, h, 0)),
                  pl.BlockSpec((None, FAR + 2, LANES, LANES), lambda b, h, i: (h, 0, 0, 0)),
                  _const_spec(lam_rows.shape), _const_spec(gn.shape)],
        out_specs=pl.BlockSpec((None, tq, B_V_DIM), lambda b, h, i: (b, i, h)),
        compiler_params=_cparams(("parallel", "parallel", "arbitrary")),
        name="diff_attention",
    )(bqt, bk, bvt, tab, lam_rows, gn).reshape(batch * t, B_OUT)


def _dsa_kernel(iqt_ref, iwt_ref, cqt_ref, k_ref, vt_ref, tab_ref, tri_ref, o_ref, s_ref, *, k_sel):
    tk = TOKEN_TILE
    nkb = tk // LANES
    qi = pl.program_id(1)
    nch = qi // nkb + 1
    qpos = qi * LANES + lax.broadcasted_iota(jnp.int32, (1, LANES), 1)
    zeros = jnp.zeros((HEAD_DIM, LANES), BF16)
    iq = iqt_ref[...]
    w = iwt_ref[...]
    iq_all = jnp.concatenate([jnp.concatenate([zeros, iq[h * IDX_DIM:(h + 1) * IDX_DIM]], axis=0)
                              for h in range(IDX_HEADS)], axis=1)
    cq = cqt_ref[...]
    cq_all = jnp.concatenate([jnp.concatenate([cq[h * HEAD_DIM:(h + 1) * HEAD_DIM], zeros], axis=0)
                              for h in range(C_HEADS)], axis=1)

    def chunk(c):
        return pl.ds(pl.multiple_of(c * tk, tk), tk)

    def raw_scores(c):
        return jnp.dot(k_ref[chunk(c), :], iq_all, preferred_element_type=F32)

    def score_chunk(c, raw, mn, mx, last):
        acc = w[0:1, :] * jnp.maximum(raw[:, 0:LANES], 0.0)
        for h in range(1, IDX_HEADS):
            acc = acc + w[h:h + 1, :] * jnp.maximum(raw[:, h * LANES:(h + 1) * LANES], 0.0)
        if last:
            kpos = c * tk + lax.broadcasted_iota(jnp.int32, (tk, LANES), 0)
            causal = kpos <= qpos
            s_ref[chunk(c), :] = jnp.where(causal, acc, NEG)
            mn = jnp.minimum(mn, _fold_rows(jnp.where(causal, acc, BIG), jnp.min))
            mx = jnp.maximum(mx, _fold_rows(jnp.where(causal, acc, NEG), jnp.max))
        else:
            s_ref[chunk(c), :] = acc
            mn = jnp.minimum(mn, _fold_rows(acc, jnp.min))
            mx = jnp.maximum(mx, _fold_rows(acc, jnp.max))
        return mn, mx

    def score_pair(c, carry, last):
        raw_a, raw_b = raw_scores(c), raw_scores(c + 1)
        return score_chunk(c + 1, raw_b, *score_chunk(c, raw_a, *carry, last=False), last=last)

    carry = lax.fori_loop(0, (nch - 1) // 2, lambda i, carry: score_pair(2 * i, carry, last=False),
                          (jnp.full((64, LANES), BIG, F32), jnp.full((64, LANES), NEG, F32)))
    mn, mx = lax.cond((nch - 1) % 2 == 1,
                      lambda st: score_pair(nch - 2, st, last=True),
                      lambda st: score_chunk(nch - 1, raw_scores(nch - 1), *st, last=True), carry)
    lo, hi = _reduce_rows(mn, jnp.min), _reduce_rows(mx, jnp.max)

    def count_gt(thr):
        def body(c, gt):
            return gt + _fold_rows(jnp.where(s_ref[chunk(c), :] > thr, 1.0, 0.0), jnp.sum)
        return _reduce_rows(lax.fori_loop(0, nch, body, jnp.zeros((64, LANES), F32)), jnp.sum)

    def largest_upto(bound):
        def body(c, mx):
            s = s_ref[chunk(c), :]
            return jnp.maximum(mx, _fold_rows(jnp.where(s <= bound, s, NEG), jnp.max))
        return _reduce_rows(lax.fori_loop(0, nch, body, jnp.full((64, LANES), NEG, F32)), jnp.max)

    def next_below_and_multiplicity(cand):
        def body(c, carry):
            mx, eq = carry
            s = s_ref[chunk(c), :]
            mx = jnp.maximum(mx, _fold_rows(jnp.where(s < cand, s, NEG), jnp.max))
            eq = eq + _fold_rows(jnp.where(s == cand, 1.0, 0.0), jnp.sum)
            return mx, eq
        mx, eq = lax.fori_loop(0, nch, body, (jnp.full((64, LANES), NEG, F32), jnp.zeros((64, LANES), F32)))
        return _reduce_rows(mx, jnp.max), _reduce_rows(eq, jnp.sum)

    kf = float(k_sel)
    need = qpos >= k_sel

    def bisect(_, carry):
        lo, hi, above = carry
        mid = lo + (hi - lo) * 0.5
        cnt = count_gt(mid)
        below = cnt < kf
        return jnp.where(below, lo, mid), jnp.where(below, mid, hi), jnp.where(below, cnt, above)

    _, hi, above = lax.fori_loop(0, N_BISECT, bisect, (lo - 1.0, hi, jnp.zeros((1, LANES), F32)))

    def walk_cond(carry):
        _, _, ge = carry
        return jnp.max(jnp.where(need & (ge < kf), 1.0, 0.0)) > 0.0

    def walk_body(carry):
        cand, gt, ge = carry
        active = ge < kf
        nxt, mult = next_below_and_multiplicity(cand)
        ge_new = gt + mult
        moved = active & (ge_new < kf)
        return jnp.where(moved, nxt, cand), jnp.where(moved, ge_new, gt), jnp.where(active, ge_new, ge)

    cand, gt, ge = lax.while_loop(walk_cond, walk_body, (largest_upto(hi), above, above))
    thr = jnp.where(need, cand, THR_ALL)
    want_eq = jnp.where(need, kf - gt, 0.0)

    any_tie = jnp.max(jnp.where(need & (ge > kf), 1.0, 0.0)) > 0.0

    def mark_with_ties(c, eq_seen):
        s = s_ref[chunk(c), :]
        eq = jnp.where(s == thr, 1.0, 0.0)
        rank = eq_seen + jnp.dot(tri_ref[...], eq.astype(BF16), preferred_element_type=F32)
        keep = jnp.where(s > thr, 1.0, jnp.where(rank <= want_eq, eq, 0.0))
        s_ref[chunk(c), :] = jnp.where(keep > 0.5, 0.0, NEG)
        return eq_seen + _reduce_rows(eq, jnp.sum)

    def mark_no_ties(c, carry):
        s_ref[chunk(c), :] = jnp.where(s_ref[chunk(c), :] >= thr, 0.0, NEG)
        return carry

    @pl.when(any_tie)
    def _():
        lax.fori_loop(0, nch, mark_with_ties, jnp.zeros((1, LANES), F32))

    @pl.when(jnp.logical_not(any_tie))
    def _():
        lax.fori_loop(0, nch, mark_no_ties, 0)

    def masked_logits(c):
        sel = s_ref[chunk(c), :]
        bias = jnp.concatenate(
            [jnp.concatenate([_bias_tile(tab_ref, h, qi, c * nkb + jk) for jk in range(nkb)], axis=0) + sel
             for h in range(C_HEADS)], axis=1)
        return jnp.dot(k_ref[chunk(c), :], cq_all, preferred_element_type=F32) + bias

    def exact_step(c, carry):
        m, l, acc = carry
        lg = masked_logits(c)
        m_new = jnp.maximum(m, _reduce_rows(lg, jnp.max))
        alpha = jnp.exp2(m - m_new)
        p = jnp.exp2(lg - m_new)
        l = alpha * l + _reduce_rows(p, jnp.sum)
        acc = alpha * acc + jnp.dot(vt_ref[c], p.astype(BF16), preferred_element_type=F32)
        return m_new, l, acc

    def lagged_update(c, lg, m, l, acc, jump):
        p = jnp.exp2(lg - m)
        top = _reduce_rows(lg, jnp.max)
        l = l + _reduce_rows(p, jnp.sum)
        acc = acc + jnp.dot(vt_ref[c], p.astype(BF16), preferred_element_type=F32)
        m_new = jnp.maximum(m, top)
        alpha = jnp.exp2(m - m_new)
        return m_new, alpha * l, alpha * acc, jnp.maximum(jump, top - m)

    wide = C_HEADS * LANES
    init = (jnp.full((1, wide), M_INIT, F32), jnp.zeros((1, wide), F32), jnp.zeros((HEAD_DIM, wide), F32))
    first = exact_step(0, init)

    def lagged_pair(i, carry):
        c = 1 + 2 * i
        lg_a, lg_b = masked_logits(c), masked_logits(c + 1)
        return lagged_update(c + 1, lg_b, *lagged_update(c, lg_a, *carry))

    state = lax.fori_loop(0, (nch - 1) // 2, lagged_pair, first + (jnp.zeros((1, wide), F32),))
    state = lax.cond((nch - 1) % 2 == 1,
                     lambda st: lagged_update(nch - 1, masked_logits(nch - 1), *st), lambda st: st, state)
    _, l, acc, jump = state
    l, acc = lax.cond(jnp.max(jump) > MAX_LAG,
                      lambda: lax.fori_loop(1, nch, exact_step, first)[1:],
                      lambda: (l, acc))
    o = acc / l
    ot = jnp.concatenate([o[:, h * LANES:(h + 1) * LANES] for h in range(C_HEADS)], axis=0)
    o_ref[...] = ot.T.astype(BF16)


def _dsa_attention(iqt, iwt, cqt, ck, cvt, tab, batch, t):
    tk = TOKEN_TILE
    per = tk // LANES
    nkt = t // tk
    k_sel = min(TOPK_MAX, t // 4)
    iqt = iqt.reshape(batch, nkt, IDX_HEADS * IDX_DIM, tk)
    iwt = iwt.reshape(batch, nkt, 16, tk)
    cqt = cqt.reshape(batch, nkt, C_OUT, tk)
    cvt = cvt.reshape(batch, nkt, HEAD_DIM, tk)
    ck = ck.reshape(batch, t, LANES)
    r = np.arange(tk)
    tri = jnp.asarray(r[:, None] >= r[None, :], BF16)
    qblock = lambda rows: pl.BlockSpec((None, None, rows, LANES), lambda b, i: (b, i // per, 0, i % per))
    return pl.pallas_call(
        functools.partial(_dsa_kernel, k_sel=k_sel),
        out_shape=jax.ShapeDtypeStruct((batch, t, C_OUT), BF16),
        grid=(batch, t // LANES),
        in_specs=[qblock(IDX_HEADS * IDX_DIM), qblock(16), qblock(C_OUT),
                  pl.BlockSpec((None, t, LANES), lambda b, i: (b, 0, 0)),
                  pl.BlockSpec((None, nkt, HEAD_DIM, tk), lambda b, i: (b, 0, 0, 0)),
                  _const_spec(tab.shape), _const_spec(tri.shape)],
        out_specs=pl.BlockSpec((None, LANES, C_OUT), lambda b, i: (b, i, 0)),
        scratch_shapes=[pltpu.VMEM((t, LANES), F32)],
        compiler_params=_cparams(("parallel", "arbitrary")),
        name="dsa_attention",
    )(iqt, iwt, cqt, ck, cvt, tab, tri).reshape(batch * t, C_OUT)


def _merge_kernel(x_ref, *refs):
    ng = len(DIL_GROUPS)
    a_refs = refs[:2 * ng]
    ob_ref, oc_ref, gate_ref, wa_ref, wb_ref, wc_ref, wo_ref, out_ref = refs[2 * ng:-1]
    shuffle_ref = refs[-1]
    tm, d = x_ref.shape

    def token_order(ref, slot):
        dil = ref.shape[0]
        if dil == 1:
            return ref[0]
        halves = range(A_OUT // LANES)
        for r in range(dil):
            for half in halves:
                shuffle_ref[slot, half, pl.ds(r, tm // dil, stride=dil), :] = ref[r, :, half * LANES:(half + 1) * LANES]
        return jnp.concatenate([shuffle_ref[slot, half] for half in halves], axis=1)

    outs = [token_order(a_refs[2 * g], 2 * g) for g in range(ng)]
    lses = [token_order(a_refs[2 * g + 1], 2 * g + 1) for g in range(ng)]
    top = functools.reduce(jnp.maximum, lses)
    es = [jnp.exp(lse - top) for lse in lses]
    num = sum(e * o for e, o in zip(es, outs))
    oa = (num / sum(es)).astype(BF16)
    y = gate_ref[:, 0:d].astype(F32) * jnp.dot(oa, wa_ref[...], preferred_element_type=F32)
    y = y + gate_ref[:, d:2 * d].astype(F32) * jnp.dot(ob_ref[...], wb_ref[...], preferred_element_type=F32)
    y = y + gate_ref[:, 2 * d:3 * d].astype(F32) * jnp.dot(oc_ref[...], wc_ref[...], preferred_element_type=F32)
    out_ref[...] = x_ref[...] + jnp.dot(y.astype(BF16), wo_ref[...], preferred_element_type=F32)


def _merge(x, a_parts, ob, oc, gates, wa, wb, wc, wo, t):
    n, d = x.shape
    tm = TOKEN_TILE
    per_batch = t // tm
    tok = lambda c: pl.BlockSpec((tm, c), lambda i: (i, 0))
    by_residue = lambda dil: pl.BlockSpec((None, dil, tm // dil, A_OUT),
                                          lambda i: (i // per_batch, 0, i % per_batch, 0))
    ws = [w.astype(BF16) for w in (wa, wb, wc, wo)]
    return pl.pallas_call(
        _merge_kernel,
        out_shape=jax.ShapeDtypeStruct((n, d), F32),
        grid=(n // tm,),
        in_specs=[tok(d)] + [by_residue(z.shape[1]) for z in a_parts] + [tok(B_OUT), tok(C_OUT), tok(3 * d)]
                 + [_const_spec(w.shape) for w in ws],
        out_specs=tok(d),
        scratch_shapes=[pltpu.VMEM((len(a_parts), A_OUT // LANES, tm, LANES), F32)],
        compiler_params=_cparams(("parallel",)),
        name="merge",
    )(x, *a_parts, ob, oc, gates, *ws)


def _token_mixer(x, batch, t, layer, mix_norm, w_in, qk_gain, diff_lambda, diff_out_norm,
                 w_branch_a, w_branch_b, w_branch_c, w_out, band_tabs, tab_b, tab_c):
    ng = len(DIL_GROUPS)
    outs = _project(x, mix_norm, w_in, qk_gain, batch, t)
    a_in, (bk, ck, gates, bqt, bvt, cqt, cvt, iqt, iwt) = outs[:3 * ng], outs[3 * ng:]
    a_parts = []
    for g, (_, dilation) in enumerate(DIL_GROUPS):
        a_parts += _dilated_group(a_in[g], a_in[ng + g], a_in[2 * ng + g], band_tabs[g], dilation)
    lam_init = 0.8 - 0.6 * np.exp(-0.3 * layer)
    lam_rows = jnp.concatenate([diff_lambda.astype(F32), jnp.full((4, HEAD_DIM), lam_init, F32)], axis=0)
    gn = jnp.broadcast_to((diff_out_norm.astype(F32) * (1.0 - lam_init))[:, None], (B_V_DIM, B_Q_TILE))
    ob = _diff_attention(bqt, bk, bvt, tab_b, lam_rows, gn, batch, t)
    oc = _dsa_attention(iqt, iwt, cqt, ck, cvt, tab_c, batch, t)
    return _merge(x, a_parts, ob, oc, gates, w_branch_a, w_branch_b, w_branch_c, w_out, t)


def kernel(x, rel_bias, ffn1_norm, ffn1_w_gate, ffn1_w_up, ffn1_w_down, mix_norm, w_in, qk_gain,
           diff_lambda, diff_out_norm, w_branch_a, w_branch_b, w_branch_c, w_out,
           ffn2_norm, ffn2_w_gate, ffn2_w_up, ffn2_w_down):
    batch, t, d = x.shape
    depth = w_in.shape[0]
    assert t % (DIL_GROUPS[-1][1] * LANES) == 0 and t % TOKEN_TILE == 0
    band_tabs = [_band_tables(rel_bias[:, g * A_GROUP_HEADS:(g + 1) * A_GROUP_HEADS], dil)
                 for g, (_, dil) in enumerate(DIL_GROUPS)]
    tab_b = _toeplitz_tables(rel_bias[:, A_HEADS:A_HEADS + B_HEADS] * LOG2E)
    tab_c = _toeplitz_tables(rel_bias[:, A_HEADS + B_HEADS:] * LOG2E)
    h = x.reshape(batch * t, d).astype(F32)
    for i in range(depth):
        h = _ffn(h, ffn1_norm[i], ffn1_w_gate[i], ffn1_w_up[i], ffn1_w_down[i])
        h = _token_mixer(h, batch, t, i, mix_norm[i], w_in[i], qk_gain[i], diff_lambda[i], diff_out_norm[i],
                         w_branch_a[i], w_branch_b[i], w_branch_c[i], w_out[i], band_tabs, tab_b, tab_c)
        h = _ffn(h, ffn2_norm[i], ffn2_w_gate[i], ffn2_w_up[i], ffn2_w_down[i])
    return h.reshape(batch, t, d).astype(x.dtype)
```

```python
import functools

import numpy as np
import jax
import jax.numpy as jnp
from jax import lax
from jax.experimental import pallas as pl
from jax.experimental.pallas import tpu as pltpu

F32 = jnp.float32
BF16 = jnp.bfloat16

HEAD_DIM = 64
DIL_GROUPS = ((128, 1), (512, 4), (2048, 16))
A_GROUP_HEADS = 4
A_HEADS = A_GROUP_HEADS * len(DIL_GROUPS)
A_OUT = A_GROUP_HEADS * HEAD_DIM
B_HEADS = 4
B_V_DIM = 2 * HEAD_DIM
B_OUT = B_HEADS * B_V_DIM
C_HEADS = 4
C_OUT = C_HEADS * HEAD_DIM
IDX_HEADS = 8
IDX_DIM = 64
TOPK_MAX = 256
NUM_BUCKETS = 32
MAX_DISTANCE = 2048
RMS_EPS = 1e-6
LOG2E = 1.4426950408889634

LANES = 128
TOKEN_TILE = 512
FFN_TILE = 1024
A_BLOCKS_PER_STEP = 4
B_Q_TILE = 512
C_Q_TILE = 256
MAX_LAG = 60.0
FFN_CHUNK = 256
VMEM_LIMIT = 58 * 1024 * 1024

NEG = -1e30
M_INIT = -1e29
BIG = 1e30
THR_ALL = -1e29
N_BISECT = 14


def _cparams(sem):
    return pltpu.CompilerParams(dimension_semantics=sem, vmem_limit_bytes=VMEM_LIMIT)


def _const_spec(shape):
    nd = len(shape)
    return pl.BlockSpec(shape, lambda *_: (0,) * nd, pipeline_mode=pl.Buffered(1))


def _rel_bucket_np(dist):
    n = np.maximum(dist, 0)
    max_exact = NUM_BUCKETS // 2
    nf = np.maximum(n, 1).astype(np.float64)
    large = max_exact + (np.log(nf / max_exact) / np.log(MAX_DISTANCE / max_exact)
                         * (NUM_BUCKETS - max_exact)).astype(np.int64)
    large = np.minimum(large, NUM_BUCKETS - 1)
    return np.where(n < max_exact, n, large)


def _far_delta():
    d = 1
    while not np.all(_rel_bucket_np(np.arange(d * LANES - LANES + 1, d * LANES + LANES)) == NUM_BUCKETS - 1):
        d += 1
    return d


FAR = _far_delta()
MASKED = FAR + 1


def _toeplitz(w, n_rows, n_cols):
    period = n_rows + n_cols
    w = jnp.pad(w, ((0, 0), (0, period - w.shape[1])))
    m = jnp.tile(w, (1, n_rows))[:, :n_rows * (period - 1)].reshape(-1, n_rows, period - 1)
    return m[:, :, n_rows - 1:n_rows - 1 + n_cols]


def _bias_by_distance(bias_heads, dist, valid):
    vals = jnp.take(bias_heads.astype(F32), jnp.asarray(_rel_bucket_np(dist), jnp.int32), axis=0).T
    return jnp.where(jnp.asarray(valid)[None], vals, NEG)


def _toeplitz_tables(bias_heads):
    n_cols = (FAR + 1) * LANES
    dist = np.arange(LANES - 1 + n_cols) - (LANES - 1)
    tiles = _toeplitz(_bias_by_distance(bias_heads, dist, dist >= 0), LANES, n_cols)
    tiles = tiles.reshape(-1, LANES, FAR + 1, LANES).transpose(0, 2, 1, 3)
    masked = jnp.full((tiles.shape[0], 1, LANES, LANES), NEG, F32)
    return jnp.concatenate([tiles, masked], axis=1)


def _band_tables(bias_heads, dilation):
    wn = LANES
    sub = np.arange(3 * wn - 1) - (wn - 1)
    w = _bias_by_distance(bias_heads, sub * dilation, (sub >= 0) & (sub <= wn))
    later = jnp.flip(_toeplitz(w, wn, 2 * wn), axis=(1, 2))
    first = jnp.where(jnp.asarray(np.arange(2 * wn) >= wn)[None, None], later, NEG)
    return jnp.stack([first, later])


def _ffn_kernel(x_ref, g_ref, wg_ref, wu_ref, wd_ref, o_ref, acc_ref):
    x = x_ref[...]
    ms = jnp.mean(x * x, axis=-1, keepdims=True)
    h = (x * lax.rsqrt(ms + RMS_EPS) * g_ref[...]).astype(BF16)
    acc_ref[...] = jnp.zeros_like(acc_ref)

    def body(c, carry):
        cols = pl.ds(pl.multiple_of(c * FFN_CHUNK, FFN_CHUNK), FFN_CHUNK)
        g = jnp.dot(h, wg_ref[:, cols], preferred_element_type=F32)
        u = jnp.dot(h, wu_ref[:, cols], preferred_element_type=F32)
        a = (g * jax.nn.sigmoid(g) * u).astype(BF16)
        acc_ref[...] += jnp.dot(a, wd_ref[cols, :], preferred_element_type=F32)
        return carry

    lax.fori_loop(0, wg_ref.shape[1] // FFN_CHUNK, body, 0)
    o_ref[...] = x + 0.5 * acc_ref[...]


def _ffn(x, gain, w_gate, w_up, w_down):
    n, d = x.shape
    f = w_gate.shape[1]
    tm = FFN_TILE
    return pl.pallas_call(
        _ffn_kernel,
        out_shape=jax.ShapeDtypeStruct((n, d), F32),
        grid=(n // tm,),
        in_specs=[pl.BlockSpec((tm, d), lambda i: (i, 0)),
                  _const_spec((1, d)), _const_spec((d, f)), _const_spec((d, f)), _const_spec((f, d))],
        out_specs=pl.BlockSpec((tm, d), lambda i: (i, 0)),
        scratch_shapes=[pltpu.VMEM((tm, d), F32)],
        compiler_params=_cparams(("parallel",)),
        name="ffn",
    )(x, gain.reshape(1, d).astype(F32), w_gate.astype(BF16), w_up.astype(BF16), w_down.astype(BF16))


S_AQ, S_AK, S_AV = 0, 768, 1536
S_BK, S_CK, S_GATE, S_END = 2304, 2816, 2944, 6016
T_BQ, T_BV, T_CQ, T_CV, T_IQ, T_IW, T_END = 0, 512, 1024, 1280, 1344, 1856, 1872


def _proj_kernel(x_ref, g_ref, ws_ref, wt_ref, bd_ref, gs_ref, gt_ref, *refs):
    ng = len(DIL_GROUPS)
    a_refs = refs[:3 * ng]
    bk_ref, ck_ref, gate_ref, bqt_ref, bvt_ref, cqt_ref, cvt_ref, iqt_ref, iwt_ref = refs[3 * ng:-1]
    shuffle_ref = refs[-1]
    tm = x_ref.shape[0]

    def store_by_residue(y, which):
        for g, (_, dil) in enumerate(DIL_GROUPS):
            out = a_refs[which * ng + g]
            part = y[:, g * A_OUT:(g + 1) * A_OUT]
            if dil == 1:
                out[0] = part.astype(BF16)
            else:
                for half in range(A_OUT // LANES):
                    shuffle_ref[half] = part[:, half * LANES:(half + 1) * LANES]
                for r in range(dil):
                    out[r] = jnp.concatenate(
                        [shuffle_ref[half, pl.ds(r, tm // dil, stride=dil), :] for half in range(A_OUT // LANES)],
                        axis=1).astype(BF16)

    x = x_ref[...]
    ms = jnp.mean(x * x, axis=-1, keepdims=True)
    h = (x * lax.rsqrt(ms + RMS_EPS) * g_ref[...]).astype(BF16)
    bd = bd_ref[...]

    def dot_s(c0, c1):
        return jnp.dot(h, ws_ref[:, c0:c1], preferred_element_type=F32)

    def head_inv_rms(y):
        width = bd.shape[0]
        outs = []
        for c0 in range(0, y.shape[1], width):
            sq = y[:, c0:c0 + width]
            n = sq.shape[1]
            msq = jnp.dot((sq * sq).astype(BF16), bd[:n, :n], preferred_element_type=F32)
            outs.append(lax.rsqrt(msq + RMS_EPS))
        return outs[0] if len(outs) == 1 else jnp.concatenate(outs, axis=1)

    y = dot_s(S_AQ, S_AK)
    store_by_residue(y * head_inv_rms(y) * gs_ref[:, 0:768], 0)
    y = dot_s(S_AK, S_AV)
    store_by_residue(y * head_inv_rms(y) * gs_ref[:, 768:1536], 1)
    store_by_residue(dot_s(S_AV, S_BK), 2)
    y = dot_s(S_BK, S_CK)
    bk_ref[...] = (y * head_inv_rms(y) * gs_ref[:, 1536:2048]).astype(BF16)
    y = dot_s(S_CK, S_GATE)
    lane = lax.broadcasted_iota(jnp.int32, y.shape, 1)
    inv = jnp.where(lane < HEAD_DIM, head_inv_rms(y), 1.0)
    ck_ref[...] = (y * inv * gs_ref[:, 2048:2176]).astype(BF16)
    for c in range(3):
        y = dot_s(S_GATE + c * 1024, S_GATE + (c + 1) * 1024)
        gate_ref[:, c * 1024:(c + 1) * 1024] = jax.nn.sigmoid(y).astype(BF16)

    def dot_t(r0, r1):
        return lax.dot_general(wt_ref[r0:r1, :], h, (((1,), (1,)), ((), ())),
                               preferred_element_type=F32)

    def norm_t(y, gain):
        r = y.shape[0] // HEAD_DIM
        y3 = y.reshape(r, HEAD_DIM, tm)
        msq = jnp.mean(y3 * y3, axis=1, keepdims=True)
        return (y3 * lax.rsqrt(msq + RMS_EPS)).reshape(r * HEAD_DIM, tm) * gain

    bqt_ref[...] = norm_t(dot_t(T_BQ, T_BV), gt_ref[0:512, :]).astype(BF16)
    bvt_ref[...] = dot_t(T_BV, T_CQ).astype(BF16)
    cqt_ref[...] = norm_t(dot_t(T_CQ, T_CV), gt_ref[512:768, :]).astype(BF16)
    cvt_ref[...] = dot_t(T_CV, T_IQ).astype(BF16)
    iqt_ref[...] = dot_t(T_IQ, T_IW).astype(BF16)
    iwt_ref[...] = dot_t(T_IW, T_END) * (IDX_HEADS ** -0.5 * IDX_DIM ** -0.5)


def _proj_weights(w_in, qk_gain, tm):
    d = w_in.shape[0]
    o = 0
    a_qkv = w_in[:, o:o + 3 * A_HEADS * HEAD_DIM].reshape(d, 3, A_HEADS * HEAD_DIM)
    o += 3 * A_HEADS * HEAD_DIM
    b_qk = w_in[:, o:o + 4 * B_HEADS * HEAD_DIM].reshape(d, 4, B_HEADS, HEAD_DIM)
    o += 4 * B_HEADS * HEAD_DIM
    b_v = w_in[:, o:o + B_OUT]
    o += B_OUT
    c_q = w_in[:, o:o + C_OUT]
    c_k = w_in[:, o + C_OUT:o + C_OUT + HEAD_DIM]
    c_v = w_in[:, o + C_OUT + HEAD_DIM:o + C_OUT + 2 * HEAD_DIM]
    o += C_OUT + 2 * HEAD_DIM
    i_q = w_in[:, o:o + IDX_HEADS * IDX_DIM]
    i_k = w_in[:, o + IDX_HEADS * IDX_DIM:o + IDX_HEADS * IDX_DIM + IDX_DIM]
    i_w = w_in[:, o + IDX_HEADS * IDX_DIM + IDX_DIM:o + IDX_HEADS * IDX_DIM + IDX_DIM + IDX_HEADS]
    o += IDX_HEADS * IDX_DIM + IDX_DIM + IDX_HEADS
    gates = w_in[:, o:]
    b_k = jnp.stack([b_qk[:, 2], b_qk[:, 3]], axis=2).reshape(d, 2 * B_HEADS * HEAD_DIM)
    b_q = jnp.stack([b_qk[:, 0], b_qk[:, 1]], axis=2).reshape(d, 2 * B_HEADS * HEAD_DIM)
    w_s = jnp.concatenate([a_qkv[:, 0], a_qkv[:, 1], a_qkv[:, 2], b_k, c_k, i_k, gates], axis=1)
    w_t = jnp.concatenate([b_q, b_v, c_q, c_v, i_q, i_w, jnp.zeros((d, 8), w_in.dtype)], axis=1).T
    assert w_s.shape[1] == S_END and w_t.shape[0] == T_END
    scale = HEAD_DIM ** -0.5
    g = qk_gain.astype(F32)
    gs = jnp.concatenate([jnp.tile(g[0, 0] * scale, A_HEADS), jnp.tile(g[0, 1], A_HEADS),
                          jnp.tile(g[1, 1], 2 * B_HEADS), g[2, 1], jnp.ones((IDX_DIM,), F32)])[None]
    gt = jnp.concatenate([jnp.tile(g[1, 0] * (scale * LOG2E), 2 * B_HEADS),
                          jnp.tile(g[2, 0] * (scale * LOG2E), C_HEADS)])
    gt = jnp.broadcast_to(gt[:, None], (gt.shape[0], tm))
    return w_s.astype(BF16), w_t.astype(BF16), gs, gt


MXU_TILE = 256


def _head_block_diag():
    r = np.arange(MXU_TILE)
    return jnp.asarray((r[:, None] // HEAD_DIM == r[None, :] // HEAD_DIM) / HEAD_DIM, BF16)


def _project(x, gain, w_in, qk_gain, batch, t):
    n, d = x.shape
    tm = TOKEN_TILE
    nt = n // tm
    per_batch = t // tm
    w_s, w_t, gs, gt = _proj_weights(w_in, qk_gain, tm)
    tok = lambda c: pl.BlockSpec((tm, c), lambda i: (i, 0))
    feat = lambda r: pl.BlockSpec((None, r, tm), lambda i: (i, 0, 0))
    a_shapes, a_specs = [], []
    for _ in range(3):
        for _, dil in DIL_GROUPS:
            a_shapes.append(jax.ShapeDtypeStruct((batch, dil, t // dil, A_OUT), BF16))
            a_specs.append(pl.BlockSpec((None, dil, tm // dil, A_OUT),
                                        lambda i: (i // per_batch, 0, i % per_batch, 0)))
    out_shape = a_shapes + [
        jax.ShapeDtypeStruct((n, 512), BF16), jax.ShapeDtypeStruct((n, 128), BF16),
        jax.ShapeDtypeStruct((n, 3072), BF16),
        jax.ShapeDtypeStruct((nt, 512, tm), BF16), jax.ShapeDtypeStruct((nt, 512, tm), BF16),
        jax.ShapeDtypeStruct((nt, 256, tm), BF16), jax.ShapeDtypeStruct((nt, 64, tm), BF16),
        jax.ShapeDtypeStruct((nt, 512, tm), BF16), jax.ShapeDtypeStruct((nt, 16, tm), F32)]
    out_specs = a_specs + [tok(512), tok(128), tok(3072),
                           feat(512), feat(512), feat(256), feat(64), feat(512), feat(16)]
    return pl.pallas_call(
        _proj_kernel,
        out_shape=out_shape,
        grid=(nt,),
        in_specs=[tok(d), _const_spec((1, d)), _const_spec(w_s.shape), _const_spec(w_t.shape),
                  _const_spec((MXU_TILE, MXU_TILE)), _const_spec(gs.shape), _const_spec(gt.shape)],
        out_specs=out_specs,
        scratch_shapes=[pltpu.VMEM((A_OUT // LANES, tm, LANES), F32)],
        compiler_params=_cparams(("parallel",)),
        name="proj",
    )(x, gain.reshape(1, d).astype(F32), w_s, w_t, _head_block_diag(), gs, gt)


def _dil_kernel(q_ref, kp_ref, kc_ref, vp_ref, vc_ref, bias_ref, o_ref, lse_ref):
    nq = q_ref.shape[0] // LANES
    qi = pl.program_id(2)
    lane = lax.broadcasted_iota(jnp.int32, (LANES, A_OUT), 1) // HEAD_DIM
    mine = [lane == h for h in range(A_GROUP_HEADS)]
    blocks = [slice(jb * LANES, (jb + 1) * LANES) for jb in range(nq)]

    def band(prev_ref, cur_ref, jb):
        if jb == 0:
            return jnp.concatenate([prev_ref[...], cur_ref[blocks[0], :]], axis=0)
        return cur_ref[(jb - 1) * LANES:(jb + 1) * LANES, :]

    logits = []
    for jb in range(nq):
        q = q_ref[blocks[jb], :]
        q4 = jnp.concatenate([jnp.where(mine[h], q, jnp.zeros_like(q)) for h in range(A_GROUP_HEADS)], axis=0)
        s = lax.dot_general(q4, band(kp_ref, kc_ref, jb), (((1,), (1,)), ((), ())), preferred_element_type=F32)
        bias = bias_ref[jnp.minimum(qi, 1) if jb == 0 else 1]
        logits.append(s + bias.reshape(A_GROUP_HEADS * LANES, 2 * LANES))
    probs, stats = [], []
    for s in logits:
        m = jnp.max(s, axis=1, keepdims=True)
        p = jnp.exp(s - m)
        ssum = jnp.sum(p, axis=1, keepdims=True)
        probs.append(p.astype(BF16))
        stats.append((1.0 / ssum, m + jnp.log(ssum)))
    for jb in range(nq):
        pv = jnp.dot(probs[jb], band(vp_ref, vc_ref, jb), preferred_element_type=F32)
        inv, lse4 = stats[jb]
        o = jnp.zeros((LANES, A_OUT), F32)
        lse = jnp.zeros((LANES, A_OUT), F32)
        for h in range(A_GROUP_HEADS):
            head = slice(h * LANES, (h + 1) * LANES)
            o = jnp.where(mine[h], pv[head] * inv[head], o)
            lse = jnp.where(mine[h], lse4[head], lse)
        o_ref[blocks[jb], :] = o
        lse_ref[blocks[jb], :] = lse


def _dilated_group(aq, ak, av, bias, dilation):
    batch, _, n, _ = aq.shape
    nblk = n // LANES
    nq = min(nblk, A_BLOCKS_PER_STEP)
    qt = nq * LANES
    cur = pl.BlockSpec((None, None, qt, A_OUT), lambda b, r, i: (b, r, i, 0))
    prev = pl.BlockSpec((None, None, LANES, A_OUT), lambda b, r, i: (b, r, jnp.maximum(i * nq - 1, 0), 0))
    shp = jax.ShapeDtypeStruct((batch, dilation, n, A_OUT), F32)
    return pl.pallas_call(
        _dil_kernel,
        out_shape=[shp, shp],
        grid=(batch, dilation, nblk // nq),
        in_specs=[cur, prev, cur, prev, cur, _const_spec(bias.shape)],
        out_specs=[cur, cur],
        compiler_params=_cparams(("parallel", "parallel", "parallel")),
        name=f"dilated_d{dilation}",
    )(aq, ak, ak, av, av, bias)


def _fold_rows(x, op):
    r, c = x.shape
    return op(x.reshape(r // 64, 64, c), axis=0) if r > 64 else x


def _reduce_rows(x, op):
    x = _fold_rows(x, op)
    x = op(x.reshape(8, 8, x.shape[1]), axis=0)
    return op(x, axis=0, keepdims=True)


def _bias_tile(tab_ref, head, qblk, kblk):
    delta = qblk - kblk
    idx = jnp.where(delta < 0, MASKED, jnp.minimum(delta, FAR))
    if head is None:
        return tab_ref[idx]
    return tab_ref[head, idx]


def _diff_kernel(qt_ref, k_ref, vt_ref, tab_ref, lam_ref, gn_ref, o_ref):
    tq = qt_ref.shape[1]
    tk = TOKEN_TILE
    qi = pl.program_id(2)
    qt = qt_ref[...]
    row = lax.broadcasted_iota(jnp.int32, qt.shape, 0)
    q12 = jnp.concatenate([jnp.where(row < HEAD_DIM, qt, jnp.zeros_like(qt)),
                           jnp.where(row >= HEAD_DIM, qt, jnp.zeros_like(qt))], axis=1)
    nqb = tq // LANES
    nkb = tk // LANES

    def logits(c):
        kc = k_ref[pl.ds(pl.multiple_of(c * tk, tk), tk), :]
        bias = jnp.concatenate(
            [jnp.concatenate([_bias_tile(tab_ref, None, qi * nqb + iq, c * nkb + jk)
                              for iq in range(nqb)] * 2, axis=1) for jk in range(nkb)], axis=0)
        return jnp.dot(kc, q12, preferred_element_type=F32) + bias

    def exact_step(c, carry):
        m, l, acc = carry
        s = logits(c)
        m_new = jnp.maximum(m, _reduce_rows(s, jnp.max))
        alpha = jnp.exp2(m - m_new)
        p = jnp.exp2(s - m_new)
        l = alpha * l + _reduce_rows(p, jnp.sum)
        acc = alpha * acc + jnp.dot(vt_ref[c], p.astype(BF16), preferred_element_type=F32)
        return m_new, l, acc

    def lagged_update(c, s, m, l, acc, jump):
        p = jnp.exp2(s - m)
        top =_reduce_rows(s, jnp.max)
        l = l + _reduce_rows(p, jnp.sum)
        acc = acc + jnp.dot(vt_ref[c], p.astype(BF16), preferred_element_type=F32)
        m_new = jnp.maximum(m, top)
        alpha = jnp.exp2(m - m_new)
        return m_new, alpha * l, alpha * acc, jnp.maximum(jump, top - m)

    nch = ((qi + 1) * tq + tk - 1) // tk
    zero = jnp.zeros((1, 2 * tq), F32)
    acc0 = jnp.zeros((B_V_DIM, 2 * tq), F32)
    s0 = logits(0)
    m0 = jnp.max(s0[0:8], axis=0, keepdims=True)
    state = lagged_update(0, s0, m0, zero, acc0, zero)
    def lagged_pair(i, carry):
        c = 1 + 2 * i
        s_a, s_b = logits(c), logits(c + 1)
        return lagged_update(c + 1, s_b, *lagged_update(c, s_a, *carry))

    state = lax.fori_loop(0, (nch - 1) // 2, lagged_pair, state)
    state = lax.cond((nch - 1) % 2 == 1,
                     lambda st: lagged_update(nch - 1, logits(nch - 1), *st), lambda st: st, state)
    _, l, acc, jump = state
    l, acc = lax.cond(jnp.max(jump) > MAX_LAG,
                      lambda: lax.fori_loop(0, nch, exact_step, (jnp.full((1, 2 * tq), M_INIT, F32), zero, acc0))[1:],
                      lambda: (l, acc))
    a1, a2 = acc[:, :tq], acc[:, tq:]
    l1, l2 = l[:, :tq], l[:, tq:]

    lv = lam_ref[...]
    lam = (jnp.exp(jnp.sum(lv[0:1] * lv[1:2], axis=1, keepdims=True))
           - jnp.exp(jnp.sum(lv[2:3] * lv[3:4], axis=1, keepdims=True)) + lv[4:5, 0:1])
    o = a1 / l1 - lam * (a2 / l2)
    ms = jnp.mean(o * o, axis=0, keepdims=True)
    o = o * lax.rsqrt(ms + RMS_EPS) * gn_ref[...]
    o_ref[...] = o.T.astype(BF16)


def _diff_attention(bqt, bk, bvt, tab, lam_rows, gn, batch, t):
    tq = B_Q_TILE
    tk = TOKEN_TILE
    per = tk // tq
    nkt = t // tk
    bqt = bqt.reshape(batch, nkt, B_HEADS * LANES, tk)
    bvt = bvt.reshape(batch, nkt, B_OUT, tk)
    bk = bk.reshape(batch, t, B_HEADS * LANES)
    return pl.pallas_call(
        _diff_kernel,
        out_shape=jax.ShapeDtypeStruct((batch, t, B_OUT), BF16),
        grid=(batch, B_HEADS, t // tq),
        in_specs=[pl.BlockSpec((None, None, LANES, tq), lambda b, h, i: (b, i // per, h, i % per)),
                  pl.BlockSpec((None, t, LANES), lambda b, h, i: (b, 0, h)),
                  pl.BlockSpec((None, nkt, B_V_DIM, tk), lambda b, h, i: (b, 0, h, 0)),
                  pl.BlockSpec((None, FAR + 2, LANES, LANES), lambda b, h, i: (h, 0, 0, 0)),
                  _const_spec(lam_rows.shape), _const_spec(gn.shape)],
        out_specs=pl.BlockSpec((None, tq, B_V_DIM), lambda b, h, i: (b, i, h)),
        compiler_params=_cparams(("parallel", "parallel", "arbitrary")),
        name="diff_attention",
    )(bqt, bk, bvt, tab, lam_rows, gn).reshape(batch * t, B_OUT)


def _dsa_kernel(iqt_ref, iwt_ref, cqt_ref, k_ref, vt_ref, tab_ref, tri_ref, o_ref, s_ref, *, k_sel):
    tk = TOKEN_TILE
    nkb = tk // LANES
    qw = o_ref.shape[0]
    nqb = qw // LANES
    qi = pl.program_id(1)
    nch = ((qi + 1) * qw + tk - 1) // tk
    qpos = qi * qw + lax.broadcasted_iota(jnp.int32, (1, qw), 1)
    zeros = jnp.zeros((HEAD_DIM, qw), BF16)
    iq = iqt_ref[...]
    w = iwt_ref[...]
    iq_all = jnp.concatenate([jnp.concatenate([zeros, iq[h * IDX_DIM:(h + 1) * IDX_DIM]], axis=0)
                              for h in range(IDX_HEADS)], axis=1)
    cq = cqt_ref[...]
    cq_all = jnp.concatenate([jnp.concatenate([cq[h * HEAD_DIM:(h + 1) * HEAD_DIM], zeros], axis=0)
                              for h in range(C_HEADS)], axis=1)

    def chunk(c):
        return pl.ds(pl.multiple_of(c * tk, tk), tk)

    def raw_scores(c):
        return jnp.dot(k_ref[chunk(c), :], iq_all, preferred_element_type=F32)

    def score_chunk(c, raw, mn, mx, last):
        acc = w[0:1, :] * jnp.maximum(raw[:, 0:qw], 0.0)
        for h in range(1, IDX_HEADS):
            acc = acc + w[h:h + 1, :] * jnp.maximum(raw[:, h * qw:(h + 1) * qw], 0.0)
        if last:
            kpos = c * tk + lax.broadcasted_iota(jnp.int32, (tk, qw), 0)
            causal = kpos <= qpos
            s_ref[chunk(c), :] = jnp.where(causal, acc, NEG)
            mn = jnp.minimum(mn, _fold_rows(jnp.where(causal, acc, BIG), jnp.min))
            mx = jnp.maximum(mx, _fold_rows(jnp.where(causal, acc, NEG), jnp.max))
        else:
            s_ref[chunk(c), :] = acc
            mn = jnp.minimum(mn, _fold_rows(acc, jnp.min))
            mx = jnp.maximum(mx, _fold_rows(acc, jnp.max))
        return mn, mx

    def score_pair(c, carry, last):
        raw_a, raw_b = raw_scores(c), raw_scores(c + 1)
        return score_chunk(c + 1, raw_b, *score_chunk(c, raw_a, *carry, last=False), last=last)

    carry = lax.fori_loop(0, (nch - 1) // 2, lambda i, carry: score_pair(2 * i, carry, last=False),
                          (jnp.full((64, qw), BIG, F32), jnp.full((64, qw), NEG, F32)))
    mn, mx = lax.cond((nch - 1) % 2 == 1,
                      lambda st: score_pair(nch - 2, st, last=True),
                      lambda st: score_chunk(nch - 1, raw_scores(nch - 1), *st, last=True), carry)
    lo, hi = _reduce_rows(mn, jnp.min), _reduce_rows(mx, jnp.max)

    def count_gt(thr):
        def body(c, gt):
            return gt + _fold_rows(jnp.where(s_ref[chunk(c), :] > thr, 1.0, 0.0), jnp.sum)
        return _reduce_rows(lax.fori_loop(0, nch, body, jnp.zeros((64, qw), F32)), jnp.sum)

    def largest_upto(bound):
        def body(c, mx):
            s = s_ref[chunk(c), :]
            return jnp.maximum(mx, _fold_rows(jnp.where(s <= bound, s, NEG), jnp.max))
        return _reduce_rows(lax.fori_loop(0, nch, body, jnp.full((64, qw), NEG, F32)), jnp.max)

    def next_below_and_multiplicity(cand):
        def body(c, carry):
            mx, eq = carry
            s = s_ref[chunk(c), :]
            mx = jnp.maximum(mx, _fold_rows(jnp.where(s < cand, s, NEG), jnp.max))
            eq = eq + _fold_rows(jnp.where(s == cand, 1.0, 0.0), jnp.sum)
            return mx, eq
        mx, eq = lax.fori_loop(0, nch, body, (jnp.full((64, qw), NEG, F32), jnp.zeros((64, qw), F32)))
        return _reduce_rows(mx, jnp.max), _reduce_rows(eq, jnp.sum)

    kf = float(k_sel)
    need = qpos >= k_sel

    def bisect(_, carry):
        lo, hi, above = carry
        mid = lo + (hi - lo) * 0.5
        cnt = count_gt(mid)
        below = cnt < kf
        return jnp.where(below, lo, mid), jnp.where(below, mid, hi), jnp.where(below, cnt, above)

    _, hi, above = lax.fori_loop(0, N_BISECT, bisect, (lo - 1.0, hi, jnp.zeros((1, qw), F32)))

    def walk_cond(carry):
        _, _, ge = carry
        return jnp.max(jnp.where(need & (ge < kf), 1.0, 0.0)) > 0.0

    def walk_body(carry):
        cand, gt, ge = carry
        active = ge < kf
        nxt, mult = next_below_and_multiplicity(cand)
        ge_new = gt + mult
        moved = active & (ge_new < kf)
        return jnp.where(moved, nxt, cand), jnp.where(moved, ge_new, gt), jnp.where(active, ge_new, ge)

    cand, gt, ge = lax.while_loop(walk_cond, walk_body, (largest_upto(hi), above, above))
    thr = jnp.where(need, cand, THR_ALL)
    want_eq = jnp.where(need, kf - gt, 0.0)

    any_tie = jnp.max(jnp.where(need & (ge > kf), 1.0, 0.0)) > 0.0

    def mark_with_ties(c, eq_seen):
        s = s_ref[chunk(c), :]
        eq = jnp.where(s == thr, 1.0, 0.0)
        rank = eq_seen + jnp.dot(tri_ref[...], eq.astype(BF16), preferred_element_type=F32)
        keep = jnp.where(s > thr, 1.0, jnp.where(rank <= want_eq, eq, 0.0))
        s_ref[chunk(c), :] = jnp.where(keep > 0.5, 0.0, NEG)
        return eq_seen + _reduce_rows(eq, jnp.sum)

    def mark_no_ties(c, carry):
        s_ref[chunk(c), :] = jnp.where(s_ref[chunk(c), :] >= thr, 0.0, NEG)
        return carry

    @pl.when(any_tie)
    def _():
        lax.fori_loop(0, nch, mark_with_ties, jnp.zeros((1, qw), F32))

    @pl.when(jnp.logical_not(any_tie))
    def _():
        lax.fori_loop(0, nch, mark_no_ties, 0)

    def masked_logits(c):
        sel = s_ref[chunk(c), :]
        bias = jnp.concatenate(
            [jnp.concatenate([jnp.concatenate([_bias_tile(tab_ref, h, qi * nqb + iq, c * nkb + jk)
                                               for iq in range(nqb)], axis=1) for jk in range(nkb)], axis=0) + sel
             for h in range(C_HEADS)], axis=1)
        return jnp.dot(k_ref[chunk(c), :], cq_all, preferred_element_type=F32) + bias

    def exact_step(c, carry):
        m, l, acc = carry
        lg = masked_logits(c)
        m_new = jnp.maximum(m, _reduce_rows(lg, jnp.max))
        alpha = jnp.exp2(m - m_new)
        p = jnp.exp2(lg - m_new)
        l = alpha * l + _reduce_rows(p, jnp.sum)
        acc = alpha * acc + jnp.dot(vt_ref[c], p.astype(BF16), preferred_element_type=F32)
        return m_new, l, acc

    def lagged_update(c, lg, m, l, acc, jump):
        p = jnp.exp2(lg - m)
        top = _reduce_rows(lg, jnp.max)
        l = l + _reduce_rows(p, jnp.sum)
        acc = acc + jnp.dot(vt_ref[c], p.astype(BF16), preferred_element_type=F32)
        m_new = jnp.maximum(m, top)
        alpha = jnp.exp2(m - m_new)
        return m_new, alpha * l, alpha * acc, jnp.maximum(jump, top - m)

    wide = C_HEADS * qw
    init = (jnp.full((1, wide), M_INIT, F32), jnp.zeros((1, wide), F32), jnp.zeros((HEAD_DIM, wide), F32))
    first = exact_step(0, init)

    def lagged_pair(i, carry):
        c = 1 + 2 * i
        lg_a, lg_b = masked_logits(c), masked_logits(c + 1)
        return lagged_update(c + 1, lg_b, *lagged_update(c, lg_a, *carry))

    state = lax.fori_loop(0, (nch - 1) // 2, lagged_pair, first + (jnp.zeros((1, wide), F32),))
    state = lax.cond((nch - 1) % 2 == 1,
                     lambda st: lagged_update(nch - 1, masked_logits(nch - 1), *st), lambda st: st, state)
    _, l, acc, jump = state
    l, acc = lax.cond(jnp.max(jump) > MAX_LAG,
                      lambda: lax.fori_loop(1, nch, exact_step, first)[1:],
                      lambda: (l, acc))
    o = acc / l
    ot = jnp.concatenate([o[:, h * qw:(h + 1) * qw] for h in range(C_HEADS)], axis=0)
    o_ref[...] = ot.T.astype(BF16)


def _dsa_attention(iqt, iwt, cqt, ck, cvt, tab, batch, t):
    tk = TOKEN_TILE
    qw = C_Q_TILE
    per = tk // qw
    nkt = t // tk
    k_sel = min(TOPK_MAX, t // 4)
    iqt = iqt.reshape(batch, nkt, IDX_HEADS * IDX_DIM, tk)
    iwt = iwt.reshape(batch, nkt, 16, tk)
    cqt = cqt.reshape(batch, nkt, C_OUT, tk)
    cvt = cvt.reshape(batch, nkt, HEAD_DIM, tk)
    ck = ck.reshape(batch, t, LANES)
    r = np.arange(tk)
    tri = jnp.asarray(r[:, None] >= r[None, :], BF16)
    qblock = lambda rows: pl.BlockSpec((None, None, rows, qw), lambda b, i: (b, i // per, 0, i % per))
    return pl.pallas_call(
        functools.partial(_dsa_kernel, k_sel=k_sel),
        out_shape=jax.ShapeDtypeStruct((batch, t, C_OUT), BF16),
        grid=(batch, t // qw),
        in_specs=[qblock(IDX_HEADS * IDX_DIM), qblock(16), qblock(C_OUT),
                  pl.BlockSpec((None, t, LANES), lambda b, i: (b, 0, 0)),
                  pl.BlockSpec((None, nkt, HEAD_DIM, tk), lambda b, i: (b, 0, 0, 0)),
                  _const_spec(tab.shape), _const_spec(tri.shape)],
        out_specs=pl.BlockSpec((None, qw, C_OUT), lambda b, i: (b, i, 0)),
        scratch_shapes=[pltpu.VMEM((t, qw), F32)],
        compiler_params=_cparams(("parallel", "arbitrary")),
        name="dsa_attention",
    )(iqt, iwt, cqt, ck, cvt, tab, tri).reshape(batch * t, C_OUT)


def _merge_kernel(x_ref, *refs):
    ng = len(DIL_GROUPS)
    a_refs = refs[:2 * ng]
    ob_ref, oc_ref, gate_ref, wa_ref, wb_ref, wc_ref, wo_ref, out_ref = refs[2 * ng:-1]
    shuffle_ref = refs[-1]
    tm, d = x_ref.shape

    def token_order(ref, slot):
        dil = ref.shape[0]
        if dil == 1:
            return ref[0]
        halves = range(A_OUT // LANES)
        for r in range(dil):
            for half in halves:
                shuffle_ref[slot, half, pl.ds(r, tm // dil, stride=dil), :] = ref[r, :, half * LANES:(half + 1) * LANES]
        return jnp.concatenate([shuffle_ref[slot, half] for half in halves], axis=1)

    outs = [token_order(a_refs[2 * g], 2 * g) for g in range(ng)]
    lses = [token_order(a_refs[2 * g + 1], 2 * g + 1) for g in range(ng)]
    top = functools.reduce(jnp.maximum, lses)
    es = [jnp.exp(lse - top) for lse in lses]
    num = sum(e * o for e, o in zip(es, outs))
    oa = (num / sum(es)).astype(BF16)
    y = gate_ref[:, 0:d].astype(F32) * jnp.dot(oa, wa_ref[...], preferred_element_type=F32)
    y = y + gate_ref[:, d:2 * d].astype(F32) * jnp.dot(ob_ref[...], wb_ref[...], preferred_element_type=F32)
    y = y + gate_ref[:, 2 * d:3 * d].astype(F32) * jnp.dot(oc_ref[...], wc_ref[...], preferred_element_type=F32)
    out_ref[...] = x_ref[...] + jnp.dot(y.astype(BF16), wo_ref[...], preferred_element_type=F32)


def _merge(x, a_parts, ob, oc, gates, wa, wb, wc, wo, t):
    n, d = x.shape
    tm = TOKEN_TILE
    per_batch = t // tm
    tok = lambda c: pl.BlockSpec((tm, c), lambda i: (i, 0))
    by_residue = lambda dil: pl.BlockSpec((None, dil, tm // dil, A_OUT),
                                          lambda i: (i // per_batch, 0, i % per_batch, 0))
    ws = [w.astype(BF16) for w in (wa, wb, wc, wo)]
    return pl.pallas_call(
        _merge_kernel,
        out_shape=jax.ShapeDtypeStruct((n, d), F32),
        grid=(n // tm,),
        in_specs=[tok(d)] + [by_residue(z.shape[1]) for z in a_parts] + [tok(B_OUT), tok(C_OUT), tok(3 * d)]
                 + [_const_spec(w.shape) for w in ws],
        out_specs=tok(d),
        scratch_shapes=[pltpu.VMEM((len(a_parts), A_OUT // LANES, tm, LANES), F32)],
        compiler_params=_cparams(("parallel",)),
        name="merge",
    )(x, *a_parts, ob, oc, gates, *ws)


def _token_mixer(x, batch, t, layer, mix_norm, w_in, qk_gain, diff_lambda, diff_out_norm,
                 w_branch_a, w_branch_b, w_branch_c, w_out, band_tabs, tab_b, tab_c):
    ng = len(DIL_GROUPS)
    outs = _project(x, mix_norm, w_in, qk_gain, batch, t)
    a_in, (bk, ck, gates, bqt, bvt, cqt, cvt, iqt, iwt) = outs[:3 * ng], outs[3 * ng:]
    a_parts = []
    for g, (_, dilation) in enumerate(DIL_GROUPS):
        a_parts += _dilated_group(a_in[g], a_in[ng + g], a_in[2 * ng + g], band_tabs[g], dilation)
    lam_init = 0.8 - 0.6 * np.exp(-0.3 * layer)
    lam_rows = jnp.concatenate([diff_lambda.astype(F32), jnp.full((4, HEAD_DIM), lam_init, F32)], axis=0)
    gn = jnp.broadcast_to((diff_out_norm.astype(F32) * (1.0 - lam_init))[:, None], (B_V_DIM, B_Q_TILE))
    ob = _diff_attention(bqt, bk, bvt, tab_b, lam_rows, gn, batch, t)
    oc = _dsa_attention(iqt, iwt, cqt, ck, cvt, tab_c, batch, t)
    return _merge(x, a_parts, ob, oc, gates, w_branch_a, w_branch_b, w_branch_c, w_out, t)


def kernel(x, rel_bias, ffn1_norm, ffn1_w_gate, ffn1_w_up, ffn1_w_down, mix_norm, w_in, qk_gain,
           diff_lambda, diff_out_norm, w_branch_a, w_branch_b, w_branch_c, w_out,
           ffn2_norm, ffn2_w_gate, ffn2_w_up, ffn2_w_down):
    batch, t, d = x.shape
    depth = w_in.shape[0]
    assert t % (DIL_GROUPS[-1][1] * LANES) == 0 and t % TOKEN_TILE == 0
    band_tabs = [_band_tables(rel_bias[:, g * A_GROUP_HEADS:(g + 1) * A_GROUP_HEADS], dil)
                 for g, (_, dil) in enumerate(DIL_GROUPS)]
    tab_b = _toeplitz_tables(rel_bias[:, A_HEADS:A_HEADS + B_HEADS] * LOG2E)
    tab_c = _toeplitz_tables(rel_bias[:, A_HEADS + B_HEADS:] * LOG2E)
    h = x.reshape(batch * t, d).astype(F32)
    for i in range(depth):
        h = _ffn(h, ffn1_norm[i], ffn1_w_gate[i], ffn1_w_up[i], ffn1_w_down[i])
        h = _token_mixer(h, batch, t, i, mix_norm[i], w_in[i], qk_gain[i], diff_lambda[i], diff_out_norm[i],
                         w_branch_a[i], w_branch_b[i], w_branch_c[i], w_out[i], band_tabs, tab_b, tab_c)
        h = _ffn(h, ffn2_norm[i], ffn2_w_gate[i], ffn2_w_up[i], ffn2_w_down[i])
    return h.reshape(batch, t, d).astype(x.dtype)
```

```python
import functools

import numpy as np
import jax
import jax.numpy as jnp
from jax import lax
from jax.experimental import pallas as pl
from jax.experimental.pallas import tpu as pltpu

F32 = jnp.float32
BF16 = jnp.bfloat16

HEAD_DIM = 64
DIL_GROUPS = ((128, 1), (512, 4), (2048, 16))
A_GROUP_HEADS = 4
A_HEADS = A_GROUP_HEADS * len(DIL_GROUPS)
A_OUT = A_GROUP_HEADS * HEAD_DIM
B_HEADS = 4
B_V_DIM = 2 * HEAD_DIM
B_OUT = B_HEADS * B_V_DIM
C_HEADS = 4
C_OUT = C_HEADS * HEAD_DIM
IDX_HEADS = 8
IDX_DIM = 64
TOPK_MAX = 256
NUM_BUCKETS = 32
MAX_DISTANCE = 2048
RMS_EPS = 1e-6
LOG2E = 1.4426950408889634

LANES = 128
TOKEN_TILE = 512
FFN_TILE = 1024
A_BLOCKS_PER_STEP = 4
B_Q_TILE = 512
C_Q_TILE = 256
MAX_LAG = 60.0
FFN_CHUNK = 256
VMEM_LIMIT = 58 * 1024 * 1024

NEG = -1e30
M_INIT = -1e29
BIG = 1e30
THR_ALL = -1e29
N_BISECT = 14


def _cparams(sem):
    return pltpu.CompilerParams(dimension_semantics=sem, vmem_limit_bytes=VMEM_LIMIT)


def _const_spec(shape):
    nd = len(shape)
    return pl.BlockSpec(shape, lambda *_: (0,) * nd, pipeline_mode=pl.Buffered(1))


def _rel_bucket_np(dist):
    n = np.maximum(dist, 0)
    max_exact = NUM_BUCKETS // 2
    nf = np.maximum(n, 1).astype(np.float64)
    large = max_exact + (np.log(nf / max_exact) / np.log(MAX_DISTANCE / max_exact)
                         * (NUM_BUCKETS - max_exact)).astype(np.int64)
    large = np.minimum(large, NUM_BUCKETS - 1)
    return np.where(n < max_exact, n, large)


def _far_delta():
    d = 1
    while not np.all(_rel_bucket_np(np.arange(d * LANES - LANES + 1, d * LANES + LANES)) == NUM_BUCKETS - 1):
        d += 1
    return d


FAR = _far_delta()
MASKED = FAR + 1


def _toeplitz(w, n_rows, n_cols):
    period = n_rows + n_cols
    w = jnp.pad(w, ((0, 0), (0, period - w.shape[1])))
    m = jnp.tile(w, (1, n_rows))[:, :n_rows * (period - 1)].reshape(-1, n_rows, period - 1)
    return m[:, :, n_rows - 1:n_rows - 1 + n_cols]


def _bias_by_distance(bias_heads, dist, valid):
    vals = jnp.take(bias_heads.astype(F32), jnp.asarray(_rel_bucket_np(dist), jnp.int32), axis=0).T
    return jnp.where(jnp.asarray(valid)[None], vals, NEG)


def _toeplitz_tables(bias_heads):
    n_cols = (FAR + 1) * LANES
    dist = np.arange(LANES - 1 + n_cols) - (LANES - 1)
    tiles = _toeplitz(_bias_by_distance(bias_heads, dist, dist >= 0), LANES, n_cols)
    tiles = tiles.reshape(-1, LANES, FAR + 1, LANES).transpose(0, 2, 1, 3)
    masked = jnp.full((tiles.shape[0], 1, LANES, LANES), NEG, F32)
    return jnp.concatenate([tiles, masked], axis=1)


def _band_tables(bias_heads, dilation):
    wn = LANES
    sub = np.arange(3 * wn - 1) - (wn - 1)
    w = _bias_by_distance(bias_heads, sub * dilation, (sub >= 0) & (sub <= wn))
    later = jnp.flip(_toeplitz(w, wn, 2 * wn), axis=(1, 2))
    first = jnp.where(jnp.asarray(np.arange(2 * wn) >= wn)[None, None], later, NEG)
    return jnp.stack([first, later])


def _ffn_kernel(x_ref, g_ref, wg_ref, wu_ref, wd_ref, o_ref, acc_ref):
    x = x_ref[...]
    ms = jnp.mean(x * x, axis=-1, keepdims=True)
    h = (x * lax.rsqrt(ms + RMS_EPS) * g_ref[...]).astype(BF16)
    acc_ref[...] = jnp.zeros_like(acc_ref)

    def body(c, carry):
        cols = pl.ds(pl.multiple_of(c * FFN_CHUNK, FFN_CHUNK), FFN_CHUNK)
        g = jnp.dot(h, wg_ref[:, cols], preferred_element_type=F32)
        u = jnp.dot(h, wu_ref[:, cols], preferred_element_type=F32)
        a = (g * jax.nn.sigmoid(g) * u).astype(BF16)
        acc_ref[...] += jnp.dot(a, wd_ref[cols, :], preferred_element_type=F32)
        return carry

    lax.fori_loop(0, wg_ref.shape[1] // FFN_CHUNK, body, 0)
    o_ref[...] = x + 0.5 * acc_ref[...]


def _ffn(x, gain, w_gate, w_up, w_down):
    n, d = x.shape
    f = w_gate.shape[1]
    tm = FFN_TILE
    return pl.pallas_call(
        _ffn_kernel,
        out_shape=jax.ShapeDtypeStruct((n, d), F32),
        grid=(n // tm,),
        in_specs=[pl.BlockSpec((tm, d), lambda i: (i, 0)),
                  _const_spec((1, d)), _const_spec((d, f)), _const_spec((d, f)), _const_spec((f, d))],
        out_specs=pl.BlockSpec((tm, d), lambda i: (i, 0)),
        scratch_shapes=[pltpu.VMEM((tm, d), F32)],
        compiler_params=_cparams(("parallel",)),
        name="ffn",
    )(x, gain.reshape(1, d).astype(F32), w_gate.astype(BF16), w_up.astype(BF16), w_down.astype(BF16))


S_AQ, S_AK, S_AV = 0, 768, 1536
S_BK, S_CK, S_GATE, S_END = 2304, 2816, 2944, 6016
T_BQ, T_BV, T_CQ, T_CV, T_IQ, T_IW, T_END = 0, 512, 1024, 1280, 1344, 1856, 1872


def _proj_kernel(x_ref, g_ref, ws_ref, wt_ref, bd_ref, gs_ref, gt_ref, *refs):
    ng = len(DIL_GROUPS)
    a_refs = refs[:3 * ng]
    bk_ref, ck_ref, gate_ref, bqt_ref, bvt_ref, cqt_ref, cvt_ref, iqt_ref, iwt_ref = refs[3 * ng:-1]
    shuffle_ref = refs[-1]
    tm = x_ref.shape[0]

    def store_by_residue(y, which):
        for g, (_, dil) in enumerate(DIL_GROUPS):
            out = a_refs[which * ng + g]
            part = y[:, g * A_OUT:(g + 1) * A_OUT]
            if dil == 1:
                out[0] = part.astype(BF16)
            else:
                for half in range(A_OUT // LANES):
                    shuffle_ref[half] = part[:, half * LANES:(half + 1) * LANES]
                for r in range(dil):
                    out[r] = jnp.concatenate(
                        [shuffle_ref[half, pl.ds(r, tm // dil, stride=dil), :] for half in range(A_OUT // LANES)],
                        axis=1).astype(BF16)

    x = x_ref[...]
    ms = jnp.mean(x * x, axis=-1, keepdims=True)
    h = (x * lax.rsqrt(ms + RMS_EPS) * g_ref[...]).astype(BF16)
    bd = bd_ref[...]

    def dot_s(c0, c1):
        return jnp.dot(h, ws_ref[:, c0:c1], preferred_element_type=F32)

    def head_inv_rms(y):
        width = bd.shape[0]
        outs = []
        for c0 in range(0, y.shape[1], width):
            sq = y[:, c0:c0 + width]
            n = sq.shape[1]
            msq = jnp.dot((sq * sq).astype(BF16), bd[:n, :n], preferred_element_type=F32)
            outs.append(lax.rsqrt(msq + RMS_EPS))
        return outs[0] if len(outs) == 1 else jnp.concatenate(outs, axis=1)

    y = dot_s(S_AQ, S_AK)
    store_by_residue(y * head_inv_rms(y) * gs_ref[:, 0:768], 0)
    y = dot_s(S_AK, S_AV)
    store_by_residue(y * head_inv_rms(y) * gs_ref[:, 768:1536], 1)
    store_by_residue(dot_s(S_AV, S_BK), 2)
    y = dot_s(S_BK, S_CK)
    bk_ref[...] = (y * head_inv_rms(y) * gs_ref[:, 1536:2048]).astype(BF16)
    y = dot_s(S_CK, S_GATE)
    lane = lax.broadcasted_iota(jnp.int32, y.shape, 1)
    inv = jnp.where(lane < HEAD_DIM, head_inv_rms(y), 1.0)
    ck_ref[...] = (y * inv * gs_ref[:, 2048:2176]).astype(BF16)
    for c in range(3):
        y = dot_s(S_GATE + c * 1024, S_GATE + (c + 1) * 1024)
        gate_ref[:, c * 1024:(c + 1) * 1024] = jax.nn.sigmoid(y).astype(BF16)

    def dot_t(r0, r1):
        return lax.dot_general(wt_ref[r0:r1, :], h, (((1,), (1,)), ((), ())),
                               preferred_element_type=F32)

    def norm_t(y, gain):
        r = y.shape[0] // HEAD_DIM
        y3 = y.reshape(r, HEAD_DIM, tm)
        msq = jnp.mean(y3 * y3, axis=1, keepdims=True)
        return (y3 * lax.rsqrt(msq + RMS_EPS)).reshape(r * HEAD_DIM, tm) * gain

    bqt_ref[...] = norm_t(dot_t(T_BQ, T_BV), gt_ref[0:512, :]).astype(BF16)
    bvt_ref[...] = dot_t(T_BV, T_CQ).astype(BF16)
    cqt_ref[...] = norm_t(dot_t(T_CQ, T_CV), gt_ref[512:768, :]).astype(BF16)
    cvt_ref[...] = dot_t(T_CV, T_IQ).astype(BF16)
    iqt_ref[...] = dot_t(T_IQ, T_IW).astype(BF16)
    iwt_ref[...] = dot_t(T_IW, T_END) * (IDX_HEADS ** -0.5 * IDX_DIM ** -0.5)


def _proj_weights(w_in, qk_gain, tm):
    d = w_in.shape[0]
    o = 0
    a_qkv = w_in[:, o:o + 3 * A_HEADS * HEAD_DIM].reshape(d, 3, A_HEADS * HEAD_DIM)
    o += 3 * A_HEADS * HEAD_DIM
    b_qk = w_in[:, o:o + 4 * B_HEADS * HEAD_DIM].reshape(d, 4, B_HEADS, HEAD_DIM)
    o += 4 * B_HEADS * HEAD_DIM
    b_v = w_in[:, o:o + B_OUT]
    o += B_OUT
    c_q = w_in[:, o:o + C_OUT]
    c_k = w_in[:, o + C_OUT:o + C_OUT + HEAD_DIM]
    c_v = w_in[:, o + C_OUT + HEAD_DIM:o + C_OUT + 2 * HEAD_DIM]
    o += C_OUT + 2 * HEAD_DIM
    i_q = w_in[:, o:o + IDX_HEADS * IDX_DIM]
    i_k = w_in[:, o + IDX_HEADS * IDX_DIM:o + IDX_HEADS * IDX_DIM + IDX_DIM]
    i_w = w_in[:, o + IDX_HEADS * IDX_DIM + IDX_DIM:o + IDX_HEADS * IDX_DIM + IDX_DIM + IDX_HEADS]
    o += IDX_HEADS * IDX_DIM + IDX_DIM + IDX_HEADS
    gates = w_in[:, o:]
    b_k = jnp.stack([b_qk[:, 2], b_qk[:, 3]], axis=2).reshape(d, 2 * B_HEADS * HEAD_DIM)
    b_q = jnp.stack([b_qk[:, 0], b_qk[:, 1]], axis=2).reshape(d, 2 * B_HEADS * HEAD_DIM)
    w_s = jnp.concatenate([a_qkv[:, 0], a_qkv[:, 1], a_qkv[:, 2], b_k, c_k, i_k, gates], axis=1)
    w_t = jnp.concatenate([b_q, b_v, c_q, c_v, i_q, i_w, jnp.zeros((d, 8), w_in.dtype)], axis=1).T
    assert w_s.shape[1] == S_END and w_t.shape[0] == T_END
    scale = HEAD_DIM ** -0.5
    g = qk_gain.astype(F32)
    gs = jnp.concatenate([jnp.tile(g[0, 0] * scale, A_HEADS), jnp.tile(g[0, 1], A_HEADS),
                          jnp.tile(g[1, 1], 2 * B_HEADS), g[2, 1], jnp.ones((IDX_DIM,), F32)])[None]
    gt = jnp.concatenate([jnp.tile(g[1, 0] * (scale * LOG2E), 2 * B_HEADS),
                          jnp.tile(g[2, 0] * (scale * LOG2E), C_HEADS)])
    gt = jnp.broadcast_to(gt[:, None], (gt.shape[0], tm))
    return w_s.astype(BF16), w_t.astype(BF16), gs, gt


MXU_TILE = 256


def _head_block_diag():
    r = np.arange(MXU_TILE)
    return jnp.asarray((r[:, None] // HEAD_DIM == r[None, :] // HEAD_DIM) / HEAD_DIM, BF16)


def _project(x, gain, w_in, qk_gain, batch, t):
    n, d = x.shape
    tm = TOKEN_TILE
    nt = n // tm
    per_batch = t // tm
    w_s, w_t, gs, gt = _proj_weights(w_in, qk_gain, tm)
    tok = lambda c: pl.BlockSpec((tm, c), lambda i: (i, 0))
    feat = lambda r: pl.BlockSpec((None, r, tm), lambda i: (i, 0, 0))
    a_shapes, a_specs = [], []
    for _ in range(3):
        for _, dil in DIL_GROUPS:
            a_shapes.append(jax.ShapeDtypeStruct((batch, dil, t // dil, A_OUT), BF16))
            a_specs.append(pl.BlockSpec((None, dil, tm // dil, A_OUT),
                                        lambda i: (i // per_batch, 0, i % per_batch, 0)))
    out_shape = a_shapes + [
        jax.ShapeDtypeStruct((n, 512), BF16), jax.ShapeDtypeStruct((n, 128), BF16),
        jax.ShapeDtypeStruct((n, 3072), BF16),
        jax.ShapeDtypeStruct((nt, 512, tm), BF16), jax.ShapeDtypeStruct((nt, 512, tm), BF16),
        jax.ShapeDtypeStruct((nt, 256, tm), BF16), jax.ShapeDtypeStruct((nt, 64, tm), BF16),
        jax.ShapeDtypeStruct((nt, 512, tm), BF16), jax.ShapeDtypeStruct((nt, 16, tm), F32)]
    out_specs = a_specs + [tok(512), tok(128), tok(3072),
                           feat(512), feat(512), feat(256), feat(64), feat(512), feat(16)]
    return pl.pallas_call(
        _proj_kernel,
        out_shape=out_shape,
        grid=(nt,),
        in_specs=[tok(d), _const_spec((1, d)), _const_spec(w_s.shape), _const_spec(w_t.shape),
                  _const_spec((MXU_TILE, MXU_TILE)), _const_spec(gs.shape), _const_spec(gt.shape)],
        out_specs=out_specs,
        scratch_shapes=[pltpu.VMEM((A_OUT // LANES, tm, LANES), F32)],
        compiler_params=_cparams(("parallel",)),
        name="proj",
    )(x, gain.reshape(1, d).astype(F32), w_s, w_t, _head_block_diag(), gs, gt)


def _dil_kernel(q_ref, kp_ref, kc_ref, vp_ref, vc_ref, bias_ref, o_ref, lse_ref):
    nq = q_ref.shape[0] // LANES
    qi = pl.program_id(2)
    lane = lax.broadcasted_iota(jnp.int32, (LANES, A_OUT), 1) // HEAD_DIM
    mine = [lane == h for h in range(A_GROUP_HEADS)]
    blocks = [slice(jb * LANES, (jb + 1) * LANES) for jb in range(nq)]

    def band(prev_ref, cur_ref, jb):
        if jb == 0:
            return jnp.concatenate([prev_ref[...], cur_ref[blocks[0], :]], axis=0)
        return cur_ref[(jb - 1) * LANES:(jb + 1) * LANES, :]

    logits = []
    for jb in range(nq):
        q = q_ref[blocks[jb], :]
        q4 = jnp.concatenate([jnp.where(mine[h], q, jnp.zeros_like(q)) for h in range(A_GROUP_HEADS)], axis=0)
        s = lax.dot_general(q4, band(kp_ref, kc_ref, jb), (((1,), (1,)), ((), ())), preferred_element_type=F32)
        bias = bias_ref[jnp.minimum(qi, 1) if jb == 0 else 1]
        logits.append(s + bias.reshape(A_GROUP_HEADS * LANES, 2 * LANES))
    probs, stats = [], []
    for s in logits:
        m = jnp.max(s, axis=1, keepdims=True)
        p = jnp.exp(s - m)
        ssum = jnp.sum(p, axis=1, keepdims=True)
        probs.append(p.astype(BF16))
        stats.append((1.0 / ssum, m + jnp.log(ssum)))
    for jb in range(nq):
        pv = jnp.dot(probs[jb], band(vp_ref, vc_ref, jb), preferred_element_type=F32)
        inv, lse4 = stats[jb]
        o = jnp.zeros((LANES, A_OUT), F32)
        lse = jnp.zeros((LANES, A_OUT), F32)
        for h in range(A_GROUP_HEADS):
            head = slice(h * LANES, (h + 1) * LANES)
            o = jnp.where(mine[h], pv[head] * inv[head], o)
            lse = jnp.where(mine[h], lse4[head], lse)
        o_ref[blocks[jb], :] = o.astype(BF16)
        lse_ref[blocks[jb], :] = lse


def _dilated_group(aq, ak, av, bias, dilation):
    batch, _, n, _ = aq.shape
    nblk = n // LANES
    nq = min(nblk, A_BLOCKS_PER_STEP)
    qt = nq * LANES
    cur = pl.BlockSpec((None, None, qt, A_OUT), lambda b, r, i: (b, r, i, 0))
    prev = pl.BlockSpec((None, None, LANES, A_OUT), lambda b, r, i: (b, r, jnp.maximum(i * nq - 1, 0), 0))
    shp = (batch, dilation, n, A_OUT)
    return pl.pallas_call(
        _dil_kernel,
        out_shape=[jax.ShapeDtypeStruct(shp, BF16), jax.ShapeDtypeStruct(shp, F32)],
        grid=(batch, dilation, nblk // nq),
        in_specs=[cur, prev, cur, prev, cur, _const_spec(bias.shape)],
        out_specs=[cur, cur],
        compiler_params=_cparams(("parallel", "parallel", "parallel")),
        name=f"dilated_d{dilation}",
    )(aq, ak, ak, av, av, bias)


def _fold_rows(x, op):
    r, c = x.shape
    return op(x.reshape(r // 64, 64, c), axis=0) if r > 64 else x


def _reduce_rows(x, op):
    x = _fold_rows(x, op)
    x = op(x.reshape(8, 8, x.shape[1]), axis=0)
    return op(x, axis=0, keepdims=True)


def _bias_tile(tab_ref, head, qblk, kblk):
    delta = qblk - kblk
    idx = jnp.where(delta < 0, MASKED, jnp.minimum(delta, FAR))
    if head is None:
        return tab_ref[idx]
    return tab_ref[head, idx]


def _diff_kernel(qt_ref, k_ref, vt_ref, tab_ref, lam_ref, gn_ref, o_ref):
    tq = qt_ref.shape[1]
    tk = TOKEN_TILE
    qi = pl.program_id(2)
    qt = qt_ref[...]
    row = lax.broadcasted_iota(jnp.int32, qt.shape, 0)
    q12 = jnp.concatenate([jnp.where(row < HEAD_DIM, qt, jnp.zeros_like(qt)),
                           jnp.where(row >= HEAD_DIM, qt, jnp.zeros_like(qt))], axis=1)
    nqb = tq // LANES
    nkb = tk // LANES

    def logits(c):
        kc = k_ref[pl.ds(pl.multiple_of(c * tk, tk), tk), :]
        bias = jnp.concatenate(
            [jnp.concatenate([_bias_tile(tab_ref, None, qi * nqb + iq, c * nkb + jk)
                              for iq in range(nqb)] * 2, axis=1) for jk in range(nkb)], axis=0)
        return jnp.dot(kc, q12, preferred_element_type=F32) + bias

    def exact_step(c, carry):
        m, l, acc = carry
        s = logits(c)
        m_new = jnp.maximum(m, _reduce_rows(s, jnp.max))
        alpha = jnp.exp2(m - m_new)
        p = jnp.exp2(s - m_new)
        l = alpha * l + _reduce_rows(p, jnp.sum)
        acc = alpha * acc + jnp.dot(vt_ref[c], p.astype(BF16), preferred_element_type=F32)
        return m_new, l, acc

    def lagged_update(c, s, m, l, acc, jump):
        p = jnp.exp2(s - m)
        top =_reduce_rows(s, jnp.max)
        l = l + _reduce_rows(p, jnp.sum)
        acc = acc + jnp.dot(vt_ref[c], p.astype(BF16), preferred_element_type=F32)
        m_new = jnp.maximum(m, top)
        alpha = jnp.exp2(m - m_new)
        return m_new, alpha * l, alpha * acc, jnp.maximum(jump, top - m)

    nch = ((qi + 1) * tq + tk - 1) // tk
    zero = jnp.zeros((1, 2 * tq), F32)
    acc0 = jnp.zeros((B_V_DIM, 2 * tq), F32)
    def opening(count):
        ss = [logits(c) for c in range(count)]
        state = (jnp.max(ss[0][0:8], axis=0, keepdims=True), zero, acc0, zero)
        for c in range(count):
            state = lagged_update(c, ss[c], *state)
        return state

    def lagged_pair(c, carry):
        s_a, s_b = logits(c), logits(c + 1)
        return lagged_update(c + 1, s_b, *lagged_update(c, s_a, *carry))

    odd = nch % 2
    state = lax.cond(odd == 1, lambda: opening(1), lambda: opening(2))
    _, l, acc, jump = lax.fori_loop(0, (nch - 2 + odd) // 2,
                                    lambda i, carry: lagged_pair(2 - odd + 2 * i, carry), state)
    l, acc = lax.cond(jnp.max(jump) > MAX_LAG,
                      lambda: lax.fori_loop(0, nch, exact_step, (jnp.full((1, 2 * tq), M_INIT, F32), zero, acc0))[1:],
                      lambda: (l, acc))
    a1, a2 = acc[:, :tq], acc[:, tq:]
    l1, l2 = l[:, :tq], l[:, tq:]

    lv = lam_ref[...]
    lam = (jnp.exp(jnp.sum(lv[0:1] * lv[1:2], axis=1, keepdims=True))
           - jnp.exp(jnp.sum(lv[2:3] * lv[3:4], axis=1, keepdims=True)) + lv[4:5, 0:1])
    o = a1 / l1 - lam * (a2 / l2)
    ms = jnp.mean(o * o, axis=0, keepdims=True)
    o = o * lax.rsqrt(ms + RMS_EPS) * gn_ref[...]
    o_ref[...] = o.T.astype(BF16)


def _diff_attention(bqt, bk, bvt, tab, lam_rows, gn, batch, t):
    tq = B_Q_TILE
    tk = TOKEN_TILE
    per = tk // tq
    nkt = t // tk
    bqt = bqt.reshape(batch, nkt, B_HEADS * LANES, tk)
    bvt = bvt.reshape(batch, nkt, B_OUT, tk)
    bk = bk.reshape(batch, t, B_HEADS * LANES)
    return pl.pallas_call(
        _diff_kernel,
        out_shape=jax.ShapeDtypeStruct((batch, t, B_OUT), BF16),
        grid=(batch, B_HEADS, t // tq),
        in_specs=[pl.BlockSpec((None, None, LANES, tq), lambda b, h, i: (b, i // per, h, i % per)),
                  pl.BlockSpec((None, t, LANES), lambda b, h, i: (b, 0, h)),
                  pl.BlockSpec((None, nkt, B_V_DIM, tk), lambda b, h, i: (b, 0, h, 0)),
                  pl.BlockSpec((None, FAR + 2, LANES, LANES), lambda b, h, i: (h, 0, 0, 0)),
                  _const_spec(lam_rows.shape), _const_spec(gn.shape)],
        out_specs=pl.BlockSpec((None, tq, B_V_DIM), lambda b, h, i: (b, i, h)),
        compiler_params=_cparams(("parallel", "parallel", "arbitrary")),
        name="diff_attention",
    )(bqt, bk, bvt, tab, lam_rows, gn).reshape(batch * t, B_OUT)


def _dsa_kernel(iqt_ref, iwt_ref, cqt_ref, k_ref, vt_ref, tab_ref, tri_ref, o_ref, s_ref, *, k_sel):
    tk = TOKEN_TILE
    nkb = tk // LANES
    qw = o_ref.shape[0]
    nqb = qw // LANES
    qi = pl.program_id(1)
    nch = ((qi + 1) * qw + tk - 1) // tk
    qpos = qi * qw + lax.broadcasted_iota(jnp.int32, (1, qw), 1)
    zeros = jnp.zeros((HEAD_DIM, qw), BF16)
    iq = iqt_ref[...]
    w = iwt_ref[...]
    iq_all = jnp.concatenate([jnp.concatenate([zeros, iq[h * IDX_DIM:(h + 1) * IDX_DIM]], axis=0)
                              for h in range(IDX_HEADS)], axis=1)
    cq = cqt_ref[...]
    cq_all = jnp.concatenate([jnp.concatenate([cq[h * HEAD_DIM:(h + 1) * HEAD_DIM], zeros], axis=0)
                              for h in range(C_HEADS)], axis=1)

    def chunk(c):
        return pl.ds(pl.multiple_of(c * tk, tk), tk)

    def raw_scores(c):
        return jnp.dot(k_ref[chunk(c), :], iq_all, preferred_element_type=F32)

    def score_chunk(c, raw, mn, mx, last):
        acc = w[0:1, :] * jnp.maximum(raw[:, 0:qw], 0.0)
        for h in range(1, IDX_HEADS):
            acc = acc + w[h:h + 1, :] * jnp.maximum(raw[:, h * qw:(h + 1) * qw], 0.0)
        if last:
            kpos = c * tk + lax.broadcasted_iota(jnp.int32, (tk, qw), 0)
            causal = kpos <= qpos
            s_ref[chunk(c), :] = jnp.where(causal, acc, NEG)
            mn = jnp.minimum(mn, _fold_rows(jnp.where(causal, acc, BIG), jnp.min))
            mx = jnp.maximum(mx, _fold_rows(jnp.where(causal, acc, NEG), jnp.max))
        else:
            s_ref[chunk(c), :] = acc
            mn = jnp.minimum(mn, _fold_rows(acc, jnp.min))
            mx = jnp.maximum(mx, _fold_rows(acc, jnp.max))
        return mn, mx

    def score_pair(c, carry, last):
        raw_a, raw_b = raw_scores(c), raw_scores(c + 1)
        return score_chunk(c + 1, raw_b, *score_chunk(c, raw_a, *carry, last=False), last=last)

    carry = lax.fori_loop(0, (nch - 1) // 2, lambda i, carry: score_pair(2 * i, carry, last=False),
                          (jnp.full((64, qw), BIG, F32), jnp.full((64, qw), NEG, F32)))
    mn, mx = lax.cond((nch - 1) % 2 == 1,
                      lambda st: score_pair(nch - 2, st, last=True),
                      lambda st: score_chunk(nch - 1, raw_scores(nch - 1), *st, last=True), carry)
    lo, hi = _reduce_rows(mn, jnp.min), _reduce_rows(mx, jnp.max)

    def count_gt(thr):
        def body(c, gt):
            return gt + _fold_rows(jnp.where(s_ref[chunk(c), :] > thr, 1.0, 0.0), jnp.sum)
        return _reduce_rows(lax.fori_loop(0, nch, body, jnp.zeros((64, qw), F32)), jnp.sum)

    def largest_upto(bound):
        def body(c, mx):
            s = s_ref[chunk(c), :]
            return jnp.maximum(mx, _fold_rows(jnp.where(s <= bound, s, NEG), jnp.max))
        return _reduce_rows(lax.fori_loop(0, nch, body, jnp.full((64, qw), NEG, F32)), jnp.max)

    def next_below_and_multiplicity(cand):
        def body(c, carry):
            mx, eq = carry
            s = s_ref[chunk(c), :]
            mx = jnp.maximum(mx, _fold_rows(jnp.where(s < cand, s, NEG), jnp.max))
            eq = eq + _fold_rows(jnp.where(s == cand, 1.0, 0.0), jnp.sum)
            return mx, eq
        mx, eq = lax.fori_loop(0, nch, body, (jnp.full((64, qw), NEG, F32), jnp.zeros((64, qw), F32)))
        return _reduce_rows(mx, jnp.max), _reduce_rows(eq, jnp.sum)

    kf = float(k_sel)
    need = qpos >= k_sel

    def bisect(_, carry):
        lo, hi, above = carry
        mid = lo + (hi - lo) * 0.5
        cnt = count_gt(mid)
        below = cnt < kf
        return jnp.where(below, lo, mid), jnp.where(below, mid, hi), jnp.where(below, cnt, above)

    _, hi, above = lax.fori_loop(0, N_BISECT, bisect, (lo - 1.0, hi, jnp.zeros((1, qw), F32)))

    def walk_cond(carry):
        _, _, ge = carry
        return jnp.max(jnp.where(need & (ge < kf), 1.0, 0.0)) > 0.0

    def walk_body(carry):
        cand, gt, ge = carry
        active = ge < kf
        nxt, mult = next_below_and_multiplicity(cand)
        ge_new = gt + mult
        moved = active & (ge_new < kf)
        return jnp.where(moved, nxt, cand), jnp.where(moved, ge_new, gt), jnp.where(active, ge_new, ge)

    cand, gt, ge = lax.while_loop(walk_cond, walk_body, (largest_upto(hi), above, above))
    thr = jnp.where(need, cand, THR_ALL)
    want_eq = jnp.where(need, kf - gt, 0.0)

    any_tie = jnp.max(jnp.where(need & (ge > kf), 1.0, 0.0)) > 0.0

    def mark_with_ties(c, eq_seen):
        s = s_ref[chunk(c), :]
        eq = jnp.where(s == thr, 1.0, 0.0)
        rank = eq_seen + jnp.dot(tri_ref[...], eq.astype(BF16), preferred_element_type=F32)
        keep = jnp.where(s > thr, 1.0, jnp.where(rank <= want_eq, eq, 0.0))
        s_ref[chunk(c), :] = jnp.where(keep > 0.5, 0.0, NEG)
        return eq_seen + _reduce_rows(eq, jnp.sum)

    def mark_no_ties(c, carry):
        s_ref[chunk(c), :] = jnp.where(s_ref[chunk(c), :] >= thr, 0.0, NEG)
        return carry

    @pl.when(any_tie)
    def _():
        lax.fori_loop(0, nch, mark_with_ties, jnp.zeros((1, qw), F32))

    @pl.when(jnp.logical_not(any_tie))
    def _():
        lax.fori_loop(0, nch, mark_no_ties, 0)

    def masked_logits(c):
        sel = s_ref[chunk(c), :]
        bias = jnp.concatenate(
            [jnp.concatenate([jnp.concatenate([_bias_tile(tab_ref, h, qi * nqb + iq, c * nkb + jk)
                                               for iq in range(nqb)], axis=1) for jk in range(nkb)], axis=0) + sel
             for h in range(C_HEADS)], axis=1)
        return jnp.dot(k_ref[chunk(c), :], cq_all, preferred_element_type=F32) + bias

    def exact_update(c, lg, m, l, acc):
        m_new = jnp.maximum(m, _reduce_rows(lg, jnp.max))
        alpha = jnp.exp2(m - m_new)
        p = jnp.exp2(lg - m_new)
        l = alpha * l + _reduce_rows(p, jnp.sum)
        acc = alpha * acc + jnp.dot(vt_ref[c], p.astype(BF16), preferred_element_type=F32)
        return m_new, l, acc

    def lagged_update(c, lg, m, l, acc, jump):
        p = jnp.exp2(lg - m)
        top = _reduce_rows(lg, jnp.max)
        l = l + _reduce_rows(p, jnp.sum)
        acc = acc + jnp.dot(vt_ref[c], p.astype(BF16), preferred_element_type=F32)
        m_new = jnp.maximum(m, top)
        alpha = jnp.exp2(m - m_new)
        return m_new, alpha * l, alpha * acc, jnp.maximum(jump, top - m)

    wide = C_HEADS * qw
    init = (jnp.full((1, wide), M_INIT, F32), jnp.zeros((1, wide), F32), jnp.zeros((HEAD_DIM, wide), F32))
    def opening(count):
        lgs = [masked_logits(c) for c in range(count)]
        state = exact_update(0, lgs[0], *init) + (jnp.zeros((1, wide), F32),)
        return lagged_update(1, lgs[1], *state) if count == 2 else state

    def lagged_pair(c, carry):
        lg_a, lg_b = masked_logits(c), masked_logits(c + 1)
        return lagged_update(c + 1, lg_b, *lagged_update(c, lg_a, *carry))

    odd = nch % 2
    state = lax.cond(odd == 1, lambda: opening(1), lambda: opening(2))
    _, l, acc, jump = lax.fori_loop(0, (nch - 2 + odd) // 2,
                                    lambda i, carry: lagged_pair(2 - odd + 2 * i, carry), state)
    l, acc = lax.cond(jnp.max(jump) > MAX_LAG,
                      lambda: lax.fori_loop(0, nch, lambda c, st: exact_update(c, masked_logits(c), *st), init)[1:],
                      lambda: (l, acc))
    o = acc / l
    ot = jnp.concatenate([o[:, h * qw:(h + 1) * qw] for h in range(C_HEADS)], axis=0)
    o_ref[...] = ot.T.astype(BF16)


def _dsa_attention(iqt, iwt, cqt, ck, cvt, tab, batch, t):
    tk = TOKEN_TILE
    qw = C_Q_TILE
    per = tk // qw
    nkt = t // tk
    k_sel = min(TOPK_MAX, t // 4)
    iqt = iqt.reshape(batch, nkt, IDX_HEADS * IDX_DIM, tk)
    iwt = iwt.reshape(batch, nkt, 16, tk)
    cqt = cqt.reshape(batch, nkt, C_OUT, tk)
    cvt = cvt.reshape(batch, nkt, HEAD_DIM, tk)
    ck = ck.reshape(batch, t, LANES)
    r = np.arange(tk)
    tri = jnp.asarray(r[:, None] >= r[None, :], BF16)
    qblock = lambda rows: pl.BlockSpec((None, None, rows, qw), lambda b, i: (b, i // per, 0, i % per))
    return pl.pallas_call(
        functools.partial(_dsa_kernel, k_sel=k_sel),
        out_shape=jax.ShapeDtypeStruct((batch, t, C_OUT), BF16),
        grid=(batch, t // qw),
        in_specs=[qblock(IDX_HEADS * IDX_DIM), qblock(16), qblock(C_OUT),
                  pl.BlockSpec((None, t, LANES), lambda b, i: (b, 0, 0)),
                  pl.BlockSpec((None, nkt, HEAD_DIM, tk), lambda b, i: (b, 0, 0, 0)),
                  _const_spec(tab.shape), _const_spec(tri.shape)],
        out_specs=pl.BlockSpec((None, qw, C_OUT), lambda b, i: (b, i, 0)),
        scratch_shapes=[pltpu.VMEM((t, qw), F32)],
        compiler_params=_cparams(("parallel", "arbitrary")),
        name="dsa_attention",
    )(iqt, iwt, cqt, ck, cvt, tab, tri).reshape(batch * t, C_OUT)


def _merge_kernel(x_ref, *refs):
    ng = len(DIL_GROUPS)
    a_refs = refs[:2 * ng]
    ob_ref, oc_ref, gate_ref, wa_ref, wb_ref, wc_ref, wo_ref, out_ref = refs[2 * ng:-1]
    shuffle_ref = refs[-1]
    tm, d = x_ref.shape

    def token_order(ref, slot):
        dil = ref.shape[0]
        if dil == 1:
            return ref[0].astype(F32)
        halves = range(A_OUT // LANES)
        for r in range(dil):
            for half in halves:
                shuffle_ref[slot, half, pl.ds(r, tm // dil, stride=dil), :] = (
                    ref[r, :, half * LANES:(half + 1) * LANES].astype(F32))
        return jnp.concatenate([shuffle_ref[slot, half] for half in halves], axis=1)

    outs = [token_order(a_refs[2 * g], 2 * g) for g in range(ng)]
    lses = [token_order(a_refs[2 * g + 1], 2 * g + 1) for g in range(ng)]
    top = functools.reduce(jnp.maximum, lses)
    es = [jnp.exp(lse - top) for lse in lses]
    num = sum(e * o for e, o in zip(es, outs))
    oa = (num / sum(es)).astype(BF16)
    y = gate_ref[:, 0:d].astype(F32) * jnp.dot(oa, wa_ref[...], preferred_element_type=F32)
    y = y + gate_ref[:, d:2 * d].astype(F32) * jnp.dot(ob_ref[...], wb_ref[...], preferred_element_type=F32)
    y = y + gate_ref[:, 2 * d:3 * d].astype(F32) * jnp.dot(oc_ref[...], wc_ref[...], preferred_element_type=F32)
    out_ref[...] = x_ref[...] + jnp.dot(y.astype(BF16), wo_ref[...], preferred_element_type=F32)


def _merge(x, a_parts, ob, oc, gates, wa, wb, wc, wo, t):
    n, d = x.shape
    tm = TOKEN_TILE
    per_batch = t // tm
    tok = lambda c: pl.BlockSpec((tm, c), lambda i: (i, 0))
    by_residue = lambda dil: pl.BlockSpec((None, dil, tm // dil, A_OUT),
                                          lambda i: (i // per_batch, 0, i % per_batch, 0))
    ws = [w.astype(BF16) for w in (wa, wb, wc, wo)]
    return pl.pallas_call(
        _merge_kernel,
        out_shape=jax.ShapeDtypeStruct((n, d), F32),
        grid=(n // tm,),
        in_specs=[tok(d)] + [by_residue(z.shape[1]) for z in a_parts] + [tok(B_OUT), tok(C_OUT), tok(3 * d)]
                 + [_const_spec(w.shape) for w in ws],
        out_specs=tok(d),
        scratch_shapes=[pltpu.VMEM((len(a_parts), A_OUT // LANES, tm, LANES), F32)],
        compiler_params=_cparams(("parallel",)),
        name="merge",
    )(x, *a_parts, ob, oc, gates, *ws)


def _token_mixer(x, batch, t, layer, mix_norm, w_in, qk_gain, diff_lambda, diff_out_norm,
                 w_branch_a, w_branch_b, w_branch_c, w_out, band_tabs, tab_b, tab_c):
    ng = len(DIL_GROUPS)
    outs = _project(x, mix_norm, w_in, qk_gain, batch, t)
    a_in, (bk, ck, gates, bqt, bvt, cqt, cvt, iqt, iwt) = outs[:3 * ng], outs[3 * ng:]
    a_parts = []
    for g, (_, dilation) in enumerate(DIL_GROUPS):
        a_parts += _dilated_group(a_in[g], a_in[ng + g], a_in[2 * ng + g], band_tabs[g], dilation)
    lam_init = 0.8 - 0.6 * np.exp(-0.3 * layer)
    lam_rows = jnp.concatenate([diff_lambda.astype(F32), jnp.full((4, HEAD_DIM), lam_init, F32)], axis=0)
    gn = jnp.broadcast_to((diff_out_norm.astype(F32) * (1.0 - lam_init))[:, None], (B_V_DIM, B_Q_TILE))
    ob = _diff_attention(bqt, bk, bvt, tab_b, lam_rows, gn, batch, t)
    oc = _dsa_attention(iqt, iwt, cqt, ck, cvt, tab_c, batch, t)
    return _merge(x, a_parts, ob, oc, gates, w_branch_a, w_branch_b, w_branch_c, w_out, t)


def kernel(x, rel_bias, ffn1_norm, ffn1_w_gate, ffn1_w_up, ffn1_w_down, mix_norm, w_in, qk_gain,
           diff_lambda, diff_out_norm, w_branch_a, w_branch_b, w_branch_c, w_out,
           ffn2_norm, ffn2_w_gate, ffn2_w_up, ffn2_w_down):
    batch, t, d = x.shape
    depth = w_in.shape[0]
    assert t % (DIL_GROUPS[-1][1] * LANES) == 0 and t % TOKEN_TILE == 0
    band_tabs = [_band_tables(rel_bias[:, g * A_GROUP_HEADS:(g + 1) * A_GROUP_HEADS], dil)
                 for g, (_, dil) in enumerate(DIL_GROUPS)]
    tab_b = _toeplitz_tables(rel_bias[:, A_HEADS:A_HEADS + B_HEADS] * LOG2E)
    tab_c = _toeplitz_tables(rel_bias[:, A_HEADS + B_HEADS:] * LOG2E)
    h = x.reshape(batch * t, d).astype(F32)
    for i in range(depth):
        h = _ffn(h, ffn1_norm[i], ffn1_w_gate[i], ffn1_w_up[i], ffn1_w_down[i])
        h = _token_mixer(h, batch, t, i, mix_norm[i], w_in[i], qk_gain[i], diff_lambda[i], diff_out_norm[i],
                         w_branch_a[i], w_branch_b[i], w_branch_c[i], w_out[i], band_tabs, tab_b, tab_c)
        h = _ffn(h, ffn2_norm[i], ffn2_w_gate[i], ffn2_w_up[i], ffn2_w_down[i])
    return h.reshape(batch, t, d).astype(x.dtype)
```

```python
import functools

import numpy as np
import jax
import jax.numpy as jnp
from jax import lax
from jax.experimental import pallas as pl
from jax.experimental.pallas import tpu as pltpu

F32 = jnp.float32
BF16 = jnp.bfloat16

HEAD_DIM = 64
DIL_GROUPS = ((128, 1), (512, 4), (2048, 16))
A_GROUP_HEADS = 4
A_HEADS = A_GROUP_HEADS * len(DIL_GROUPS)
A_OUT = A_GROUP_HEADS * HEAD_DIM
B_HEADS = 4
B_V_DIM = 2 * HEAD_DIM
B_OUT = B_HEADS * B_V_DIM
C_HEADS = 4
C_OUT = C_HEADS * HEAD_DIM
IDX_HEADS = 8
IDX_DIM = 64
TOPK_MAX = 256
NUM_BUCKETS = 32
MAX_DISTANCE = 2048
RMS_EPS = 1e-6
LOG2E = 1.4426950408889634

LANES = 128
SUBLANES = 8
TOKEN_TILE = 512
FFN_TILE = 1024
A_BLOCKS_PER_STEP = 4
B_Q_TILE = 512
C_Q_TILE = 256
MAX_LAG = 60.0
FFN_CHUNK = 256
VMEM_LIMIT = 58 * 1024 * 1024

NEG = -1e30
M_INIT = -1e29
BIG = 1e30
THR_ALL = -1e29
N_BISECT = 14


def _cparams(sem):
    return pltpu.CompilerParams(dimension_semantics=sem, vmem_limit_bytes=VMEM_LIMIT)


def _const_spec(shape):
    nd = len(shape)
    return pl.BlockSpec(shape, lambda *_: (0,) * nd)


def _weight_spec(shape):
    nd = len(shape)
    return pl.BlockSpec(shape, lambda *_: (0,) * nd, pipeline_mode=pl.Buffered(1))


def _rel_bucket_np(dist):
    n = np.maximum(dist, 0)
    max_exact = NUM_BUCKETS // 2
    nf = np.maximum(n, 1).astype(np.float64)
    large = max_exact + (np.log(nf / max_exact) / np.log(MAX_DISTANCE / max_exact)
                         * (NUM_BUCKETS - max_exact)).astype(np.int64)
    large = np.minimum(large, NUM_BUCKETS - 1)
    return np.where(n < max_exact, n, large)


def _far_delta():
    d = 1
    while not np.all(_rel_bucket_np(np.arange(d * LANES - LANES + 1, d * LANES + LANES)) == NUM_BUCKETS - 1):
        d += 1
    return d


FAR = _far_delta()
MASKED = FAR + 1


def _toeplitz(w, n_rows, n_cols):
    period = n_rows + n_cols
    w = jnp.pad(w, ((0, 0), (0, period - w.shape[1])))
    m = jnp.tile(w, (1, n_rows))[:, :n_rows * (period - 1)].reshape(-1, n_rows, period - 1)
    return m[:, :, n_rows - 1:n_rows - 1 + n_cols]


def _bias_by_distance(bias_heads, dist, valid):
    vals = jnp.take(bias_heads.astype(F32), jnp.asarray(_rel_bucket_np(dist), jnp.int32), axis=0).T
    return jnp.where(jnp.asarray(valid)[None], vals, NEG)


def _toeplitz_tables(bias_heads):
    n_cols = (FAR + 1) * LANES
    dist = np.arange(LANES - 1 + n_cols) - (LANES - 1)
    tiles = _toeplitz(_bias_by_distance(bias_heads, dist, dist >= 0), LANES, n_cols)
    tiles = tiles.reshape(-1, LANES, FAR + 1, LANES).transpose(0, 2, 1, 3)
    masked = jnp.full((tiles.shape[0], 1, LANES, LANES), NEG, F32)
    return jnp.concatenate([tiles, masked], axis=1)


def _band_tables(bias_heads, dilation):
    wn = LANES
    sub = np.arange(3 * wn - 1) - (wn - 1)
    w = _bias_by_distance(bias_heads, sub * dilation, (sub >= 0) & (sub <= wn))
    later = jnp.flip(_toeplitz(w, wn, 2 * wn), axis=(1, 2))
    first = jnp.where(jnp.asarray(np.arange(2 * wn) >= wn)[None, None], later, NEG)
    return jnp.stack([first, later])


def _ffn_kernel(x_ref, g_ref, wg_ref, wu_ref, wd_ref, o_ref, acc_ref):
    x = x_ref[...]
    ms = jnp.mean(x * x, axis=-1, keepdims=True)
    h = (x * lax.rsqrt(ms + RMS_EPS) * g_ref[...]).astype(BF16)
    acc_ref[...] = jnp.zeros_like(acc_ref)

    def body(c, carry):
        cols = pl.ds(pl.multiple_of(c * FFN_CHUNK, FFN_CHUNK), FFN_CHUNK)
        g = jnp.dot(h, wg_ref[:, cols], preferred_element_type=F32)
        u = jnp.dot(h, wu_ref[:, cols], preferred_element_type=F32)
        a = (g * jax.nn.sigmoid(g) * u).astype(BF16)
        acc_ref[...] += jnp.dot(a, wd_ref[cols, :], preferred_element_type=F32)
        return carry

    lax.fori_loop(0, wg_ref.shape[1] // FFN_CHUNK, body, 0)
    o_ref[...] = x + 0.5 * acc_ref[...]


def _ffn(x, gain, w_gate, w_up, w_down):
    n, d = x.shape
    f = w_gate.shape[1]
    tm = FFN_TILE
    return pl.pallas_call(
        _ffn_kernel,
        out_shape=jax.ShapeDtypeStruct((n, d), F32),
        grid=(n // tm,),
        in_specs=[pl.BlockSpec((tm, d), lambda i: (i, 0)),
                  _const_spec((1, d)), _weight_spec((d, f)), _weight_spec((d, f)), _weight_spec((f, d))],
        out_specs=pl.BlockSpec((tm, d), lambda i: (i, 0)),
        scratch_shapes=[pltpu.VMEM((tm, d), F32)],
        compiler_params=_cparams(("parallel",)),
        name="ffn",
    )(x, gain.reshape(1, d).astype(F32), w_gate.astype(BF16), w_up.astype(BF16), w_down.astype(BF16))


S_AQ, S_AK, S_AV = 0, 768, 1536
S_BK, S_CK, S_GATE, S_END = 2304, 2816, 2944, 6016
T_BQ, T_BV, T_CQ, T_CV, T_IQ, T_IW, T_END = 0, 512, 1024, 1280, 1344, 1856, 1872


def _proj_kernel(x_ref, g_ref, ws_ref, wt_ref, bd_ref, gs_ref, gt_ref, *refs):
    ng = len(DIL_GROUPS)
    a_refs = refs[:3 * ng]
    bk_ref, ck_ref, gate_ref, bqt_ref, bvt_ref, cqt_ref, cvt_ref, iqt_ref, iwt_ref = refs[3 * ng:-1]
    shuffle_ref = refs[-1]
    tm = x_ref.shape[0]

    def store_by_residue(y, which):
        for g, (_, dil) in enumerate(DIL_GROUPS):
            out = a_refs[which * ng + g]
            part = y[:, g * A_OUT:(g + 1) * A_OUT]
            if dil == 1:
                out[0] = part.astype(BF16)
            else:
                for half in range(A_OUT // LANES):
                    shuffle_ref[half] = part[:, half * LANES:(half + 1) * LANES]
                for r in range(dil):
                    out[r] = jnp.concatenate(
                        [shuffle_ref[half, pl.ds(r, tm // dil, stride=dil), :] for half in range(A_OUT // LANES)],
                        axis=1).astype(BF16)

    x = x_ref[...]
    ms = jnp.mean(x * x, axis=-1, keepdims=True)
    h = (x * lax.rsqrt(ms + RMS_EPS) * g_ref[...]).astype(BF16)
    bd = bd_ref[...]

    def dot_s(c0, c1):
        return jnp.dot(h, ws_ref[:, c0:c1], preferred_element_type=F32)

    def head_inv_rms(y):
        width = bd.shape[0]
        outs = []
        for c0 in range(0, y.shape[1], width):
            sq = y[:, c0:c0 + width]
            n = sq.shape[1]
            msq = jnp.dot((sq * sq).astype(BF16), bd[:n, :n], preferred_element_type=F32)
            outs.append(lax.rsqrt(msq + RMS_EPS))
        return outs[0] if len(outs) == 1 else jnp.concatenate(outs, axis=1)

    y = dot_s(S_AQ, S_AK)
    store_by_residue(y * head_inv_rms(y) * gs_ref[:, 0:768], 0)
    y = dot_s(S_AK, S_AV)
    store_by_residue(y * head_inv_rms(y) * gs_ref[:, 768:1536], 1)
    store_by_residue(dot_s(S_AV, S_BK), 2)
    y = dot_s(S_BK, S_CK)
    bk_ref[...] = (y * head_inv_rms(y) * gs_ref[:, 1536:2048]).astype(BF16)
    y = dot_s(S_CK, S_GATE)
    lane = lax.broadcasted_iota(jnp.int32, y.shape, 1)
    inv = jnp.where(lane < HEAD_DIM, head_inv_rms(y), 1.0)
    ck_ref[...] = (y * inv * gs_ref[:, 2048:2176]).astype(BF16)
    for c in range(3):
        y = dot_s(S_GATE + c * 1024, S_GATE + (c + 1) * 1024)
        gate_ref[:, c * 1024:(c + 1) * 1024] = jax.nn.sigmoid(y).astype(BF16)

    def dot_t(r0, r1):
        return lax.dot_general(wt_ref[r0:r1, :], h, (((1,), (1,)), ((), ())),
                               preferred_element_type=F32)

    def norm_t(y, gain):
        r = y.shape[0] // HEAD_DIM
        y3 = y.reshape(r, HEAD_DIM, tm)
        msq = jnp.mean(y3 * y3, axis=1, keepdims=True)
        return (y3 * lax.rsqrt(msq + RMS_EPS)).reshape(r * HEAD_DIM, tm) * gain

    bqt_ref[...] = norm_t(dot_t(T_BQ, T_BV), gt_ref[0:512, :]).astype(BF16)
    bvt_ref[...] = dot_t(T_BV, T_CQ).astype(BF16)
    cqt_ref[...] = norm_t(dot_t(T_CQ, T_CV), gt_ref[512:768, :]).astype(BF16)
    cvt_ref[...] = dot_t(T_CV, T_IQ).astype(BF16)
    iqt_ref[...] = dot_t(T_IQ, T_IW).astype(BF16)
    iwt_ref[...] = dot_t(T_IW, T_END) * (IDX_HEADS ** -0.5 * IDX_DIM ** -0.5)


def _proj_weights(w_in, qk_gain, tm):
    d = w_in.shape[0]
    o = 0
    a_qkv = w_in[:, o:o + 3 * A_HEADS * HEAD_DIM].reshape(d, 3, A_HEADS * HEAD_DIM)
    o += 3 * A_HEADS * HEAD_DIM
    b_qk = w_in[:, o:o + 4 * B_HEADS * HEAD_DIM].reshape(d, 4, B_HEADS, HEAD_DIM)
    o += 4 * B_HEADS * HEAD_DIM
    b_v = w_in[:, o:o + B_OUT]
    o += B_OUT
    c_q = w_in[:, o:o + C_OUT]
    c_k = w_in[:, o + C_OUT:o + C_OUT + HEAD_DIM]
    c_v = w_in[:, o + C_OUT + HEAD_DIM:o + C_OUT + 2 * HEAD_DIM]
    o += C_OUT + 2 * HEAD_DIM
    i_q = w_in[:, o:o + IDX_HEADS * IDX_DIM]
    i_k = w_in[:, o + IDX_HEADS * IDX_DIM:o + IDX_HEADS * IDX_DIM + IDX_DIM]
    i_w = w_in[:, o + IDX_HEADS * IDX_DIM + IDX_DIM:o + IDX_HEADS * IDX_DIM + IDX_DIM + IDX_HEADS]
    o += IDX_HEADS * IDX_DIM + IDX_DIM + IDX_HEADS
    gates = w_in[:, o:]
    b_k = jnp.stack([b_qk[:, 2], b_qk[:, 3]], axis=2).reshape(d, 2 * B_HEADS * HEAD_DIM)
    b_q = jnp.stack([b_qk[:, 0], b_qk[:, 1]], axis=2).reshape(d, 2 * B_HEADS * HEAD_DIM)
    w_s = jnp.concatenate([a_qkv[:, 0], a_qkv[:, 1], a_qkv[:, 2], b_k, c_k, i_k, gates], axis=1)
    w_t = jnp.concatenate([b_q, b_v, c_q, c_v, i_q, i_w, jnp.zeros((d, 8), w_in.dtype)], axis=1).T
    assert w_s.shape[1] == S_END and w_t.shape[0] == T_END
    scale = HEAD_DIM ** -0.5
    g = qk_gain.astype(F32)
    gs = jnp.concatenate([jnp.tile(g[0, 0] * scale, A_HEADS), jnp.tile(g[0, 1], A_HEADS),
                          jnp.tile(g[1, 1], 2 * B_HEADS), g[2, 1], jnp.ones((IDX_DIM,), F32)])[None]
    gt = jnp.concatenate([jnp.tile(g[1, 0] * (scale * LOG2E), 2 * B_HEADS),
                          jnp.tile(g[2, 0] * (scale * LOG2E), C_HEADS)])
    gt = jnp.broadcast_to(gt[:, None], (gt.shape[0], tm))
    return w_s.astype(BF16), w_t.astype(BF16), gs, gt


MXU_TILE = 256


def _head_block_diag():
    r = np.arange(MXU_TILE)
    return jnp.asarray((r[:, None] // HEAD_DIM == r[None, :] // HEAD_DIM) / HEAD_DIM, BF16)


def _project(x, gain, w_in, qk_gain, batch, t):
    n, d = x.shape
    tm = TOKEN_TILE
    nt = n // tm
    per_batch = t // tm
    w_s, w_t, gs, gt = _proj_weights(w_in, qk_gain, tm)
    tok = lambda c: pl.BlockSpec((tm, c), lambda i: (i, 0))
    feat = lambda r: pl.BlockSpec((None, r, tm), lambda i: (i, 0, 0))
    a_shapes, a_specs = [], []
    for _ in range(3):
        for _, dil in DIL_GROUPS:
            a_shapes.append(jax.ShapeDtypeStruct((batch, dil, t // dil, A_OUT), BF16))
            a_specs.append(pl.BlockSpec((None, dil, tm // dil, A_OUT),
                                        lambda i: (i // per_batch, 0, i % per_batch, 0)))
    out_shape = a_shapes + [
        jax.ShapeDtypeStruct((n, 512), BF16), jax.ShapeDtypeStruct((n, 128), BF16),
        jax.ShapeDtypeStruct((n, 3072), BF16),
        jax.ShapeDtypeStruct((nt, 512, tm), BF16), jax.ShapeDtypeStruct((nt, 512, tm), BF16),
        jax.ShapeDtypeStruct((nt, 256, tm), BF16), jax.ShapeDtypeStruct((nt, 64, tm), BF16),
        jax.ShapeDtypeStruct((nt, 512, tm), BF16), jax.ShapeDtypeStruct((nt, 16, tm), F32)]
    out_specs = a_specs + [tok(512), tok(128), tok(3072),
                           feat(512), feat(512), feat(256), feat(64), feat(512), feat(16)]
    return pl.pallas_call(
        _proj_kernel,
        out_shape=out_shape,
        grid=(nt,),
        in_specs=[tok(d), _const_spec((1, d)), _weight_spec(w_s.shape), _weight_spec(w_t.shape),
                  _const_spec((MXU_TILE, MXU_TILE)), _const_spec(gs.shape), _const_spec(gt.shape)],
        out_specs=out_specs,
        scratch_shapes=[pltpu.VMEM((A_OUT // LANES, tm, LANES), F32)],
        compiler_params=_cparams(("parallel",)),
        name="proj",
    )(x, gain.reshape(1, d).astype(F32), w_s, w_t, _head_block_diag(), gs, gt)


def _dil_kernel(q_ref, kp_ref, kc_ref, vp_ref, vc_ref, bias_ref, o_ref, lse_ref):
    nq = q_ref.shape[0] // LANES
    qi = pl.program_id(2)
    lane = lax.broadcasted_iota(jnp.int32, (LANES, A_OUT), 1) // HEAD_DIM
    mine = [lane == h for h in range(A_GROUP_HEADS)]
    blocks = [slice(jb * LANES, (jb + 1) * LANES) for jb in range(nq)]

    def band(prev_ref, cur_ref, jb):
        if jb == 0:
            return jnp.concatenate([prev_ref[...], cur_ref[blocks[0], :]], axis=0)
        return cur_ref[(jb - 1) * LANES:(jb + 1) * LANES, :]

    logits = []
    for jb in range(nq):
        q = q_ref[blocks[jb], :]
        q4 = jnp.concatenate([jnp.where(mine[h], q, jnp.zeros_like(q)) for h in range(A_GROUP_HEADS)], axis=0)
        s = lax.dot_general(q4, band(kp_ref, kc_ref, jb), (((1,), (1,)), ((), ())), preferred_element_type=F32)
        bias = bias_ref[jnp.minimum(qi, 1) if jb == 0 else 1]
        logits.append(s + bias.reshape(A_GROUP_HEADS * LANES, 2 * LANES))
    probs, stats = [], []
    for s in logits:
        m = jnp.max(s, axis=1, keepdims=True)
        p = jnp.exp(s - m)
        ssum = jnp.sum(p, axis=1, keepdims=True)
        probs.append(p.astype(BF16))
        stats.append((1.0 / ssum, m + jnp.log(ssum)))
    for jb in range(nq):
        pv = jnp.dot(probs[jb], band(vp_ref, vc_ref, jb), preferred_element_type=F32)
        inv, lse4 = stats[jb]
        o = jnp.zeros((LANES, A_OUT), F32)
        lse = jnp.zeros((LANES, A_OUT), F32)
        for h in range(A_GROUP_HEADS):
            head = slice(h * LANES, (h + 1) * LANES)
            o = jnp.where(mine[h], pv[head] * inv[head], o)
            lse = jnp.where(mine[h], lse4[head], lse)
        o_ref[blocks[jb], :] = o.astype(BF16)
        lse_ref[blocks[jb], :] = lse


def _dilated_group(aq, ak, av, bias, dilation):
    batch, _, n, _ = aq.shape
    nblk = n // LANES
    nq = min(nblk, A_BLOCKS_PER_STEP)
    qt = nq * LANES
    cur = pl.BlockSpec((None, None, qt, A_OUT), lambda b, r, i: (b, r, i, 0))
    prev = pl.BlockSpec((None, None, LANES, A_OUT), lambda b, r, i: (b, r, jnp.maximum(i * nq - 1, 0), 0))
    shp = (batch, dilation, n, A_OUT)
    return pl.pallas_call(
        _dil_kernel,
        out_shape=[jax.ShapeDtypeStruct(shp, BF16), jax.ShapeDtypeStruct(shp, F32)],
        grid=(batch, dilation, nblk // nq),
        in_specs=[cur, prev, cur, prev, cur, _const_spec(bias.shape)],
        out_specs=[cur, cur],
        compiler_params=_cparams(("parallel", "parallel", "parallel")),
        name=f"dilated_d{dilation}",
    )(aq, ak, ak, av, av, bias)


def _fold_rows(x, op):
    r, c = x.shape
    return op(x.reshape(r // 64, 64, c), axis=0) if r > 64 else x


def _reduce_rows(x, op):
    x = _fold_rows(x, op)
    x = op(x.reshape(8, 8, x.shape[1]), axis=0)
    return op(x, axis=0, keepdims=True)


def _bias_tile(tab_ref, head, qblk, kblk):
    delta = qblk - kblk
    idx = jnp.where(delta < 0, MASKED, jnp.minimum(delta, FAR))
    if head is None:
        return tab_ref[idx]
    return tab_ref[head, idx]


def _diff_kernel(qt_ref, k_ref, vt_ref, tab_ref, lam_ref, gn_ref, o_ref):
    tq = qt_ref.shape[1]
    tk = TOKEN_TILE
    qi = pl.program_id(2)
    qt = qt_ref[...]
    row = lax.broadcasted_iota(jnp.int32, qt.shape, 0)
    q12 = jnp.concatenate([jnp.where(row < HEAD_DIM, qt, jnp.zeros_like(qt)),
                           jnp.where(row >= HEAD_DIM, qt, jnp.zeros_like(qt))], axis=1)
    nqb = tq // LANES
    nkb = tk // LANES

    def logits(c):
        kc = k_ref[pl.ds(pl.multiple_of(c * tk, tk), tk), :]
        bias = jnp.concatenate(
            [jnp.concatenate([_bias_tile(tab_ref, None, qi * nqb + iq, c * nkb + jk)
                              for iq in range(nqb)] * 2, axis=1) for jk in range(nkb)], axis=0)
        return jnp.dot(kc, q12, preferred_element_type=F32) + bias

    def exact_step(c, carry):
        m, l, acc = carry
        s = logits(c)
        m_new = jnp.maximum(m, _reduce_rows(s, jnp.max))
        alpha = jnp.exp2(m - m_new)
        p = jnp.exp2(s - m_new)
        l = alpha * l + _reduce_rows(p, jnp.sum)
        acc = alpha * acc + jnp.dot(vt_ref[c], p.astype(BF16), preferred_element_type=F32)
        return m_new, l, acc

    def lagged_update(c, s, m, l, acc, jump):
        p = jnp.exp2(s - m)
        top =_reduce_rows(s, jnp.max)
        l = l + _reduce_rows(p, jnp.sum)
        acc = acc + jnp.dot(vt_ref[c], p.astype(BF16), preferred_element_type=F32)
        m_new = jnp.maximum(m, top)
        alpha = jnp.exp2(m - m_new)
        return m_new, alpha * l, alpha * acc, jnp.maximum(jump, top - m)

    nch = ((qi + 1) * tq + tk - 1) // tk
    zero = jnp.zeros((1, 2 * tq), F32)
    acc0 = jnp.zeros((B_V_DIM, 2 * tq), F32)

    def opening(count):
        ss = [logits(c) for c in range(count)]
        state = (jnp.max(ss[0][0:SUBLANES], axis=0, keepdims=True), zero, acc0, zero)
        for c in range(count):
            state = lagged_update(c, ss[c], *state)
        return state

    def lagged_pair(c, carry):
        s_a, s_b = logits(c), logits(c + 1)
        return lagged_update(c + 1, s_b, *lagged_update(c, s_a, *carry))

    odd = nch % 2
    state = lax.cond(odd == 1, lambda: opening(1), lambda: opening(2))
    _, l, acc, jump = lax.fori_loop(0, (nch - 2 + odd) // 2,
                                    lambda i, carry: lagged_pair(2 - odd + 2 * i, carry), state)
    l, acc = lax.cond(jnp.max(jump) > MAX_LAG,
                      lambda: lax.fori_loop(0, nch, exact_step, (jnp.full((1, 2 * tq), M_INIT, F32), zero, acc0))[1:],
                      lambda: (l, acc))
    a1, a2 = acc[:, :tq], acc[:, tq:]
    l1, l2 = l[:, :tq], l[:, tq:]

    lv = lam_ref[...]
    lam = (jnp.exp(jnp.sum(lv[0:1] * lv[1:2], axis=1, keepdims=True))
           - jnp.exp(jnp.sum(lv[2:3] * lv[3:4], axis=1, keepdims=True)) + lv[4:5, 0:1])
    o = a1 / l1 - lam * (a2 / l2)
    ms = jnp.mean(o * o, axis=0, keepdims=True)
    o = o * lax.rsqrt(ms + RMS_EPS) * gn_ref[...]
    o_ref[...] = o.T.astype(BF16)


def _diff_attention(bqt, bk, bvt, tab, lam_rows, gn, batch, t):
    tq = B_Q_TILE
    tk = TOKEN_TILE
    per = tk // tq
    nkt = t // tk
    bqt = bqt.reshape(batch, nkt, B_HEADS * LANES, tk)
    bvt = bvt.reshape(batch, nkt, B_OUT, tk)
    bk = bk.reshape(batch, t, B_HEADS * LANES)
    return pl.pallas_call(
        _diff_kernel,
        out_shape=jax.ShapeDtypeStruct((batch, t, B_OUT), BF16),
        grid=(batch, B_HEADS, t // tq),
        in_specs=[pl.BlockSpec((None, None, LANES, tq), lambda b, h, i: (b, i // per, h, i % per)),
                  pl.BlockSpec((None, t, LANES), lambda b, h, i: (b, 0, h)),
                  pl.BlockSpec((None, nkt, B_V_DIM, tk), lambda b, h, i: (b, 0, h, 0)),
                  pl.BlockSpec((None, FAR + 2, LANES, LANES), lambda b, h, i: (h, 0, 0, 0)),
                  _const_spec(lam_rows.shape), _const_spec(gn.shape)],
        out_specs=pl.BlockSpec((None, tq, B_V_DIM), lambda b, h, i: (b, i, h)),
        compiler_params=_cparams(("parallel", "parallel", "arbitrary")),
        name="diff_attention",
    )(bqt, bk, bvt, tab, lam_rows, gn).reshape(batch * t, B_OUT)


def _dsa_kernel(iqt_ref, iwt_ref, cqt_ref, k_ref, vt_ref, tab_ref, tri_ref, o_ref, s_ref, *, k_sel):
    tk = TOKEN_TILE
    nkb = tk // LANES
    qw = o_ref.shape[0]
    nqb = qw // LANES
    qi = pl.program_id(1)
    nch = ((qi + 1) * qw + tk - 1) // tk
    qpos = qi * qw + lax.broadcasted_iota(jnp.int32, (1, qw), 1)
    zeros = jnp.zeros((HEAD_DIM, qw), BF16)
    iq = iqt_ref[...]
    w = iwt_ref[...]
    iq_all = jnp.concatenate([jnp.concatenate([zeros, iq[h * IDX_DIM:(h + 1) * IDX_DIM]], axis=0)
                              for h in range(IDX_HEADS)], axis=1)
    cq = cqt_ref[...]
    cq_all = jnp.concatenate([jnp.concatenate([cq[h * HEAD_DIM:(h + 1) * HEAD_DIM], zeros], axis=0)
                              for h in range(C_HEADS)], axis=1)

    def chunk(c):
        return pl.ds(pl.multiple_of(c * tk, tk), tk)

    def raw_scores(c):
        return jnp.dot(k_ref[chunk(c), :], iq_all, preferred_element_type=F32)

    def score_chunk(c, raw, mn, mx, last):
        acc = w[0:1, :] * jnp.maximum(raw[:, 0:qw], 0.0)
        for h in range(1, IDX_HEADS):
            acc = acc + w[h:h + 1, :] * jnp.maximum(raw[:, h * qw:(h + 1) * qw], 0.0)
        if last:
            kpos = c * tk + lax.broadcasted_iota(jnp.int32, (tk, qw), 0)
            causal = kpos <= qpos
            s_ref[chunk(c), :] = jnp.where(causal, acc, NEG)
            mn = jnp.minimum(mn, _fold_rows(jnp.where(causal, acc, BIG), jnp.min))
            mx = jnp.maximum(mx, _fold_rows(jnp.where(causal, acc, NEG), jnp.max))
        else:
            s_ref[chunk(c), :] = acc
            mn = jnp.minimum(mn, _fold_rows(acc, jnp.min))
            mx = jnp.maximum(mx, _fold_rows(acc, jnp.max))
        return mn, mx

    def score_pair(c, carry, last):
        raw_a, raw_b = raw_scores(c), raw_scores(c + 1)
        return score_chunk(c + 1, raw_b, *score_chunk(c, raw_a, *carry, last=False), last=last)

    carry = lax.fori_loop(0, (nch - 1) // 2, lambda i, carry: score_pair(2 * i, carry, last=False),
                          (jnp.full((64, qw), BIG, F32), jnp.full((64, qw), NEG, F32)))
    mn, mx = lax.cond((nch - 1) % 2 == 1,
                      lambda st: score_pair(nch - 2, st, last=True),
                      lambda st: score_chunk(nch - 1, raw_scores(nch - 1), *st, last=True), carry)
    lo, hi = _reduce_rows(mn, jnp.min), _reduce_rows(mx, jnp.max)

    def count_gt(thr):
        def body(c, gt):
            return gt + _fold_rows(jnp.where(s_ref[chunk(c), :] > thr, 1.0, 0.0), jnp.sum)
        return _reduce_rows(lax.fori_loop(0, nch, body, jnp.zeros((64, qw), F32)), jnp.sum)

    def largest_upto(bound):
        def body(c, mx):
            s = s_ref[chunk(c), :]
            return jnp.maximum(mx, _fold_rows(jnp.where(s <= bound, s, NEG), jnp.max))
        return _reduce_rows(lax.fori_loop(0, nch, body, jnp.full((64, qw), NEG, F32)), jnp.max)

    def next_below_and_multiplicity(cand):
        def body(c, carry):
            mx, eq = carry
            s = s_ref[chunk(c), :]
            mx = jnp.maximum(mx, _fold_rows(jnp.where(s < cand, s, NEG), jnp.max))
            eq = eq + _fold_rows(jnp.where(s == cand, 1.0, 0.0), jnp.sum)
            return mx, eq
        mx, eq = lax.fori_loop(0, nch, body, (jnp.full((64, qw), NEG, F32), jnp.zeros((64, qw), F32)))
        return _reduce_rows(mx, jnp.max), _reduce_rows(eq, jnp.sum)

    kf = float(k_sel)
    need = qpos >= k_sel

    def bisect(_, carry):
        lo, hi, above = carry
        mid = lo + (hi - lo) * 0.5
        cnt = count_gt(mid)
        below = cnt < kf
        return jnp.where(below, lo, mid), jnp.where(below, mid, hi), jnp.where(below, cnt, above)

    _, hi, above = lax.fori_loop(0, N_BISECT, bisect, (lo - 1.0, hi, jnp.zeros((1, qw), F32)))

    def walk_cond(carry):
        _, _, ge = carry
        return jnp.max(jnp.where(need & (ge < kf), 1.0, 0.0)) > 0.0

    def walk_body(carry):
        cand, gt, ge = carry
        active = ge < kf
        nxt, mult = next_below_and_multiplicity(cand)
        ge_new = gt + mult
        moved = active & (ge_new < kf)
        return jnp.where(moved, nxt, cand), jnp.where(moved, ge_new, gt), jnp.where(active, ge_new, ge)

    cand, gt, ge = lax.while_loop(walk_cond, walk_body, (largest_upto(hi), above, above))
    thr = jnp.where(need, cand, THR_ALL)
    want_eq = jnp.where(need, kf - gt, 0.0)

    any_tie = jnp.max(jnp.where(need & (ge > kf), 1.0, 0.0)) > 0.0

    def mark_with_ties(c, eq_seen):
        s = s_ref[chunk(c), :]
        eq = jnp.where(s == thr, 1.0, 0.0)
        rank = eq_seen + jnp.dot(tri_ref[...], eq.astype(BF16), preferred_element_type=F32)
        keep = jnp.where(s > thr, 1.0, jnp.where(rank <= want_eq, eq, 0.0))
        s_ref[chunk(c), :] = jnp.where(keep > 0.5, 0.0, NEG)
        return eq_seen + _reduce_rows(eq, jnp.sum)

    def mark_no_ties(c, carry):
        s_ref[chunk(c), :] = jnp.where(s_ref[chunk(c), :] >= thr, 0.0, NEG)
        return carry

    @pl.when(any_tie)
    def _():
        lax.fori_loop(0, nch, mark_with_ties, jnp.zeros((1, qw), F32))

    @pl.when(jnp.logical_not(any_tie))
    def _():
        lax.fori_loop(0, nch, mark_no_ties, 0)

    def masked_logits(c):
        sel = s_ref[chunk(c), :]
        bias = jnp.concatenate(
            [jnp.concatenate([jnp.concatenate([_bias_tile(tab_ref, h, qi * nqb + iq, c * nkb + jk)
                                               for iq in range(nqb)], axis=1) for jk in range(nkb)], axis=0) + sel
             for h in range(C_HEADS)], axis=1)
        return jnp.dot(k_ref[chunk(c), :], cq_all, preferred_element_type=F32) + bias

    def exact_update(c, lg, m, l, acc):
        m_new = jnp.maximum(m, _reduce_rows(lg, jnp.max))
        alpha = jnp.exp2(m - m_new)
        p = jnp.exp2(lg - m_new)
        l = alpha * l + _reduce_rows(p, jnp.sum)
        acc = alpha * acc + jnp.dot(vt_ref[c], p.astype(BF16), preferred_element_type=F32)
        return m_new, l, acc

    def lagged_update(c, lg, m, l, acc, jump):
        p = jnp.exp2(lg - m)
        top = _reduce_rows(lg, jnp.max)
        l = l + _reduce_rows(p, jnp.sum)
        acc = acc + jnp.dot(vt_ref[c], p.astype(BF16), preferred_element_type=F32)
        m_new = jnp.maximum(m, top)
        alpha = jnp.exp2(m - m_new)
        return m_new, alpha * l, alpha * acc, jnp.maximum(jump, top - m)

    wide = C_HEADS * qw
    init = (jnp.full((1, wide), M_INIT, F32), jnp.zeros((1, wide), F32), jnp.zeros((HEAD_DIM, wide), F32))
    def opening(count):
        lgs = [masked_logits(c) for c in range(count)]
        state = exact_update(0, lgs[0], *init) + (jnp.zeros((1, wide), F32),)
        return lagged_update(1, lgs[1], *state) if count == 2 else state

    def lagged_pair(c, carry):
        lg_a, lg_b = masked_logits(c), masked_logits(c + 1)
        return lagged_update(c + 1, lg_b, *lagged_update(c, lg_a, *carry))

    odd = nch % 2
    state = lax.cond(odd == 1, lambda: opening(1), lambda: opening(2))
    _, l, acc, jump = lax.fori_loop(0, (nch - 2 + odd) // 2,
                                    lambda i, carry: lagged_pair(2 - odd + 2 * i, carry), state)
    l, acc = lax.cond(jnp.max(jump) > MAX_LAG,
                      lambda: lax.fori_loop(0, nch, lambda c, st: exact_update(c, masked_logits(c), *st), init)[1:],
                      lambda: (l, acc))
    o = acc / l
    ot = jnp.concatenate([o[:, h * qw:(h + 1) * qw] for h in range(C_HEADS)], axis=0)
    o_ref[...] = ot.T.astype(BF16)


def _dsa_attention(iqt, iwt, cqt, ck, cvt, tab, batch, t):
    tk = TOKEN_TILE
    qw = C_Q_TILE
    per = tk // qw
    nkt = t // tk
    k_sel = min(TOPK_MAX, t // 4)
    iqt = iqt.reshape(batch, nkt, IDX_HEADS * IDX_DIM, tk)
    iwt = iwt.reshape(batch, nkt, 16, tk)
    cqt = cqt.reshape(batch, nkt, C_OUT, tk)
    cvt = cvt.reshape(batch, nkt, HEAD_DIM, tk)
    ck = ck.reshape(batch, t, LANES)
    r = np.arange(tk)
    tri = jnp.asarray(r[:, None] >= r[None, :], BF16)
    qblock = lambda rows: pl.BlockSpec((None, None, rows, qw), lambda b, i: (b, i // per, 0, i % per))
    return pl.pallas_call(
        functools.partial(_dsa_kernel, k_sel=k_sel),
        out_shape=jax.ShapeDtypeStruct((batch, t, C_OUT), BF16),
        grid=(batch, t // qw),
        in_specs=[qblock(IDX_HEADS * IDX_DIM), qblock(16), qblock(C_OUT),
                  pl.BlockSpec((None, t, LANES), lambda b, i: (b, 0, 0)),
                  pl.BlockSpec((None, nkt, HEAD_DIM, tk), lambda b, i: (b, 0, 0, 0)),
                  _const_spec(tab.shape), _const_spec(tri.shape)],
        out_specs=pl.BlockSpec((None, qw, C_OUT), lambda b, i: (b, i, 0)),
        scratch_shapes=[pltpu.VMEM((t, qw), F32)],
        compiler_params=_cparams(("parallel", "arbitrary")),
        name="dsa_attention",
    )(iqt, iwt, cqt, ck, cvt, tab, tri).reshape(batch * t, C_OUT)


def _merge_kernel(x_ref, *refs):
    ng = len(DIL_GROUPS)
    a_refs = refs[:2 * ng]
    ob_ref, oc_ref, gate_ref, wa_ref, wb_ref, wc_ref, wo_ref, out_ref = refs[2 * ng:-1]
    shuffle_ref = refs[-1]
    tm, d = x_ref.shape

    def token_order(ref, slot):
        dil = ref.shape[0]
        if dil == 1:
            return ref[0].astype(F32)
        halves = range(A_OUT // LANES)
        for r in range(dil):
            for half in halves:
                shuffle_ref[slot, half, pl.ds(r, tm // dil, stride=dil), :] = (
                    ref[r, :, half * LANES:(half + 1) * LANES].astype(F32))
        return jnp.concatenate([shuffle_ref[slot, half] for half in halves], axis=1)

    outs = [token_order(a_refs[2 * g], 2 * g) for g in range(ng)]
    lses = [token_order(a_refs[2 * g + 1], 2 * g + 1) for g in range(ng)]
    top = functools.reduce(jnp.maximum, lses)
    es = [jnp.exp(lse - top) for lse in lses]
    num = sum(e * o for e, o in zip(es, outs))
    oa = (num / sum(es)).astype(BF16)
    y = gate_ref[:, 0:d].astype(F32) * jnp.dot(oa, wa_ref[...], preferred_element_type=F32)
    y = y + gate_ref[:, d:2 * d].astype(F32) * jnp.dot(ob_ref[...], wb_ref[...], preferred_element_type=F32)
    y = y + gate_ref[:, 2 * d:3 * d].astype(F32) * jnp.dot(oc_ref[...], wc_ref[...], preferred_element_type=F32)
    out_ref[...] = x_ref[...] + jnp.dot(y.astype(BF16), wo_ref[...], preferred_element_type=F32)


def _merge(x, a_parts, ob, oc, gates, wa, wb, wc, wo, t):
    n, d = x.shape
    tm = TOKEN_TILE
    per_batch = t // tm
    tok = lambda c: pl.BlockSpec((tm, c), lambda i: (i, 0))
    by_residue = lambda dil: pl.BlockSpec((None, dil, tm // dil, A_OUT),
                                          lambda i: (i // per_batch, 0, i % per_batch, 0))
    ws = [w.astype(BF16) for w in (wa, wb, wc, wo)]
    return pl.pallas_call(
        _merge_kernel,
        out_shape=jax.ShapeDtypeStruct((n, d), F32),
        grid=(n // tm,),
        in_specs=[tok(d)] + [by_residue(z.shape[1]) for z in a_parts] + [tok(B_OUT), tok(C_OUT), tok(3 * d)]
                 + [_weight_spec(w.shape) for w in ws],
        out_specs=tok(d),
        scratch_shapes=[pltpu.VMEM((len(a_parts), A_OUT // LANES, tm, LANES), F32)],
        compiler_params=_cparams(("parallel",)),
        name="merge",
    )(x, *a_parts, ob, oc, gates, *ws)


def _token_mixer(x, batch, t, layer, mix_norm, w_in, qk_gain, diff_lambda, diff_out_norm,
                 w_branch_a, w_branch_b, w_branch_c, w_out, band_tabs, tab_b, tab_c):
    ng = len(DIL_GROUPS)
    outs = _project(x, mix_norm, w_in, qk_gain, batch, t)
    a_in, (bk, ck, gates, bqt, bvt, cqt, cvt, iqt, iwt) = outs[:3 * ng], outs[3 * ng:]
    a_parts = []
    for g, (_, dilation) in enumerate(DIL_GROUPS):
        a_parts += _dilated_group(a_in[g], a_in[ng + g], a_in[2 * ng + g], band_tabs[g], dilation)
    lam_init = 0.8 - 0.6 * np.exp(-0.3 * layer)
    lam_rows = jnp.concatenate([diff_lambda.astype(F32), jnp.full((4, HEAD_DIM), lam_init, F32)], axis=0)
    gn = jnp.broadcast_to((diff_out_norm.astype(F32) * (1.0 - lam_init))[:, None], (B_V_DIM, B_Q_TILE))
    ob = _diff_attention(bqt, bk, bvt, tab_b, lam_rows, gn, batch, t)
    oc = _dsa_attention(iqt, iwt, cqt, ck, cvt, tab_c, batch, t)
    return _merge(x, a_parts, ob, oc, gates, w_branch_a, w_branch_b, w_branch_c, w_out, t)


def kernel(x, rel_bias, ffn1_norm, ffn1_w_gate, ffn1_w_up, ffn1_w_down, mix_norm, w_in, qk_gain,
           diff_lambda, diff_out_norm, w_branch_a, w_branch_b, w_branch_c, w_out,
           ffn2_norm, ffn2_w_gate, ffn2_w_up, ffn2_w_down):
    batch, t, d = x.shape
    depth = w_in.shape[0]
    assert t % (DIL_GROUPS[-1][1] * LANES) == 0 and t % TOKEN_TILE == 0
    band_tabs = [_band_tables(rel_bias[:, g * A_GROUP_HEADS:(g + 1) * A_GROUP_HEADS], dil)
                 for g, (_, dil) in enumerate(DIL_GROUPS)]
    tab_b = _toeplitz_tables(rel_bias[:, A_HEADS:A_HEADS + B_HEADS] * LOG2E)
    tab_c = _toeplitz_tables(rel_bias[:, A_HEADS + B_HEADS:] * LOG2E)
    h = x.reshape(batch * t, d).astype(F32)
    for i in range(depth):
        h = _ffn(h, ffn1_norm[i], ffn1_w_gate[i], ffn1_w_up[i], ffn1_w_down[i])
        h = _token_mixer(h, batch, t, i, mix_norm[i], w_in[i], qk_gain[i], diff_lambda[i], diff_out_norm[i],
                         w_branch_a[i], w_branch_b[i], w_branch_c[i], w_out[i], band_tabs, tab_b, tab_c)
        h = _ffn(h, ffn2_norm[i], ffn2_w_gate[i], ffn2_w_up[i], ffn2_w_down[i])
    return h.reshape(batch, t, d).astype(x.dtype)
```

```python
import functools

import numpy as np
import jax
import jax.numpy as jnp
from jax import lax
from jax.experimental import pallas as pl
from jax.experimental.pallas import tpu as pltpu

F32 = jnp.float32
BF16 = jnp.bfloat16

HEAD_DIM = 64
DIL_GROUPS = ((128, 1), (512, 4), (2048, 16))
A_GROUP_HEADS = 4
A_HEADS = A_GROUP_HEADS * len(DIL_GROUPS)
A_OUT = A_GROUP_HEADS * HEAD_DIM
B_HEADS = 4
B_V_DIM = 2 * HEAD_DIM
B_OUT = B_HEADS * B_V_DIM
C_HEADS = 4
C_OUT = C_HEADS * HEAD_DIM
IDX_HEADS = 8
IDX_DIM = 64
TOPK_MAX = 256
NUM_BUCKETS = 32
MAX_DISTANCE = 2048
RMS_EPS = 1e-6
LOG2E = 1.4426950408889634

LANES = 128
SUBLANES = 8
ONES_ROWS = 16
TOKEN_TILE = 512
FFN_TILE = 1024
A_BLOCKS_PER_STEP = 4
B_Q_TILE = 512
C_Q_TILE = 256
MAX_LAG = 60.0
FFN_CHUNK = 256
VMEM_LIMIT = 58 * 1024 * 1024

NEG = -1e30
M_INIT = -1e29
BIG = 1e30
THR_ALL = -1e29
N_BISECT = 14


def _cparams(sem):
    return pltpu.CompilerParams(dimension_semantics=sem, vmem_limit_bytes=VMEM_LIMIT)


def _const_spec(shape):
    nd = len(shape)
    return pl.BlockSpec(shape, lambda *_: (0,) * nd)


def _weight_spec(shape):
    nd = len(shape)
    return pl.BlockSpec(shape, lambda *_: (0,) * nd, pipeline_mode=pl.Buffered(1))


def _rel_bucket_np(dist):
    n = np.maximum(dist, 0)
    max_exact = NUM_BUCKETS // 2
    nf = np.maximum(n, 1).astype(np.float64)
    large = max_exact + (np.log(nf / max_exact) / np.log(MAX_DISTANCE / max_exact)
                         * (NUM_BUCKETS - max_exact)).astype(np.int64)
    large = np.minimum(large, NUM_BUCKETS - 1)
    return np.where(n < max_exact, n, large)


def _far_delta():
    d = 1
    while not np.all(_rel_bucket_np(np.arange(d * LANES - LANES + 1, d * LANES + LANES)) == NUM_BUCKETS - 1):
        d += 1
    return d


FAR = _far_delta()
MASKED = FAR + 1


def _toeplitz(w, n_rows, n_cols):
    period = n_rows + n_cols
    w = jnp.pad(w, ((0, 0), (0, period - w.shape[1])))
    m = jnp.tile(w, (1, n_rows))[:, :n_rows * (period - 1)].reshape(-1, n_rows, period - 1)
    return m[:, :, n_rows - 1:n_rows - 1 + n_cols]


def _bias_by_distance(bias_heads, dist, valid):
    vals = jnp.take(bias_heads.astype(F32), jnp.asarray(_rel_bucket_np(dist), jnp.int32), axis=0).T
    return jnp.where(jnp.asarray(valid)[None], vals, NEG)


def _toeplitz_tables(bias_heads):
    n_cols = (FAR + 1) * LANES
    dist = np.arange(LANES - 1 + n_cols) - (LANES - 1)
    tiles = _toeplitz(_bias_by_distance(bias_heads, dist, dist >= 0), LANES, n_cols)
    tiles = tiles.reshape(-1, LANES, FAR + 1, LANES).transpose(0, 2, 1, 3)
    masked = jnp.full((tiles.shape[0], 1, LANES, LANES), NEG, F32)
    return jnp.concatenate([tiles, masked], axis=1)


def _band_tables(bias_heads, dilation):
    wn = LANES
    sub = np.arange(3 * wn - 1) - (wn - 1)
    w = _bias_by_distance(bias_heads, sub * dilation, (sub >= 0) & (sub <= wn))
    later = jnp.flip(_toeplitz(w, wn, 2 * wn), axis=(1, 2))
    first = jnp.where(jnp.asarray(np.arange(2 * wn) >= wn)[None, None], later, NEG)
    return jnp.stack([first, later])


def _ffn_kernel(x_ref, g_ref, wg_ref, wu_ref, wd_ref, o_ref, acc_ref):
    x = x_ref[...]
    ms = jnp.mean(x * x, axis=-1, keepdims=True)
    h = (x * lax.rsqrt(ms + RMS_EPS) * g_ref[...]).astype(BF16)
    acc_ref[...] = jnp.zeros_like(acc_ref)

    def body(c, carry):
        cols = pl.ds(pl.multiple_of(c * FFN_CHUNK, FFN_CHUNK), FFN_CHUNK)
        g = jnp.dot(h, wg_ref[:, cols], preferred_element_type=F32)
        u = jnp.dot(h, wu_ref[:, cols], preferred_element_type=F32)
        a = (g * jax.nn.sigmoid(g) * u).astype(BF16)
        acc_ref[...] += jnp.dot(a, wd_ref[cols, :], preferred_element_type=F32)
        return carry

    lax.fori_loop(0, wg_ref.shape[1] // FFN_CHUNK, body, 0)
    o_ref[...] = x + 0.5 * acc_ref[...]


def _ffn(x, gain, w_gate, w_up, w_down):
    n, d = x.shape
    f = w_gate.shape[1]
    tm = FFN_TILE
    return pl.pallas_call(
        _ffn_kernel,
        out_shape=jax.ShapeDtypeStruct((n, d), F32),
        grid=(n // tm,),
        in_specs=[pl.BlockSpec((tm, d), lambda i: (i, 0)),
                  _const_spec((1, d)), _weight_spec((d, f)), _weight_spec((d, f)), _weight_spec((f, d))],
        out_specs=pl.BlockSpec((tm, d), lambda i: (i, 0)),
        scratch_shapes=[pltpu.VMEM((tm, d), F32)],
        compiler_params=_cparams(("parallel",)),
        name="ffn",
    )(x, gain.reshape(1, d).astype(F32), w_gate.astype(BF16), w_up.astype(BF16), w_down.astype(BF16))


S_AQ, S_AK, S_AV = 0, 768, 1536
S_BK, S_CK, S_GATE, S_END = 2304, 2816, 2944, 6016
T_BQ, T_BV, T_CQ, T_CV, T_IQ, T_IW, T_END = 0, 512, 1024, 1280, 1344, 1856, 1872


def _proj_kernel(x_ref, g_ref, ws_ref, wt_ref, bd_ref, gs_ref, gt_ref, *refs):
    ng = len(DIL_GROUPS)
    a_refs = refs[:3 * ng]
    bk_ref, ck_ref, gate_ref, bqt_ref, bvt_ref, cqt_ref, cvt_ref, iqt_ref, iwt_ref = refs[3 * ng:-1]
    shuffle_ref = refs[-1]
    tm = x_ref.shape[0]

    def store_by_residue(y, which):
        for g, (_, dil) in enumerate(DIL_GROUPS):
            out = a_refs[which * ng + g]
            part = y[:, g * A_OUT:(g + 1) * A_OUT]
            if dil == 1:
                out[0] = part.astype(BF16)
            else:
                for half in range(A_OUT // LANES):
                    shuffle_ref[half] = part[:, half * LANES:(half + 1) * LANES]
                for r in range(dil):
                    out[r] = jnp.concatenate(
                        [shuffle_ref[half, pl.ds(r, tm // dil, stride=dil), :] for half in range(A_OUT // LANES)],
                        axis=1).astype(BF16)

    x = x_ref[...]
    ms = jnp.mean(x * x, axis=-1, keepdims=True)
    h = (x * lax.rsqrt(ms + RMS_EPS) * g_ref[...]).astype(BF16)
    bd = bd_ref[...]

    def dot_s(c0, c1):
        return jnp.dot(h, ws_ref[:, c0:c1], preferred_element_type=F32)

    def head_inv_rms(y):
        width = bd.shape[0]
        outs = []
        for c0 in range(0, y.shape[1], width):
            sq = y[:, c0:c0 + width]
            n = sq.shape[1]
            msq = jnp.dot((sq * sq).astype(BF16), bd[:n, :n], preferred_element_type=F32)
            outs.append(lax.rsqrt(msq + RMS_EPS))
        return outs[0] if len(outs) == 1 else jnp.concatenate(outs, axis=1)

    y = dot_s(S_AQ, S_AK)
    store_by_residue(y * head_inv_rms(y) * gs_ref[:, 0:768], 0)
    y = dot_s(S_AK, S_AV)
    store_by_residue(y * head_inv_rms(y) * gs_ref[:, 768:1536], 1)
    store_by_residue(dot_s(S_AV, S_BK), 2)
    y = dot_s(S_BK, S_CK)
    bk_ref[...] = (y * head_inv_rms(y) * gs_ref[:, 1536:2048]).astype(BF16)
    y = dot_s(S_CK, S_GATE)
    lane = lax.broadcasted_iota(jnp.int32, y.shape, 1)
    inv = jnp.where(lane < HEAD_DIM, head_inv_rms(y), 1.0)
    ck_ref[...] = (y * inv * gs_ref[:, 2048:2176]).astype(BF16)
    for c in range(3):
        y = dot_s(S_GATE + c * 1024, S_GATE + (c + 1) * 1024)
        gate_ref[:, c * 1024:(c + 1) * 1024] = jax.nn.sigmoid(y).astype(BF16)

    def dot_t(r0, r1):
        return lax.dot_general(wt_ref[r0:r1, :], h, (((1,), (1,)), ((), ())),
                               preferred_element_type=F32)

    def norm_t(y, gain):
        r = y.shape[0] // HEAD_DIM
        y3 = y.reshape(r, HEAD_DIM, tm)
        msq = jnp.mean(y3 * y3, axis=1, keepdims=True)
        return (y3 * lax.rsqrt(msq + RMS_EPS)).reshape(r * HEAD_DIM, tm) * gain

    bqt_ref[...] = norm_t(dot_t(T_BQ, T_BV), gt_ref[0:512, :]).astype(BF16)
    bvt_ref[...] = dot_t(T_BV, T_CQ).astype(BF16)
    cqt_ref[...] = norm_t(dot_t(T_CQ, T_CV), gt_ref[512:768, :]).astype(BF16)
    cvt_ref[...] = dot_t(T_CV, T_IQ).astype(BF16)
    iqt_ref[...] = dot_t(T_IQ, T_IW).astype(BF16)
    iwt_ref[...] = dot_t(T_IW, T_END) * (IDX_HEADS ** -0.5 * IDX_DIM ** -0.5)


def _proj_weights(w_in, qk_gain, tm):
    d = w_in.shape[0]
    o = 0
    a_qkv = w_in[:, o:o + 3 * A_HEADS * HEAD_DIM].reshape(d, 3, A_HEADS * HEAD_DIM)
    o += 3 * A_HEADS * HEAD_DIM
    b_qk = w_in[:, o:o + 4 * B_HEADS * HEAD_DIM].reshape(d, 4, B_HEADS, HEAD_DIM)
    o += 4 * B_HEADS * HEAD_DIM
    b_v = w_in[:, o:o + B_OUT]
    o += B_OUT
    c_q = w_in[:, o:o + C_OUT]
    c_k = w_in[:, o + C_OUT:o + C_OUT + HEAD_DIM]
    c_v = w_in[:, o + C_OUT + HEAD_DIM:o + C_OUT + 2 * HEAD_DIM]
    o += C_OUT + 2 * HEAD_DIM
    i_q = w_in[:, o:o + IDX_HEADS * IDX_DIM]
    i_k = w_in[:, o + IDX_HEADS * IDX_DIM:o + IDX_HEADS * IDX_DIM + IDX_DIM]
    i_w = w_in[:, o + IDX_HEADS * IDX_DIM + IDX_DIM:o + IDX_HEADS * IDX_DIM + IDX_DIM + IDX_HEADS]
    o += IDX_HEADS * IDX_DIM + IDX_DIM + IDX_HEADS
    gates = w_in[:, o:]
    b_k = jnp.stack([b_qk[:, 2], b_qk[:, 3]], axis=2).reshape(d, 2 * B_HEADS * HEAD_DIM)
    b_q = jnp.stack([b_qk[:, 0], b_qk[:, 1]], axis=2).reshape(d, 2 * B_HEADS * HEAD_DIM)
    w_s = jnp.concatenate([a_qkv[:, 0], a_qkv[:, 1], a_qkv[:, 2], b_k, c_k, i_k, gates], axis=1)
    w_t = jnp.concatenate([b_q, b_v, c_q, c_v, i_q, i_w, jnp.zeros((d, 8), w_in.dtype)], axis=1).T
    assert w_s.shape[1] == S_END and w_t.shape[0] == T_END
    scale = HEAD_DIM ** -0.5
    g = qk_gain.astype(F32)
    gs = jnp.concatenate([jnp.tile(g[0, 0] * scale, A_HEADS), jnp.tile(g[0, 1], A_HEADS),
                          jnp.tile(g[1, 1], 2 * B_HEADS), g[2, 1], jnp.ones((IDX_DIM,), F32)])[None]
    gt = jnp.concatenate([jnp.tile(g[1, 0] * (scale * LOG2E), 2 * B_HEADS),
                          jnp.tile(g[2, 0] * (scale * LOG2E), C_HEADS)])
    gt = jnp.broadcast_to(gt[:, None], (gt.shape[0], tm))
    return w_s.astype(BF16), w_t.astype(BF16), gs, gt


MXU_TILE = 256


def _head_block_diag():
    r = np.arange(MXU_TILE)
    return jnp.asarray((r[:, None] // HEAD_DIM == r[None, :] // HEAD_DIM) / HEAD_DIM, BF16)


def _project(x, gain, w_in, qk_gain, batch, t):
    n, d = x.shape
    tm = TOKEN_TILE
    nt = n // tm
    per_batch = t // tm
    w_s, w_t, gs, gt = _proj_weights(w_in, qk_gain, tm)
    tok = lambda c: pl.BlockSpec((tm, c), lambda i: (i, 0))
    feat = lambda r: pl.BlockSpec((None, r, tm), lambda i: (i, 0, 0))
    a_shapes, a_specs = [], []
    for _ in range(3):
        for _, dil in DIL_GROUPS:
            a_shapes.append(jax.ShapeDtypeStruct((batch, dil, t // dil, A_OUT), BF16))
            a_specs.append(pl.BlockSpec((None, dil, tm // dil, A_OUT),
                                        lambda i: (i // per_batch, 0, i % per_batch, 0)))
    out_shape = a_shapes + [
        jax.ShapeDtypeStruct((n, 512), BF16), jax.ShapeDtypeStruct((n, 128), BF16),
        jax.ShapeDtypeStruct((n, 3072), BF16),
        jax.ShapeDtypeStruct((nt, 512, tm), BF16), jax.ShapeDtypeStruct((nt, 512, tm), BF16),
        jax.ShapeDtypeStruct((nt, 256, tm), BF16), jax.ShapeDtypeStruct((nt, 64, tm), BF16),
        jax.ShapeDtypeStruct((nt, 512, tm), BF16), jax.ShapeDtypeStruct((nt, 16, tm), F32)]
    out_specs = a_specs + [tok(512), tok(128), tok(3072),
                           feat(512), feat(512), feat(256), feat(64), feat(512), feat(16)]
    return pl.pallas_call(
        _proj_kernel,
        out_shape=out_shape,
        grid=(nt,),
        in_specs=[tok(d), _const_spec((1, d)), _weight_spec(w_s.shape), _weight_spec(w_t.shape),
                  _const_spec((MXU_TILE, MXU_TILE)), _const_spec(gs.shape), _const_spec(gt.shape)],
        out_specs=out_specs,
        scratch_shapes=[pltpu.VMEM((A_OUT // LANES, tm, LANES), F32)],
        compiler_params=_cparams(("parallel",)),
        name="proj",
    )(x, gain.reshape(1, d).astype(F32), w_s, w_t, _head_block_diag(), gs, gt)


def _dil_kernel(q_ref, kp_ref, kc_ref, vp_ref, vc_ref, bias_ref, o_ref, lse_ref):
    nq = q_ref.shape[0] // LANES
    qi = pl.program_id(2)
    lane = lax.broadcasted_iota(jnp.int32, (LANES, A_OUT), 1) // HEAD_DIM
    mine = [lane == h for h in range(A_GROUP_HEADS)]
    blocks = [slice(jb * LANES, (jb + 1) * LANES) for jb in range(nq)]

    def band(prev_ref, cur_ref, jb):
        if jb == 0:
            return jnp.concatenate([prev_ref[...], cur_ref[blocks[0], :]], axis=0)
        return cur_ref[(jb - 1) * LANES:(jb + 1) * LANES, :]

    logits = []
    for jb in range(nq):
        q = q_ref[blocks[jb], :]
        q4 = jnp.concatenate([jnp.where(mine[h], q, jnp.zeros_like(q)) for h in range(A_GROUP_HEADS)], axis=0)
        s = lax.dot_general(q4, band(kp_ref, kc_ref, jb), (((1,), (1,)), ((), ())), preferred_element_type=F32)
        bias = bias_ref[jnp.minimum(qi, 1) if jb == 0 else 1]
        logits.append(s + bias.reshape(A_GROUP_HEADS * LANES, 2 * LANES))
    probs, stats = [], []
    for s in logits:
        m = jnp.max(s, axis=1, keepdims=True)
        p = jnp.exp(s - m)
        ssum = jnp.sum(p, axis=1, keepdims=True)
        probs.append(p.astype(BF16))
        stats.append((1.0 / ssum, m + jnp.log(ssum)))
    for jb in range(nq):
        pv = jnp.dot(probs[jb], band(vp_ref, vc_ref, jb), preferred_element_type=F32)
        inv, lse4 = stats[jb]
        o = jnp.zeros((LANES, A_OUT), F32)
        lse = jnp.zeros((LANES, A_OUT), F32)
        for h in range(A_GROUP_HEADS):
            head = slice(h * LANES, (h + 1) * LANES)
            o = jnp.where(mine[h], pv[head] * inv[head], o)
            lse = jnp.where(mine[h], lse4[head], lse)
        o_ref[blocks[jb], :] = o.astype(BF16)
        lse_ref[blocks[jb], :] = lse


def _dilated_group(aq, ak, av, bias, dilation):
    batch, _, n, _ = aq.shape
    nblk = n // LANES
    nq = min(nblk, A_BLOCKS_PER_STEP)
    qt = nq * LANES
    cur = pl.BlockSpec((None, None, qt, A_OUT), lambda b, r, i: (b, r, i, 0))
    prev = pl.BlockSpec((None, None, LANES, A_OUT), lambda b, r, i: (b, r, jnp.maximum(i * nq - 1, 0), 0))
    shp = (batch, dilation, n, A_OUT)
    return pl.pallas_call(
        _dil_kernel,
        out_shape=[jax.ShapeDtypeStruct(shp, BF16), jax.ShapeDtypeStruct(shp, F32)],
        grid=(batch, dilation, nblk // nq),
        in_specs=[cur, prev, cur, prev, cur, _const_spec(bias.shape)],
        out_specs=[cur, cur],
        compiler_params=_cparams(("parallel", "parallel", "parallel")),
        name=f"dilated_d{dilation}",
    )(aq, ak, ak, av, av, bias)


def _fold_rows(x, op):
    r, c = x.shape
    return op(x.reshape(r // 64, 64, c), axis=0) if r > 64 else x


def _reduce_rows(x, op):
    x = _fold_rows(x, op)
    x = op(x.reshape(8, 8, x.shape[1]), axis=0)
    return op(x, axis=0, keepdims=True)


def _bias_tile(tab_ref, head, qblk, kblk):
    delta = qblk - kblk
    idx = jnp.where(delta < 0, MASKED, jnp.minimum(delta, FAR))
    if head is None:
        return tab_ref[idx]
    return tab_ref[head, idx]


def _diff_kernel(qt_ref, k_ref, vt_ref, tab_ref, lam_ref, gn_ref, o_ref):
    tq = qt_ref.shape[1]
    tk = TOKEN_TILE
    qi = pl.program_id(2)
    qt = qt_ref[...]
    row = lax.broadcasted_iota(jnp.int32, qt.shape, 0)
    q12 = jnp.concatenate([jnp.where(row < HEAD_DIM, qt, jnp.zeros_like(qt)),
                           jnp.where(row >= HEAD_DIM, qt, jnp.zeros_like(qt))], axis=1)
    nqb = tq // LANES
    nkb = tk // LANES

    def logits(c):
        kc = k_ref[pl.ds(pl.multiple_of(c * tk, tk), tk), :]
        bias = jnp.concatenate(
            [jnp.concatenate([_bias_tile(tab_ref, None, qi * nqb + iq, c * nkb + jk)
                              for iq in range(nqb)] * 2, axis=1) for jk in range(nkb)], axis=0)
        return jnp.dot(kc, q12, preferred_element_type=F32) + bias

    ones_rows = jnp.ones((ONES_ROWS, tk), BF16)

    def weigh(c, p):
        return jnp.dot(jnp.concatenate([vt_ref[c], ones_rows], axis=0), p.astype(BF16), preferred_element_type=F32)

    def exact_step(c, carry):
        m, acc = carry
        s = logits(c)
        m_new = jnp.maximum(m, _reduce_rows(s, jnp.max))
        alpha = jnp.exp2(m - m_new)
        return m_new, alpha * acc + weigh(c, jnp.exp2(s - m_new))

    def lagged_update(c, s, m, acc, jump):
        p = jnp.exp2(s - m)
        top =_reduce_rows(s, jnp.max)
        acc = acc + weigh(c, p)
        m_new = jnp.maximum(m, top)
        alpha = jnp.exp2(m - m_new)
        return m_new, alpha * acc, jnp.maximum(jump, top - m)

    nch = ((qi + 1) * tq + tk - 1) // tk
    zero = jnp.zeros((1, 2 * tq), F32)
    acc0 = jnp.zeros((B_V_DIM + ONES_ROWS, 2 * tq), F32)

    def opening(count):
        ss = [logits(c) for c in range(count)]
        state = (jnp.max(ss[0][0:SUBLANES], axis=0, keepdims=True), acc0, zero)
        for c in range(count):
            state = lagged_update(c, ss[c], *state)
        return state

    def lagged_pair(c, carry):
        s_a, s_b = logits(c), logits(c + 1)
        return lagged_update(c + 1, s_b, *lagged_update(c, s_a, *carry))

    odd = nch % 2
    state = lax.cond(odd == 1, lambda: opening(1), lambda: opening(2))
    _, acc, jump = lax.fori_loop(0, (nch - 2 + odd) // 2,
                                 lambda i, carry: lagged_pair(2 - odd + 2 * i, carry), state)
    acc = lax.cond(jnp.max(jump) > MAX_LAG,
                   lambda: lax.fori_loop(0, nch, exact_step, (jnp.full((1, 2 * tq), M_INIT, F32), acc0))[1],
                   lambda: acc)
    a1, a2 = acc[:B_V_DIM, :tq], acc[:B_V_DIM, tq:]
    l1, l2 = acc[B_V_DIM:B_V_DIM + 1, :tq], acc[B_V_DIM:B_V_DIM + 1, tq:]

    lv = lam_ref[...]
    lam = (jnp.exp(jnp.sum(lv[0:1] * lv[1:2], axis=1, keepdims=True))
           - jnp.exp(jnp.sum(lv[2:3] * lv[3:4], axis=1, keepdims=True)) + lv[4:5, 0:1])
    o = a1 / l1 - lam * (a2 / l2)
    ms = jnp.mean(o * o, axis=0, keepdims=True)
    o = o * lax.rsqrt(ms + RMS_EPS) * gn_ref[...]
    o_ref[...] = o.T.astype(BF16)


def _diff_attention(bqt, bk, bvt, tab, lam_rows, gn, batch, t):
    tq = B_Q_TILE
    tk = TOKEN_TILE
    per = tk // tq
    nkt = t // tk
    bqt = bqt.reshape(batch, nkt, B_HEADS * LANES, tk)
    bvt = bvt.reshape(batch, nkt, B_OUT, tk)
    bk = bk.reshape(batch, t, B_HEADS * LANES)
    return pl.pallas_call(
        _diff_kernel,
        out_shape=jax.ShapeDtypeStruct((batch, t, B_OUT), BF16),
        grid=(batch, B_HEADS, t // tq),
        in_specs=[pl.BlockSpec((None, None, LANES, tq), lambda b, h, i: (b, i // per, h, i % per)),
                  pl.BlockSpec((None, t, LANES), lambda b, h, i: (b, 0, h)),
                  pl.BlockSpec((None, nkt, B_V_DIM, tk), lambda b, h, i: (b, 0, h, 0)),
                  pl.BlockSpec((None, FAR + 2, LANES, LANES), lambda b, h, i: (h, 0, 0, 0)),
                  _const_spec(lam_rows.shape), _const_spec(gn.shape)],
        out_specs=pl.BlockSpec((None, tq, B_V_DIM), lambda b, h, i: (b, i, h)),
        compiler_params=_cparams(("parallel", "parallel", "arbitrary")),
        name="diff_attention",
    )(bqt, bk, bvt, tab, lam_rows, gn).reshape(batch * t, B_OUT)


def _dsa_kernel(iqt_ref, iwt_ref, cqt_ref, k_ref, vt_ref, tab_ref, tri_ref, o_ref, s_ref, *, k_sel):
    tk = TOKEN_TILE
    nkb = tk // LANES
    qw = o_ref.shape[0]
    nqb = qw // LANES
    qi = pl.program_id(1)
    nch = ((qi + 1) * qw + tk - 1) // tk
    qpos = qi * qw + lax.broadcasted_iota(jnp.int32, (1, qw), 1)
    zeros = jnp.zeros((HEAD_DIM, qw), BF16)
    iq = iqt_ref[...]
    w = iwt_ref[...]
    iq_all = jnp.concatenate([jnp.concatenate([zeros, iq[h * IDX_DIM:(h + 1) * IDX_DIM]], axis=0)
                              for h in range(IDX_HEADS)], axis=1)
    cq = cqt_ref[...]
    cq_all = jnp.concatenate([jnp.concatenate([cq[h * HEAD_DIM:(h + 1) * HEAD_DIM], zeros], axis=0)
                              for h in range(C_HEADS)], axis=1)

    def chunk(c):
        return pl.ds(pl.multiple_of(c * tk, tk), tk)

    def raw_scores(c):
        return jnp.dot(k_ref[chunk(c), :], iq_all, preferred_element_type=F32)

    def score_chunk(c, raw, mn, mx, last):
        acc = w[0:1, :] * jnp.maximum(raw[:, 0:qw], 0.0)
        for h in range(1, IDX_HEADS):
            acc = acc + w[h:h + 1, :] * jnp.maximum(raw[:, h * qw:(h + 1) * qw], 0.0)
        if last:
            kpos = c * tk + lax.broadcasted_iota(jnp.int32, (tk, qw), 0)
            causal = kpos <= qpos
            s_ref[chunk(c), :] = jnp.where(causal, acc, NEG)
            mn = jnp.minimum(mn, _fold_rows(jnp.where(causal, acc, BIG), jnp.min))
            mx = jnp.maximum(mx, _fold_rows(jnp.where(causal, acc, NEG), jnp.max))
        else:
            s_ref[chunk(c), :] = acc
            mn = jnp.minimum(mn, _fold_rows(acc, jnp.min))
            mx = jnp.maximum(mx, _fold_rows(acc, jnp.max))
        return mn, mx

    def score_pair(c, carry, last):
        raw_a, raw_b = raw_scores(c), raw_scores(c + 1)
        return score_chunk(c + 1, raw_b, *score_chunk(c, raw_a, *carry, last=False), last=last)

    carry = lax.fori_loop(0, (nch - 1) // 2, lambda i, carry: score_pair(2 * i, carry, last=False),
                          (jnp.full((64, qw), BIG, F32), jnp.full((64, qw), NEG, F32)))
    mn, mx = lax.cond((nch - 1) % 2 == 1,
                      lambda st: score_pair(nch - 2, st, last=True),
                      lambda st: score_chunk(nch - 1, raw_scores(nch - 1), *st, last=True), carry)
    lo, hi = _reduce_rows(mn, jnp.min), _reduce_rows(mx, jnp.max)

    def count_gt(thr):
        def body(c, gt):
            return gt + _fold_rows(jnp.where(s_ref[chunk(c), :] > thr, 1.0, 0.0), jnp.sum)
        return _reduce_rows(lax.fori_loop(0, nch, body, jnp.zeros((64, qw), F32)), jnp.sum)

    def largest_upto(bound):
        def body(c, mx):
            s = s_ref[chunk(c), :]
            return jnp.maximum(mx, _fold_rows(jnp.where(s <= bound, s, NEG), jnp.max))
        return _reduce_rows(lax.fori_loop(0, nch, body, jnp.full((64, qw), NEG, F32)), jnp.max)

    def next_below_and_multiplicity(cand):
        def body(c, carry):
            mx, eq = carry
            s = s_ref[chunk(c), :]
            mx = jnp.maximum(mx, _fold_rows(jnp.where(s < cand, s, NEG), jnp.max))
            eq = eq + _fold_rows(jnp.where(s == cand, 1.0, 0.0), jnp.sum)
            return mx, eq
        mx, eq = lax.fori_loop(0, nch, body, (jnp.full((64, qw), NEG, F32), jnp.zeros((64, qw), F32)))
        return _reduce_rows(mx, jnp.max), _reduce_rows(eq, jnp.sum)

    kf = float(k_sel)
    need = qpos >= k_sel

    def bisect(_, carry):
        lo, hi, above = carry
        mid = lo + (hi - lo) * 0.5
        cnt = count_gt(mid)
        below = cnt < kf
        return jnp.where(below, lo, mid), jnp.where(below, mid, hi), jnp.where(below, cnt, above)

    _, hi, above = lax.fori_loop(0, N_BISECT, bisect, (lo - 1.0, hi, jnp.zeros((1, qw), F32)))

    def walk_cond(carry):
        _, _, ge = carry
        return jnp.max(jnp.where(need & (ge < kf), 1.0, 0.0)) > 0.0

    def walk_body(carry):
        cand, gt, ge = carry
        active = ge < kf
        nxt, mult = next_below_and_multiplicity(cand)
        ge_new = gt + mult
        moved = active & (ge_new < kf)
        return jnp.where(moved, nxt, cand), jnp.where(moved, ge_new, gt), jnp.where(active, ge_new, ge)

    cand, gt, ge = lax.while_loop(walk_cond, walk_body, (largest_upto(hi), above, above))
    thr = jnp.where(need, cand, THR_ALL)
    want_eq = jnp.where(need, kf - gt, 0.0)

    any_tie = jnp.max(jnp.where(need & (ge > kf), 1.0, 0.0)) > 0.0

    def mark_with_ties(c, eq_seen):
        s = s_ref[chunk(c), :]
        eq = jnp.where(s == thr, 1.0, 0.0)
        rank = eq_seen + jnp.dot(tri_ref[...], eq.astype(BF16), preferred_element_type=F32)
        keep = jnp.where(s > thr, 1.0, jnp.where(rank <= want_eq, eq, 0.0))
        s_ref[chunk(c), :] = jnp.where(keep > 0.5, 0.0, NEG)
        return eq_seen + _reduce_rows(eq, jnp.sum)

    def mark_no_ties(c, carry):
        s_ref[chunk(c), :] = jnp.where(s_ref[chunk(c), :] >= thr, 0.0, NEG)
        return carry

    @pl.when(any_tie)
    def _():
        lax.fori_loop(0, nch, mark_with_ties, jnp.zeros((1, qw), F32))

    @pl.when(jnp.logical_not(any_tie))
    def _():
        lax.fori_loop(0, nch, mark_no_ties, 0)

    def masked_logits(c):
        sel = s_ref[chunk(c), :]
        bias = jnp.concatenate(
            [jnp.concatenate([jnp.concatenate([_bias_tile(tab_ref, h, qi * nqb + iq, c * nkb + jk)
                                               for iq in range(nqb)], axis=1) for jk in range(nkb)], axis=0) + sel
             for h in range(C_HEADS)], axis=1)
        return jnp.dot(k_ref[chunk(c), :], cq_all, preferred_element_type=F32) + bias

    ones_rows = jnp.ones((ONES_ROWS, tk), BF16)

    def weigh(c, p):
        return jnp.dot(jnp.concatenate([vt_ref[c], ones_rows], axis=0), p.astype(BF16), preferred_element_type=F32)

    def exact_update(c, lg, m, acc):
        m_new = jnp.maximum(m, _reduce_rows(lg, jnp.max))
        alpha = jnp.exp2(m - m_new)
        return m_new, alpha * acc + weigh(c, jnp.exp2(lg - m_new))

    def lagged_update(c, lg, m, acc, jump):
        p = jnp.exp2(lg - m)
        top = _reduce_rows(lg, jnp.max)
        acc = acc + weigh(c, p)
        m_new = jnp.maximum(m, top)
        alpha = jnp.exp2(m - m_new)
        return m_new, alpha * acc, jnp.maximum(jump, top - m)

    wide = C_HEADS * qw
    init = (jnp.full((1, wide), M_INIT, F32), jnp.zeros((HEAD_DIM + ONES_ROWS, wide), F32))

    def opening(count):
        lgs = [masked_logits(c) for c in range(count)]
        state = exact_update(0, lgs[0], *init) + (jnp.zeros((1, wide), F32),)
        return lagged_update(1, lgs[1], *state) if count == 2 else state

    def lagged_pair(c, carry):
        lg_a, lg_b = masked_logits(c), masked_logits(c + 1)
        return lagged_update(c + 1, lg_b, *lagged_update(c, lg_a, *carry))

    odd = nch % 2
    state = lax.cond(odd == 1, lambda: opening(1), lambda: opening(2))
    _, acc, jump = lax.fori_loop(0, (nch - 2 + odd) // 2,
                                 lambda i, carry: lagged_pair(2 - odd + 2 * i, carry), state)
    acc = lax.cond(jnp.max(jump) > MAX_LAG,
                   lambda: lax.fori_loop(0, nch, lambda c, st: exact_update(c, masked_logits(c), *st), init)[1],
                   lambda: acc)
    o = acc[:HEAD_DIM] / acc[HEAD_DIM:HEAD_DIM + 1]
    ot = jnp.concatenate([o[:, h * qw:(h + 1) * qw] for h in range(C_HEADS)], axis=0)
    o_ref[...] = ot.T.astype(BF16)


def _dsa_attention(iqt, iwt, cqt, ck, cvt, tab, batch, t):
    tk = TOKEN_TILE
    qw = C_Q_TILE
    per = tk // qw
    nkt = t // tk
    k_sel = min(TOPK_MAX, t // 4)
    iqt = iqt.reshape(batch, nkt, IDX_HEADS * IDX_DIM, tk)
    iwt = iwt.reshape(batch, nkt, 16, tk)
    cqt = cqt.reshape(batch, nkt, C_OUT, tk)
    cvt = cvt.reshape(batch, nkt, HEAD_DIM, tk)
    ck = ck.reshape(batch, t, LANES)
    r = np.arange(tk)
    tri = jnp.asarray(r[:, None] >= r[None, :], BF16)
    qblock = lambda rows: pl.BlockSpec((None, None, rows, qw), lambda b, i: (b, i // per, 0, i % per))
    return pl.pallas_call(
        functools.partial(_dsa_kernel, k_sel=k_sel),
        out_shape=jax.ShapeDtypeStruct((batch, t, C_OUT), BF16),
        grid=(batch, t // qw),
        in_specs=[qblock(IDX_HEADS * IDX_DIM), qblock(16), qblock(C_OUT),
                  pl.BlockSpec((None, t, LANES), lambda b, i: (b, 0, 0)),
                  pl.BlockSpec((None, nkt, HEAD_DIM, tk), lambda b, i: (b, 0, 0, 0)),
                  _const_spec(tab.shape), _const_spec(tri.shape)],
        out_specs=pl.BlockSpec((None, qw, C_OUT), lambda b, i: (b, i, 0)),
        scratch_shapes=[pltpu.VMEM((t, qw), F32)],
        compiler_params=_cparams(("parallel", "arbitrary")),
        name="dsa_attention",
    )(iqt, iwt, cqt, ck, cvt, tab, tri).reshape(batch * t, C_OUT)


def _merge_kernel(x_ref, *refs):
    ng = len(DIL_GROUPS)
    a_refs = refs[:2 * ng]
    ob_ref, oc_ref, gate_ref, wa_ref, wb_ref, wc_ref, wo_ref, out_ref = refs[2 * ng:-1]
    shuffle_ref = refs[-1]
    tm, d = x_ref.shape

    def token_order(ref, slot):
        dil = ref.shape[0]
        if dil == 1:
            return ref[0].astype(F32)
        halves = range(A_OUT // LANES)
        for r in range(dil):
            for half in halves:
                shuffle_ref[slot, half, pl.ds(r, tm // dil, stride=dil), :] = (
                    ref[r, :, half * LANES:(half + 1) * LANES].astype(F32))
        return jnp.concatenate([shuffle_ref[slot, half] for half in halves], axis=1)

    outs = [token_order(a_refs[2 * g], 2 * g) for g in range(ng)]
    lses = [token_order(a_refs[2 * g + 1], 2 * g + 1) for g in range(ng)]
    top = functools.reduce(jnp.maximum, lses)
    es = [jnp.exp(lse - top) for lse in lses]
    num = sum(e * o for e, o in zip(es, outs))
    oa = (num / sum(es)).astype(BF16)
    y = gate_ref[:, 0:d].astype(F32) * jnp.dot(oa, wa_ref[...], preferred_element_type=F32)
    y = y + gate_ref[:, d:2 * d].astype(F32) * jnp.dot(ob_ref[...], wb_ref[...], preferred_element_type=F32)
    y = y + gate_ref[:, 2 * d:3 * d].astype(F32) * jnp.dot(oc_ref[...], wc_ref[...], preferred_element_type=F32)
    out_ref[...] = x_ref[...] + jnp.dot(y.astype(BF16), wo_ref[...], preferred_element_type=F32)


def _merge(x, a_parts, ob, oc, gates, wa, wb, wc, wo, t):
    n, d = x.shape
    tm = TOKEN_TILE
    per_batch = t // tm
    tok = lambda c: pl.BlockSpec((tm, c), lambda i: (i, 0))
    by_residue = lambda dil: pl.BlockSpec((None, dil, tm // dil, A_OUT),
                                          lambda i: (i // per_batch, 0, i % per_batch, 0))
    ws = [w.astype(BF16) for w in (wa, wb, wc, wo)]
    return pl.pallas_call(
        _merge_kernel,
        out_shape=jax.ShapeDtypeStruct((n, d), F32),
        grid=(n // tm,),
        in_specs=[tok(d)] + [by_residue(z.shape[1]) for z in a_parts] + [tok(B_OUT), tok(C_OUT), tok(3 * d)]
                 + [_weight_spec(w.shape) for w in ws],
        out_specs=tok(d),
        scratch_shapes=[pltpu.VMEM((len(a_parts), A_OUT // LANES, tm, LANES), F32)],
        compiler_params=_cparams(("parallel",)),
        name="merge",
    )(x, *a_parts, ob, oc, gates, *ws)


def _token_mixer(x, batch, t, layer, mix_norm, w_in, qk_gain, diff_lambda, diff_out_norm,
                 w_branch_a, w_branch_b, w_branch_c, w_out, band_tabs, tab_b, tab_c):
    ng = len(DIL_GROUPS)
    outs = _project(x, mix_norm, w_in, qk_gain, batch, t)
    a_in, (bk, ck, gates, bqt, bvt, cqt, cvt, iqt, iwt) = outs[:3 * ng], outs[3 * ng:]
    a_parts = []
    for g, (_, dilation) in enumerate(DIL_GROUPS):
        a_parts += _dilated_group(a_in[g], a_in[ng + g], a_in[2 * ng + g], band_tabs[g], dilation)
    lam_init = 0.8 - 0.6 * np.exp(-0.3 * layer)
    lam_rows = jnp.concatenate([diff_lambda.astype(F32), jnp.full((4, HEAD_DIM), lam_init, F32)], axis=0)
    gn = jnp.broadcast_to((diff_out_norm.astype(F32) * (1.0 - lam_init))[:, None], (B_V_DIM, B_Q_TILE))
    ob = _diff_attention(bqt, bk, bvt, tab_b, lam_rows, gn, batch, t)
    oc = _dsa_attention(iqt, iwt, cqt, ck, cvt, tab_c, batch, t)
    return _merge(x, a_parts, ob, oc, gates, w_branch_a, w_branch_b, w_branch_c, w_out, t)


def kernel(x, rel_bias, ffn1_norm, ffn1_w_gate, ffn1_w_up, ffn1_w_down, mix_norm, w_in, qk_gain,
           diff_lambda, diff_out_norm, w_branch_a, w_branch_b, w_branch_c, w_out,
           ffn2_norm, ffn2_w_gate, ffn2_w_up, ffn2_w_down):
    batch, t, d = x.shape
    depth = w_in.shape[0]
    assert t % (DIL_GROUPS[-1][1] * LANES) == 0 and t % TOKEN_TILE == 0
    band_tabs = [_band_tables(rel_bias[:, g * A_GROUP_HEADS:(g + 1) * A_GROUP_HEADS], dil)
                 for g, (_, dil) in enumerate(DIL_GROUPS)]
    tab_b = _toeplitz_tables(rel_bias[:, A_HEADS:A_HEADS + B_HEADS] * LOG2E)
    tab_c = _toeplitz_tables(rel_bias[:, A_HEADS + B_HEADS:] * LOG2E)
    h = x.reshape(batch * t, d).astype(F32)
    for i in range(depth):
        h = _ffn(h, ffn1_norm[i], ffn1_w_gate[i], ffn1_w_up[i], ffn1_w_down[i])
        h = _token_mixer(h, batch, t, i, mix_norm[i], w_in[i], qk_gain[i], diff_lambda[i], diff_out_norm[i],
                         w_branch_a[i], w_branch_b[i], w_branch_c[i], w_out[i], band_tabs, tab_b, tab_c)
        h = _ffn(h, ffn2_norm[i], ffn2_w_gate[i], ffn2_w_up[i], ffn2_w_down[i])
    return h.reshape(batch, t, d).astype(x.dtype)
```

```python
import functools

import numpy as np
import jax
import jax.numpy as jnp
from jax import lax
from jax.experimental import pallas as pl
from jax.experimental.pallas import tpu as pltpu

F32 = jnp.float32
BF16 = jnp.bfloat16

HEAD_DIM = 64
DIL_GROUPS = ((128, 1), (512, 4), (2048, 16))
A_GROUP_HEADS = 4
A_HEADS = A_GROUP_HEADS * len(DIL_GROUPS)
A_OUT = A_GROUP_HEADS * HEAD_DIM
B_HEADS = 4
B_V_DIM = 2 * HEAD_DIM
B_OUT = B_HEADS * B_V_DIM
C_HEADS = 4
C_OUT = C_HEADS * HEAD_DIM
IDX_HEADS = 8
IDX_DIM = 64
TOPK_MAX = 256
NUM_BUCKETS = 32
MAX_DISTANCE = 2048
RMS_EPS = 1e-6
LOG2E = 1.4426950408889634

LANES = 128
SUBLANES = 8
ONES_ROWS = 16
TOKEN_TILE = 512
FFN_TILE = 1024
A_BLOCKS_PER_STEP = 4
B_Q_TILE = 512
C_Q_TILE = 256
MAX_LAG = 60.0
FFN_CHUNK = 256
VMEM_LIMIT = 58 * 1024 * 1024

NEG = -1e30
M_INIT = -1e29
BIG = 1e30
THR_ALL = -1e29
N_BISECT = 14


def _cparams(sem):
    return pltpu.CompilerParams(dimension_semantics=sem, vmem_limit_bytes=VMEM_LIMIT)


def _const_spec(shape):
    nd = len(shape)
    return pl.BlockSpec(shape, lambda *_: (0,) * nd)


def _weight_spec(shape):
    nd = len(shape)
    return pl.BlockSpec(shape, lambda *_: (0,) * nd, pipeline_mode=pl.Buffered(1))


def _rel_bucket_np(dist):
    n = np.maximum(dist, 0)
    max_exact = NUM_BUCKETS // 2
    nf = np.maximum(n, 1).astype(np.float64)
    large = max_exact + (np.log(nf / max_exact) / np.log(MAX_DISTANCE / max_exact)
                         * (NUM_BUCKETS - max_exact)).astype(np.int64)
    large = np.minimum(large, NUM_BUCKETS - 1)
    return np.where(n < max_exact, n, large)


def _far_delta():
    d = 1
    while not np.all(_rel_bucket_np(np.arange(d * LANES - LANES + 1, d * LANES + LANES)) == NUM_BUCKETS - 1):
        d += 1
    return d


FAR = _far_delta()
MASKED = FAR + 1


def _toeplitz(w, n_rows, n_cols):
    period = n_rows + n_cols
    w = jnp.pad(w, ((0, 0), (0, period - w.shape[1])))
    m = jnp.tile(w, (1, n_rows))[:, :n_rows * (period - 1)].reshape(-1, n_rows, period - 1)
    return m[:, :, n_rows - 1:n_rows - 1 + n_cols]


def _bias_by_distance(bias_heads, dist, valid):
    vals = jnp.take(bias_heads.astype(F32), jnp.asarray(_rel_bucket_np(dist), jnp.int32), axis=0).T
    return jnp.where(jnp.asarray(valid)[None], vals, NEG)


def _toeplitz_tables(bias_heads):
    n_cols = (FAR + 1) * LANES
    dist = np.arange(LANES - 1 + n_cols) - (LANES - 1)
    tiles = _toeplitz(_bias_by_distance(bias_heads, dist, dist >= 0), LANES, n_cols)
    tiles = tiles.reshape(-1, LANES, FAR + 1, LANES).transpose(0, 2, 1, 3)
    masked = jnp.full((tiles.shape[0], 1, LANES, LANES), NEG, F32)
    return jnp.concatenate([tiles, masked], axis=1)


def _band_tables(bias_heads, dilation):
    wn = LANES
    sub = np.arange(3 * wn - 1) - (wn - 1)
    w = _bias_by_distance(bias_heads, sub * dilation, (sub >= 0) & (sub <= wn))
    later = jnp.flip(_toeplitz(w, wn, 2 * wn), axis=(1, 2))
    first = jnp.where(jnp.asarray(np.arange(2 * wn) >= wn)[None, None], later, NEG)
    return jnp.stack([first, later])


def _ffn_kernel(x_ref, g_ref, wg_ref, wu_ref, wd_ref, o_ref, acc_ref):
    x = x_ref[...]
    ms = jnp.mean(x * x, axis=-1, keepdims=True)
    h = (x * lax.rsqrt(ms + RMS_EPS) * g_ref[...]).astype(BF16)
    acc_ref[...] = jnp.zeros_like(acc_ref)

    def body(c, carry):
        cols = pl.ds(pl.multiple_of(c * FFN_CHUNK, FFN_CHUNK), FFN_CHUNK)
        g = jnp.dot(h, wg_ref[:, cols], preferred_element_type=F32)
        u = jnp.dot(h, wu_ref[:, cols], preferred_element_type=F32)
        a = (g * jax.nn.sigmoid(g) * u).astype(BF16)
        acc_ref[...] += jnp.dot(a, wd_ref[cols, :], preferred_element_type=F32)
        return carry

    lax.fori_loop(0, wg_ref.shape[1] // FFN_CHUNK, body, 0)
    o_ref[...] = x + 0.5 * acc_ref[...]


def _ffn(x, gain, w_gate, w_up, w_down):
    n, d = x.shape
    f = w_gate.shape[1]
    tm = FFN_TILE
    return pl.pallas_call(
        _ffn_kernel,
        out_shape=jax.ShapeDtypeStruct((n, d), F32),
        grid=(n // tm,),
        in_specs=[pl.BlockSpec((tm, d), lambda i: (i, 0)),
                  _const_spec((1, d)), _weight_spec((d, f)), _weight_spec((d, f)), _weight_spec((f, d))],
        out_specs=pl.BlockSpec((tm, d), lambda i: (i, 0)),
        scratch_shapes=[pltpu.VMEM((tm, d), F32)],
        compiler_params=_cparams(("parallel",)),
        name="ffn",
    )(x, gain.reshape(1, d).astype(F32), w_gate.astype(BF16), w_up.astype(BF16), w_down.astype(BF16))


S_AQ, S_AK, S_AV = 0, 768, 1536
S_BK, S_CK, S_GATE, S_END = 2304, 2816, 2944, 6016
T_BQ, T_BV, T_CQ, T_CV, T_IQ, T_IW, T_END = 0, 512, 1024, 1280, 1344, 1856, 1872


def _proj_kernel(x_ref, g_ref, ws_ref, wt_ref, bd_ref, gs_ref, gt_ref, *refs):
    ng = len(DIL_GROUPS)
    a_refs = refs[:3 * ng]
    bk_ref, ck_ref, gate_ref, bqt_ref, bvt_ref, cqt_ref, cvt_ref, iqt_ref, iwt_ref = refs[3 * ng:-1]
    shuffle_ref = refs[-1]
    tm = x_ref.shape[0]

    def store_by_residue(y, which):
        for g, (_, dil) in enumerate(DIL_GROUPS):
            out = a_refs[which * ng + g]
            part = y[:, g * A_OUT:(g + 1) * A_OUT]
            if dil == 1:
                out[0] = part.astype(BF16)
            else:
                for half in range(A_OUT // LANES):
                    shuffle_ref[half] = part[:, half * LANES:(half + 1) * LANES]
                for r in range(dil):
                    out[r] = jnp.concatenate(
                        [shuffle_ref[half, pl.ds(r, tm // dil, stride=dil), :] for half in range(A_OUT // LANES)],
                        axis=1).astype(BF16)

    x = x_ref[...]
    ms = jnp.mean(x * x, axis=-1, keepdims=True)
    h = (x * lax.rsqrt(ms + RMS_EPS) * g_ref[...]).astype(BF16)
    bd = bd_ref[...]

    def dot_s(c0, c1):
        return jnp.dot(h, ws_ref[:, c0:c1], preferred_element_type=F32)

    def head_inv_rms(y):
        width = bd.shape[0]
        outs = []
        for c0 in range(0, y.shape[1], width):
            sq = y[:, c0:c0 + width]
            n = sq.shape[1]
            msq = jnp.dot((sq * sq).astype(BF16), bd[:n, :n], preferred_element_type=F32)
            outs.append(lax.rsqrt(msq + RMS_EPS))
        return outs[0] if len(outs) == 1 else jnp.concatenate(outs, axis=1)

    y = dot_s(S_AQ, S_AK)
    store_by_residue(y * head_inv_rms(y) * gs_ref[:, 0:768], 0)
    y = dot_s(S_AK, S_AV)
    store_by_residue(y * head_inv_rms(y) * gs_ref[:, 768:1536], 1)
    store_by_residue(dot_s(S_AV, S_BK), 2)
    y = dot_s(S_BK, S_CK)
    bk_ref[...] = (y * head_inv_rms(y) * gs_ref[:, 1536:2048]).astype(BF16)
    y = dot_s(S_CK, S_GATE)
    lane = lax.broadcasted_iota(jnp.int32, y.shape, 1)
    inv = jnp.where(lane < HEAD_DIM, head_inv_rms(y), 1.0)
    ck_ref[...] = (y * inv * gs_ref[:, 2048:2176]).astype(BF16)
    for c in range(3):
        y = dot_s(S_GATE + c * 1024, S_GATE + (c + 1) * 1024)
        gate_ref[:, c * 1024:(c + 1) * 1024] = jax.nn.sigmoid(y).astype(BF16)

    def dot_t(r0, r1):
        return lax.dot_general(wt_ref[r0:r1, :], h, (((1,), (1,)), ((), ())),
                               preferred_element_type=F32)

    def norm_t(y, gain):
        r = y.shape[0] // HEAD_DIM
        y3 = y.reshape(r, HEAD_DIM, tm)
        msq = jnp.mean(y3 * y3, axis=1, keepdims=True)
        return (y3 * lax.rsqrt(msq + RMS_EPS)).reshape(r * HEAD_DIM, tm) * gain

    bqt_ref[...] = norm_t(dot_t(T_BQ, T_BV), gt_ref[0:512, :]).astype(BF16)
    bvt_ref[...] = dot_t(T_BV, T_CQ).astype(BF16)
    cqt_ref[...] = norm_t(dot_t(T_CQ, T_CV), gt_ref[512:768, :]).astype(BF16)
    cvt_ref[...] = dot_t(T_CV, T_IQ).astype(BF16)
    iqt_ref[...] = dot_t(T_IQ, T_IW).astype(BF16)
    iwt_ref[...] = dot_t(T_IW, T_END) * (IDX_HEADS ** -0.5 * IDX_DIM ** -0.5)


def _proj_weights(w_in, qk_gain, tm):
    d = w_in.shape[0]
    o = 0
    a_qkv = w_in[:, o:o + 3 * A_HEADS * HEAD_DIM].reshape(d, 3, A_HEADS * HEAD_DIM)
    o += 3 * A_HEADS * HEAD_DIM
    b_qk = w_in[:, o:o + 4 * B_HEADS * HEAD_DIM].reshape(d, 4, B_HEADS, HEAD_DIM)
    o += 4 * B_HEADS * HEAD_DIM
    b_v = w_in[:, o:o + B_OUT]
    o += B_OUT
    c_q = w_in[:, o:o + C_OUT]
    c_k = w_in[:, o + C_OUT:o + C_OUT + HEAD_DIM]
    c_v = w_in[:, o + C_OUT + HEAD_DIM:o + C_OUT + 2 * HEAD_DIM]
    o += C_OUT + 2 * HEAD_DIM
    i_q = w_in[:, o:o + IDX_HEADS * IDX_DIM]
    i_k = w_in[:, o + IDX_HEADS * IDX_DIM:o + IDX_HEADS * IDX_DIM + IDX_DIM]
    i_w = w_in[:, o + IDX_HEADS * IDX_DIM + IDX_DIM:o + IDX_HEADS * IDX_DIM + IDX_DIM + IDX_HEADS]
    o += IDX_HEADS * IDX_DIM + IDX_DIM + IDX_HEADS
    gates = w_in[:, o:]
    b_k = jnp.stack([b_qk[:, 2], b_qk[:, 3]], axis=2).reshape(d, 2 * B_HEADS * HEAD_DIM)
    b_q = jnp.stack([b_qk[:, 0], b_qk[:, 1]], axis=2).reshape(d, 2 * B_HEADS * HEAD_DIM)
    w_s = jnp.concatenate([a_qkv[:, 0], a_qkv[:, 1], a_qkv[:, 2], b_k, c_k, i_k, gates], axis=1)
    w_t = jnp.concatenate([b_q, b_v, c_q, c_v, i_q, i_w, jnp.zeros((d, 8), w_in.dtype)], axis=1).T
    assert w_s.shape[1] == S_END and w_t.shape[0] == T_END
    scale = HEAD_DIM ** -0.5
    g = qk_gain.astype(F32)
    gs = jnp.concatenate([jnp.tile(g[0, 0] * scale, A_HEADS), jnp.tile(g[0, 1], A_HEADS),
                          jnp.tile(g[1, 1], 2 * B_HEADS), g[2, 1], jnp.ones((IDX_DIM,), F32)])[None]
    gt = jnp.concatenate([jnp.tile(g[1, 0] * (scale * LOG2E), 2 * B_HEADS),
                          jnp.tile(g[2, 0] * (scale * LOG2E), C_HEADS)])
    gt = jnp.broadcast_to(gt[:, None], (gt.shape[0], tm))
    return w_s.astype(BF16), w_t.astype(BF16), gs, gt


MXU_TILE = 256


def _head_block_diag():
    r = np.arange(MXU_TILE)
    return jnp.asarray((r[:, None] // HEAD_DIM == r[None, :] // HEAD_DIM) / HEAD_DIM, BF16)


def _project(x, gain, w_in, qk_gain, batch, t):
    n, d = x.shape
    tm = TOKEN_TILE
    nt = n // tm
    per_batch = t // tm
    w_s, w_t, gs, gt = _proj_weights(w_in, qk_gain, tm)
    tok = lambda c: pl.BlockSpec((tm, c), lambda i: (i, 0))
    feat = lambda r: pl.BlockSpec((None, r, tm), lambda i: (i, 0, 0))
    a_shapes, a_specs = [], []
    for _ in range(3):
        for _, dil in DIL_GROUPS:
            a_shapes.append(jax.ShapeDtypeStruct((batch, dil, t // dil, A_OUT), BF16))
            a_specs.append(pl.BlockSpec((None, dil, tm // dil, A_OUT),
                                        lambda i: (i // per_batch, 0, i % per_batch, 0)))
    out_shape = a_shapes + [
        jax.ShapeDtypeStruct((n, 512), BF16), jax.ShapeDtypeStruct((n, 128), BF16),
        jax.ShapeDtypeStruct((n, 3072), BF16),
        jax.ShapeDtypeStruct((nt, 512, tm), BF16), jax.ShapeDtypeStruct((nt, 512, tm), BF16),
        jax.ShapeDtypeStruct((nt, 256, tm), BF16), jax.ShapeDtypeStruct((nt, 64, tm), BF16),
        jax.ShapeDtypeStruct((nt, 512, tm), BF16), jax.ShapeDtypeStruct((nt, 16, tm), F32)]
    out_specs = a_specs + [tok(512), tok(128), tok(3072),
                           feat(512), feat(512), feat(256), feat(64), feat(512), feat(16)]
    return pl.pallas_call(
        _proj_kernel,
        out_shape=out_shape,
        grid=(nt,),
        in_specs=[tok(d), _const_spec((1, d)), _weight_spec(w_s.shape), _weight_spec(w_t.shape),
                  _const_spec((MXU_TILE, MXU_TILE)), _const_spec(gs.shape), _const_spec(gt.shape)],
        out_specs=out_specs,
        scratch_shapes=[pltpu.VMEM((A_OUT // LANES, tm, LANES), F32)],
        compiler_params=_cparams(("parallel",)),
        name="proj",
    )(x, gain.reshape(1, d).astype(F32), w_s, w_t, _head_block_diag(), gs, gt)


def _dil_kernel(q_ref, kp_ref, kc_ref, vp_ref, vc_ref, bias_ref, o_ref, lse_ref):
    nres, rows, _ = q_ref.shape
    nq = rows // LANES
    qi = pl.program_id(2)
    lane = lax.broadcasted_iota(jnp.int32, (LANES, A_OUT), 1) // HEAD_DIM
    mine = [lane == h for h in range(A_GROUP_HEADS)]
    blocks = [slice(jb * LANES, (jb + 1) * LANES) for jb in range(nq)]
    work = [(r, jb) for r in range(nres) for jb in range(nq)]

    def band(prev_ref, cur_ref, r, jb):
        if jb == 0:
            return jnp.concatenate([prev_ref[r], cur_ref[r, blocks[0], :]], axis=0)
        return cur_ref[r, (jb - 1) * LANES:(jb + 1) * LANES, :]

    logits = []
    for r, jb in work:
        q = q_ref[r, blocks[jb], :]
        q4 = jnp.concatenate([jnp.where(mine[h], q, jnp.zeros_like(q)) for h in range(A_GROUP_HEADS)], axis=0)
        s = lax.dot_general(q4, band(kp_ref, kc_ref, r, jb), (((1,), (1,)), ((), ())), preferred_element_type=F32)
        bias = bias_ref[jnp.minimum(qi, 1) if jb == 0 else 1]
        logits.append(s + bias.reshape(A_GROUP_HEADS * LANES, 2 * LANES))
    probs, stats = [], []
    for s in logits:
        m = jnp.max(s, axis=1, keepdims=True)
        p = jnp.exp(s - m)
        ssum = jnp.sum(p, axis=1, keepdims=True)
        probs.append(p.astype(BF16))
        stats.append((1.0 / ssum, m + jnp.log(ssum)))
    for (r, jb), p, (inv, lse4) in zip(work, probs, stats):
        pv = jnp.dot(p, band(vp_ref, vc_ref, r, jb), preferred_element_type=F32)
        o = jnp.zeros((LANES, A_OUT), F32)
        lse = jnp.zeros((LANES, A_OUT), F32)
        for h in range(A_GROUP_HEADS):
            head = slice(h * LANES, (h + 1) * LANES)
            o = jnp.where(mine[h], pv[head] * inv[head], o)
            lse = jnp.where(mine[h], lse4[head], lse)
        o_ref[r, blocks[jb], :] = o.astype(BF16)
        lse_ref[r, blocks[jb], :] = lse


def _dilated_group(aq, ak, av, bias, dilation):
    batch, _, n, _ = aq.shape
    nblk = n // LANES
    nq = min(nblk, A_BLOCKS_PER_STEP)
    nres = min(dilation, A_BLOCKS_PER_STEP // nq)
    qt = nq * LANES
    cur = pl.BlockSpec((None, nres, qt, A_OUT), lambda b, r, i: (b, r, i, 0))
    prev = pl.BlockSpec((None, nres, LANES, A_OUT), lambda b, r, i: (b, r, jnp.maximum(i * nq - 1, 0), 0))
    shp = (batch, dilation, n, A_OUT)
    return pl.pallas_call(
        _dil_kernel,
        out_shape=[jax.ShapeDtypeStruct(shp, BF16), jax.ShapeDtypeStruct(shp, F32)],
        grid=(batch, dilation // nres, nblk // nq),
        in_specs=[cur, prev, cur, prev, cur, _const_spec(bias.shape)],
        out_specs=[cur, cur],
        compiler_params=_cparams(("parallel", "parallel", "parallel")),
        name=f"dilated_d{dilation}",
    )(aq, ak, ak, av, av, bias)


def _fold_rows(x, op):
    r, c = x.shape
    return op(x.reshape(r // 64, 64, c), axis=0) if r > 64 else x


def _reduce_rows(x, op):
    x = _fold_rows(x, op)
    x = op(x.reshape(8, 8, x.shape[1]), axis=0)
    return op(x, axis=0, keepdims=True)


def _bias_tile(tab_ref, head, qblk, kblk):
    delta = qblk - kblk
    idx = jnp.where(delta < 0, MASKED, jnp.minimum(delta, FAR))
    if head is None:
        return tab_ref[idx]
    return tab_ref[head, idx]


def _diff_kernel(qt_ref, k_ref, vt_ref, tab_ref, lam_ref, gn_ref, o_ref):
    tq = qt_ref.shape[1]
    tk = TOKEN_TILE
    qi = pl.program_id(2)
    qt = qt_ref[...]
    row = lax.broadcasted_iota(jnp.int32, qt.shape, 0)
    q12 = jnp.concatenate([jnp.where(row < HEAD_DIM, qt, jnp.zeros_like(qt)),
                           jnp.where(row >= HEAD_DIM, qt, jnp.zeros_like(qt))], axis=1)
    nqb = tq // LANES
    nkb = tk // LANES

    def logits(c):
        kc = k_ref[pl.ds(pl.multiple_of(c * tk, tk), tk), :]
        bias = jnp.concatenate(
            [jnp.concatenate([_bias_tile(tab_ref, None, qi * nqb + iq, c * nkb + jk)
                              for iq in range(nqb)] * 2, axis=1) for jk in range(nkb)], axis=0)
        return jnp.dot(kc, q12, preferred_element_type=F32) + bias

    ones_rows = jnp.ones((ONES_ROWS, tk), BF16)

    def weigh(c, p):
        return jnp.dot(jnp.concatenate([vt_ref[c], ones_rows], axis=0), p.astype(BF16), preferred_element_type=F32)

    def exact_step(c, carry):
        m, acc = carry
        s = logits(c)
        m_new = jnp.maximum(m, _reduce_rows(s, jnp.max))
        alpha = jnp.exp2(m - m_new)
        return m_new, alpha * acc + weigh(c, jnp.exp2(s - m_new))

    def lagged_update(c, s, m, acc, jump):
        p = jnp.exp2(s - m)
        top =_reduce_rows(s, jnp.max)
        acc = acc + weigh(c, p)
        m_new = jnp.maximum(m, top)
        alpha = jnp.exp2(m - m_new)
        return m_new, alpha * acc, jnp.maximum(jump, top - m)

    nch = ((qi + 1) * tq + tk - 1) // tk
    zero = jnp.zeros((1, 2 * tq), F32)
    acc0 = jnp.zeros((B_V_DIM + ONES_ROWS, 2 * tq), F32)

    def opening(count):
        ss = [logits(c) for c in range(count)]
        state = (jnp.max(ss[0][0:SUBLANES], axis=0, keepdims=True), acc0, zero)
        for c in range(count):
            state = lagged_update(c, ss[c], *state)
        return state

    def lagged_pair(c, carry):
        s_a, s_b = logits(c), logits(c + 1)
        return lagged_update(c + 1, s_b, *lagged_update(c, s_a, *carry))

    odd = nch % 2
    state = lax.cond(odd == 1, lambda: opening(1), lambda: opening(2))
    _, acc, jump = lax.fori_loop(0, (nch - 2 + odd) // 2,
                                 lambda i, carry: lagged_pair(2 - odd + 2 * i, carry), state)
    acc = lax.cond(jnp.max(jump) > MAX_LAG,
                   lambda: lax.fori_loop(0, nch, exact_step, (jnp.full((1, 2 * tq), M_INIT, F32), acc0))[1],
                   lambda: acc)
    a1, a2 = acc[:B_V_DIM, :tq], acc[:B_V_DIM, tq:]
    l1, l2 = acc[B_V_DIM:B_V_DIM + 1, :tq], acc[B_V_DIM:B_V_DIM + 1, tq:]

    lv = lam_ref[...]
    lam = (jnp.exp(jnp.sum(lv[0:1] * lv[1:2], axis=1, keepdims=True))
           - jnp.exp(jnp.sum(lv[2:3] * lv[3:4], axis=1, keepdims=True)) + lv[4:5, 0:1])
    o = a1 / l1 - lam * (a2 / l2)
    ms = jnp.mean(o * o, axis=0, keepdims=True)
    o = o * lax.rsqrt(ms + RMS_EPS) * gn_ref[...]
    o_ref[...] = o.T.astype(BF16)


def _diff_attention(bqt, bk, bvt, tab, lam_rows, gn, batch, t):
    tq = B_Q_TILE
    tk = TOKEN_TILE
    per = tk // tq
    nkt = t // tk
    bqt = bqt.reshape(batch, nkt, B_HEADS * LANES, tk)
    bvt = bvt.reshape(batch, nkt, B_OUT, tk)
    bk = bk.reshape(batch, t, B_HEADS * LANES)
    return pl.pallas_call(
        _diff_kernel,
        out_shape=jax.ShapeDtypeStruct((batch, t, B_OUT), BF16),
        grid=(batch, B_HEADS, t // tq),
        in_specs=[pl.BlockSpec((None, None, LANES, tq), lambda b, h, i: (b, i // per, h, i % per)),
                  pl.BlockSpec((None, t, LANES), lambda b, h, i: (b, 0, h)),
                  pl.BlockSpec((None, nkt, B_V_DIM, tk), lambda b, h, i: (b, 0, h, 0)),
                  pl.BlockSpec((None, FAR + 2, LANES, LANES), lambda b, h, i: (h, 0, 0, 0)),
                  _const_spec(lam_rows.shape), _const_spec(gn.shape)],
        out_specs=pl.BlockSpec((None, tq, B_V_DIM), lambda b, h, i: (b, i, h)),
        compiler_params=_cparams(("parallel", "parallel", "arbitrary")),
        name="diff_attention",
    )(bqt, bk, bvt, tab, lam_rows, gn).reshape(batch * t, B_OUT)


def _dsa_kernel(iqt_ref, iwt_ref, cqt_ref, k_ref, vt_ref, tab_ref, tri_ref, o_ref, s_ref, *, k_sel):
    tk = TOKEN_TILE
    nkb = tk // LANES
    qw = o_ref.shape[0]
    nqb = qw // LANES
    qi = pl.program_id(1)
    nch = ((qi + 1) * qw + tk - 1) // tk
    qpos = qi * qw + lax.broadcasted_iota(jnp.int32, (1, qw), 1)
    zeros = jnp.zeros((HEAD_DIM, qw), BF16)
    iq = iqt_ref[...]
    w = iwt_ref[...]
    iq_all = jnp.concatenate([jnp.concatenate([zeros, iq[h * IDX_DIM:(h + 1) * IDX_DIM]], axis=0)
                              for h in range(IDX_HEADS)], axis=1)
    cq = cqt_ref[...]
    cq_all = jnp.concatenate([jnp.concatenate([cq[h * HEAD_DIM:(h + 1) * HEAD_DIM], zeros], axis=0)
                              for h in range(C_HEADS)], axis=1)

    def chunk(c):
        return pl.ds(pl.multiple_of(c * tk, tk), tk)

    def raw_scores(c):
        return jnp.dot(k_ref[chunk(c), :], iq_all, preferred_element_type=F32)

    def score_chunk(c, raw, mn, mx, last):
        acc = w[0:1, :] * jnp.maximum(raw[:, 0:qw], 0.0)
        for h in range(1, IDX_HEADS):
            acc = acc + w[h:h + 1, :] * jnp.maximum(raw[:, h * qw:(h + 1) * qw], 0.0)
        if last:
            kpos = c * tk + lax.broadcasted_iota(jnp.int32, (tk, qw), 0)
            causal = kpos <= qpos
            s_ref[chunk(c), :] = jnp.where(causal, acc, NEG)
            mn = jnp.minimum(mn, _fold_rows(jnp.where(causal, acc, BIG), jnp.min))
            mx = jnp.maximum(mx, _fold_rows(jnp.where(causal, acc, NEG), jnp.max))
        else:
            s_ref[chunk(c), :] = acc
            mn = jnp.minimum(mn, _fold_rows(acc, jnp.min))
            mx = jnp.maximum(mx, _fold_rows(acc, jnp.max))
        return mn, mx

    def score_pair(c, carry, last):
        raw_a, raw_b = raw_scores(c), raw_scores(c + 1)
        return score_chunk(c + 1, raw_b, *score_chunk(c, raw_a, *carry, last=False), last=last)

    carry = lax.fori_loop(0, (nch - 1) // 2, lambda i, carry: score_pair(2 * i, carry, last=False),
                          (jnp.full((64, qw), BIG, F32), jnp.full((64, qw), NEG, F32)))
    mn, mx = lax.cond((nch - 1) % 2 == 1,
                      lambda st: score_pair(nch - 2, st, last=True),
                      lambda st: score_chunk(nch - 1, raw_scores(nch - 1), *st, last=True), carry)
    lo, hi = _reduce_rows(mn, jnp.min), _reduce_rows(mx, jnp.max)

    def count_gt(thr):
        def body(c, gt):
            return gt + _fold_rows(jnp.where(s_ref[chunk(c), :] > thr, 1.0, 0.0), jnp.sum)
        return _reduce_rows(lax.fori_loop(0, nch, body, jnp.zeros((64, qw), F32)), jnp.sum)

    def largest_upto(bound):
        def body(c, mx):
            s = s_ref[chunk(c), :]
            return jnp.maximum(mx, _fold_rows(jnp.where(s <= bound, s, NEG), jnp.max))
        return _reduce_rows(lax.fori_loop(0, nch, body, jnp.full((64, qw), NEG, F32)), jnp.max)

    def next_below_and_multiplicity(cand):
        def body(c, carry):
            mx, eq = carry
            s = s_ref[chunk(c), :]
            mx = jnp.maximum(mx, _fold_rows(jnp.where(s < cand, s, NEG), jnp.max))
            eq = eq + _fold_rows(jnp.where(s == cand, 1.0, 0.0), jnp.sum)
            return mx, eq
        mx, eq = lax.fori_loop(0, nch, body, (jnp.full((64, qw), NEG, F32), jnp.zeros((64, qw), F32)))
        return _reduce_rows(mx, jnp.max), _reduce_rows(eq, jnp.sum)

    kf = float(k_sel)
    need = qpos >= k_sel

    def bisect(_, carry):
        lo, hi, above = carry
        mid = lo + (hi - lo) * 0.5
        cnt = count_gt(mid)
        below = cnt < kf
        return jnp.where(below, lo, mid), jnp.where(below, mid, hi), jnp.where(below, cnt, above)

    _, hi, above = lax.fori_loop(0, N_BISECT, bisect, (lo - 1.0, hi, jnp.zeros((1, qw), F32)))

    def walk_cond(carry):
        _, _, ge = carry
        return jnp.max(jnp.where(need & (ge < kf), 1.0, 0.0)) > 0.0

    def walk_body(carry):
        cand, gt, ge = carry
        active = ge < kf
        nxt, mult = next_below_and_multiplicity(cand)
        ge_new = gt + mult
        moved = active & (ge_new < kf)
        return jnp.where(moved, nxt, cand), jnp.where(moved, ge_new, gt), jnp.where(active, ge_new, ge)

    cand, gt, ge = lax.while_loop(walk_cond, walk_body, (largest_upto(hi), above, above))
    thr = jnp.where(need, cand, THR_ALL)
    want_eq = jnp.where(need, kf - gt, 0.0)

    any_tie = jnp.max(jnp.where(need & (ge > kf), 1.0, 0.0)) > 0.0

    def mark_with_ties(c, eq_seen):
        s = s_ref[chunk(c), :]
        eq = jnp.where(s == thr, 1.0, 0.0)
        rank = eq_seen + jnp.dot(tri_ref[...], eq.astype(BF16), preferred_element_type=F32)
        keep = jnp.where(s > thr, 1.0, jnp.where(rank <= want_eq, eq, 0.0))
        s_ref[chunk(c), :] = jnp.where(keep > 0.5, 0.0, NEG)
        return eq_seen + _reduce_rows(eq, jnp.sum)

    def mark_no_ties(c, carry):
        s_ref[chunk(c), :] = jnp.where(s_ref[chunk(c), :] >= thr, 0.0, NEG)
        return carry

    @pl.when(any_tie)
    def _():
        lax.fori_loop(0, nch, mark_with_ties, jnp.zeros((1, qw), F32))

    @pl.when(jnp.logical_not(any_tie))
    def _():
        lax.fori_loop(0, nch, mark_no_ties, 0)

    def masked_logits(c):
        sel = s_ref[chunk(c), :]
        bias = jnp.concatenate(
            [jnp.concatenate([jnp.concatenate([_bias_tile(tab_ref, h, qi * nqb + iq, c * nkb + jk)
                                               for iq in range(nqb)], axis=1) for jk in range(nkb)], axis=0) + sel
             for h in range(C_HEADS)], axis=1)
        return jnp.dot(k_ref[chunk(c), :], cq_all, preferred_element_type=F32) + bias

    ones_rows = jnp.ones((ONES_ROWS, tk), BF16)

    def weigh(c, p):
        return jnp.dot(jnp.concatenate([vt_ref[c], ones_rows], axis=0), p.astype(BF16), preferred_element_type=F32)

    def exact_update(c, lg, m, acc):
        m_new = jnp.maximum(m, _reduce_rows(lg, jnp.max))
        alpha = jnp.exp2(m - m_new)
        return m_new, alpha * acc + weigh(c, jnp.exp2(lg - m_new))

    def lagged_update(c, lg, m, acc, jump):
        p = jnp.exp2(lg - m)
        top = _reduce_rows(lg, jnp.max)
        acc = acc + weigh(c, p)
        m_new = jnp.maximum(m, top)
        alpha = jnp.exp2(m - m_new)
        return m_new, alpha * acc, jnp.maximum(jump, top - m)

    wide = C_HEADS * qw
    init = (jnp.full((1, wide), M_INIT, F32), jnp.zeros((HEAD_DIM + ONES_ROWS, wide), F32))

    def opening(count):
        lgs = [masked_logits(c) for c in range(count)]
        state = exact_update(0, lgs[0], *init) + (jnp.zeros((1, wide), F32),)
        return lagged_update(1, lgs[1], *state) if count == 2 else state

    def lagged_pair(c, carry):
        lg_a, lg_b = masked_logits(c), masked_logits(c + 1)
        return lagged_update(c + 1, lg_b, *lagged_update(c, lg_a, *carry))

    odd = nch % 2
    state = lax.cond(odd == 1, lambda: opening(1), lambda: opening(2))
    _, acc, jump = lax.fori_loop(0, (nch - 2 + odd) // 2,
                                 lambda i, carry: lagged_pair(2 - odd + 2 * i, carry), state)
    acc = lax.cond(jnp.max(jump) > MAX_LAG,
                   lambda: lax.fori_loop(0, nch, lambda c, st: exact_update(c, masked_logits(c), *st), init)[1],
                   lambda: acc)
    o = acc[:HEAD_DIM] / acc[HEAD_DIM:HEAD_DIM + 1]
    ot = jnp.concatenate([o[:, h * qw:(h + 1) * qw] for h in range(C_HEADS)], axis=0)
    o_ref[...] = ot.T.astype(BF16)


def _dsa_attention(iqt, iwt, cqt, ck, cvt, tab, batch, t):
    tk = TOKEN_TILE
    qw = C_Q_TILE
    per = tk // qw
    nkt = t // tk
    k_sel = min(TOPK_MAX, t // 4)
    iqt = iqt.reshape(batch, nkt, IDX_HEADS * IDX_DIM, tk)
    iwt = iwt.reshape(batch, nkt, 16, tk)
    cqt = cqt.reshape(batch, nkt, C_OUT, tk)
    cvt = cvt.reshape(batch, nkt, HEAD_DIM, tk)
    ck = ck.reshape(batch, t, LANES)
    r = np.arange(tk)
    tri = jnp.asarray(r[:, None] >= r[None, :], BF16)
    qblock = lambda rows: pl.BlockSpec((None, None, rows, qw), lambda b, i: (b, i // per, 0, i % per))
    return pl.pallas_call(
        functools.partial(_dsa_kernel, k_sel=k_sel),
        out_shape=jax.ShapeDtypeStruct((batch, t, C_OUT), BF16),
        grid=(batch, t // qw),
        in_specs=[qblock(IDX_HEADS * IDX_DIM), qblock(16), qblock(C_OUT),
                  pl.BlockSpec((None, t, LANES), lambda b, i: (b, 0, 0)),
                  pl.BlockSpec((None, nkt, HEAD_DIM, tk), lambda b, i: (b, 0, 0, 0)),
                  _const_spec(tab.shape), _const_spec(tri.shape)],
        out_specs=pl.BlockSpec((None, qw, C_OUT), lambda b, i: (b, i, 0)),
        scratch_shapes=[pltpu.VMEM((t, qw), F32)],
        compiler_params=_cparams(("parallel", "arbitrary")),
        name="dsa_attention",
    )(iqt, iwt, cqt, ck, cvt, tab, tri).reshape(batch * t, C_OUT)


def _merge_kernel(x_ref, *refs):
    ng = len(DIL_GROUPS)
    a_refs = refs[:2 * ng]
    ob_ref, oc_ref, gate_ref, wa_ref, wb_ref, wc_ref, wo_ref, out_ref = refs[2 * ng:-1]
    shuffle_ref = refs[-1]
    tm, d = x_ref.shape

    def token_order(ref, slot):
        dil = ref.shape[0]
        if dil == 1:
            return ref[0].astype(F32)
        halves = range(A_OUT // LANES)
        for r in range(dil):
            for half in halves:
                shuffle_ref[slot, half, pl.ds(r, tm // dil, stride=dil), :] = (
                    ref[r, :, half * LANES:(half + 1) * LANES].astype(F32))
        return jnp.concatenate([shuffle_ref[slot, half] for half in halves], axis=1)

    outs = [token_order(a_refs[2 * g], 2 * g) for g in range(ng)]
    lses = [token_order(a_refs[2 * g + 1], 2 * g + 1) for g in range(ng)]
    top = functools.reduce(jnp.maximum, lses)
    es = [jnp.exp(lse - top) for lse in lses]
    num = sum(e * o for e, o in zip(es, outs))
    oa = (num / sum(es)).astype(BF16)
    y = gate_ref[:, 0:d].astype(F32) * jnp.dot(oa, wa_ref[...], preferred_element_type=F32)
    y = y + gate_ref[:, d:2 * d].astype(F32) * jnp.dot(ob_ref[...], wb_ref[...], preferred_element_type=F32)
    y = y + gate_ref[:, 2 * d:3 * d].astype(F32) * jnp.dot(oc_ref[...], wc_ref[...], preferred_element_type=F32)
    out_ref[...] = x_ref[...] + jnp.dot(y.astype(BF16), wo_ref[...], preferred_element_type=F32)


def _merge(x, a_parts, ob, oc, gates, wa, wb, wc, wo, t):
    n, d = x.shape
    tm = TOKEN_TILE
    per_batch = t // tm
    tok = lambda c: pl.BlockSpec((tm, c), lambda i: (i, 0))
    by_residue = lambda dil: pl.BlockSpec((None, dil, tm // dil, A_OUT),
                                          lambda i: (i // per_batch, 0, i % per_batch, 0))
    ws = [w.astype(BF16) for w in (wa, wb, wc, wo)]
    return pl.pallas_call(
        _merge_kernel,
        out_shape=jax.ShapeDtypeStruct((n, d), F32),
        grid=(n // tm,),
        in_specs=[tok(d)] + [by_residue(z.shape[1]) for z in a_parts] + [tok(B_OUT), tok(C_OUT), tok(3 * d)]
                 + [_weight_spec(w.shape) for w in ws],
        out_specs=tok(d),
        scratch_shapes=[pltpu.VMEM((len(a_parts), A_OUT // LANES, tm, LANES), F32)],
        compiler_params=_cparams(("parallel",)),
        name="merge",
    )(x, *a_parts, ob, oc, gates, *ws)


def _token_mixer(x, batch, t, layer, mix_norm, w_in, qk_gain, diff_lambda, diff_out_norm,
                 w_branch_a, w_branch_b, w_branch_c, w_out, band_tabs, tab_b, tab_c):
    ng = len(DIL_GROUPS)
    outs = _project(x, mix_norm, w_in, qk_gain, batch, t)
    a_in, (bk, ck, gates, bqt, bvt, cqt, cvt, iqt, iwt) = outs[:3 * ng], outs[3 * ng:]
    a_parts = []
    for g, (_, dilation) in enumerate(DIL_GROUPS):
        a_parts += _dilated_group(a_in[g], a_in[ng + g], a_in[2 * ng + g], band_tabs[g], dilation)
    lam_init = 0.8 - 0.6 * np.exp(-0.3 * layer)
    lam_rows = jnp.concatenate([diff_lambda.astype(F32), jnp.full((4, HEAD_DIM), lam_init, F32)], axis=0)
    gn = jnp.broadcast_to((diff_out_norm.astype(F32) * (1.0 - lam_init))[:, None], (B_V_DIM, B_Q_TILE))
    ob = _diff_attention(bqt, bk, bvt, tab_b, lam_rows, gn, batch, t)
    oc = _dsa_attention(iqt, iwt, cqt, ck, cvt, tab_c, batch, t)
    return _merge(x, a_parts, ob, oc, gates, w_branch_a, w_branch_b, w_branch_c, w_out, t)


def kernel(x, rel_bias, ffn1_norm, ffn1_w_gate, ffn1_w_up, ffn1_w_down, mix_norm, w_in, qk_gain,
           diff_lambda, diff_out_norm, w_branch_a, w_branch_b, w_branch_c, w_out,
           ffn2_norm, ffn2_w_gate, ffn2_w_up, ffn2_w_down):
    batch, t, d = x.shape
    depth = w_in.shape[0]
    assert t % (DIL_GROUPS[-1][1] * LANES) == 0 and t % TOKEN_TILE == 0
    band_tabs = [_band_tables(rel_bias[:, g * A_GROUP_HEADS:(g + 1) * A_GROUP_HEADS], dil)
                 for g, (_, dil) in enumerate(DIL_GROUPS)]
    tab_b = _toeplitz_tables(rel_bias[:, A_HEADS:A_HEADS + B_HEADS] * LOG2E)
    tab_c = _toeplitz_tables(rel_bias[:, A_HEADS + B_HEADS:] * LOG2E)
    h = x.reshape(batch * t, d).astype(F32)
    for i in range(depth):
        h = _ffn(h, ffn1_norm[i], ffn1_w_gate[i], ffn1_w_up[i], ffn1_w_down[i])
        h = _token_mixer(h, batch, t, i, mix_norm[i], w_in[i], qk_gain[i], diff_lambda[i], diff_out_norm[i],
                         w_branch_a[i], w_branch_b[i], w_branch_c[i], w_out[i], band_tabs, tab_b, tab_c)
        h = _ffn(h, ffn2_norm[i], ffn2_w_gate[i], ffn2_w_up[i], ffn2_w_down[i])
    return h.reshape(batch, t, d).astype(x.dtype)
```

```python
import functools

import numpy as np
import jax
import jax.numpy as jnp
from jax import lax
from jax.experimental import pallas as pl
from jax.experimental.pallas import tpu as pltpu

F32 = jnp.float32
BF16 = jnp.bfloat16

HEAD_DIM = 64
DIL_GROUPS = ((128, 1), (512, 4), (2048, 16))
A_GROUP_HEADS = 4
A_HEADS = A_GROUP_HEADS * len(DIL_GROUPS)
A_OUT = A_GROUP_HEADS * HEAD_DIM
B_HEADS = 4
B_V_DIM = 2 * HEAD_DIM
B_OUT = B_HEADS * B_V_DIM
C_HEADS = 4
C_OUT = C_HEADS * HEAD_DIM
IDX_HEADS = 8
IDX_DIM = 64
TOPK_MAX = 256
NUM_BUCKETS = 32
MAX_DISTANCE = 2048
RMS_EPS = 1e-6
LOG2E = 1.4426950408889634

LANES = 128
SUBLANES = 8
ONES_ROWS = 16
TOKEN_TILE = 512
FFN_TILE = 1024
A_BLOCKS_PER_STEP = 4
B_Q_TILE = 512
C_Q_TILE = 256
MAX_LAG = 60.0
FFN_CHUNK = 256
VMEM_LIMIT = 58 * 1024 * 1024

NEG = -1e30
M_INIT = -1e29
BIG = 1e30
THR_ALL = -1e29
N_BISECT = 14


def _cparams(sem):
    return pltpu.CompilerParams(dimension_semantics=sem, vmem_limit_bytes=VMEM_LIMIT)


def _const_spec(shape):
    nd = len(shape)
    return pl.BlockSpec(shape, lambda *_: (0,) * nd)


def _weight_spec(shape):
    nd = len(shape)
    return pl.BlockSpec(shape, lambda *_: (0,) * nd, pipeline_mode=pl.Buffered(1))


def _rel_bucket_np(dist):
    n = np.maximum(dist, 0)
    max_exact = NUM_BUCKETS // 2
    nf = np.maximum(n, 1).astype(np.float64)
    large = max_exact + (np.log(nf / max_exact) / np.log(MAX_DISTANCE / max_exact)
                         * (NUM_BUCKETS - max_exact)).astype(np.int64)
    large = np.minimum(large, NUM_BUCKETS - 1)
    return np.where(n < max_exact, n, large)


def _far_delta():
    d = 1
    while not np.all(_rel_bucket_np(np.arange(d * LANES - LANES + 1, d * LANES + LANES)) == NUM_BUCKETS - 1):
        d += 1
    return d


FAR = _far_delta()
MASKED = FAR + 1


def _toeplitz(w, n_rows, n_cols):
    period = n_rows + n_cols
    w = jnp.pad(w, ((0, 0), (0, period - w.shape[1])))
    m = jnp.tile(w, (1, n_rows))[:, :n_rows * (period - 1)].reshape(-1, n_rows, period - 1)
    return m[:, :, n_rows - 1:n_rows - 1 + n_cols]


def _bias_by_distance(bias_heads, dist, valid):
    vals = jnp.take(bias_heads.astype(F32), jnp.asarray(_rel_bucket_np(dist), jnp.int32), axis=0).T
    return jnp.where(jnp.asarray(valid)[None], vals, NEG)


def _toeplitz_tables(bias_heads):
    n_cols = (FAR + 1) * LANES
    dist = np.arange(LANES - 1 + n_cols) - (LANES - 1)
    tiles = _toeplitz(_bias_by_distance(bias_heads, dist, dist >= 0), LANES, n_cols)
    tiles = tiles.reshape(-1, LANES, FAR + 1, LANES).transpose(0, 2, 1, 3)
    masked = jnp.full((tiles.shape[0], 1, LANES, LANES), NEG, F32)
    return jnp.concatenate([tiles, masked], axis=1)


def _band_tables(bias_heads, dilation):
    wn = LANES
    sub = np.arange(3 * wn - 1) - (wn - 1)
    w = _bias_by_distance(bias_heads, sub * dilation, (sub >= 0) & (sub <= wn))
    later = jnp.flip(_toeplitz(w, wn, 2 * wn), axis=(1, 2))
    first = jnp.where(jnp.asarray(np.arange(2 * wn) >= wn)[None, None], later, NEG)
    return jnp.stack([first, later])


def _ffn_kernel(x_ref, g_ref, wg_ref, wu_ref, wd_ref, o_ref, acc_ref):
    x = x_ref[...]
    ms = jnp.mean(x * x, axis=-1, keepdims=True)
    h = (x * lax.rsqrt(ms + RMS_EPS) * g_ref[...]).astype(BF16)
    acc_ref[...] = jnp.zeros_like(acc_ref)

    def body(c, carry):
        cols = pl.ds(pl.multiple_of(c * FFN_CHUNK, FFN_CHUNK), FFN_CHUNK)
        g = jnp.dot(h, wg_ref[:, cols], preferred_element_type=F32)
        u = jnp.dot(h, wu_ref[:, cols], preferred_element_type=F32)
        a = (g * jax.nn.sigmoid(g) * u).astype(BF16)
        acc_ref[...] += jnp.dot(a, wd_ref[cols, :], preferred_element_type=F32)
        return carry

    lax.fori_loop(0, wg_ref.shape[1] // FFN_CHUNK, body, 0)
    o_ref[...] = x + 0.5 * acc_ref[...]


def _ffn(x, gain, w_gate, w_up, w_down):
    n, d = x.shape
    f = w_gate.shape[1]
    tm = FFN_TILE
    return pl.pallas_call(
        _ffn_kernel,
        out_shape=jax.ShapeDtypeStruct((n, d), F32),
        grid=(n // tm,),
        in_specs=[pl.BlockSpec((tm, d), lambda i: (i, 0)),
                  _const_spec((1, d)), _weight_spec((d, f)), _weight_spec((d, f)), _weight_spec((f, d))],
        out_specs=pl.BlockSpec((tm, d), lambda i: (i, 0)),
        scratch_shapes=[pltpu.VMEM((tm, d), F32)],
        compiler_params=_cparams(("parallel",)),
        name="ffn",
    )(x, gain.reshape(1, d).astype(F32), w_gate.astype(BF16), w_up.astype(BF16), w_down.astype(BF16))


S_AQ, S_AK, S_AV = 0, 768, 1536
S_BK, S_CK, S_GATE, S_END = 2304, 2816, 2944, 6016
T_BQ, T_BV, T_CQ, T_CV, T_IQ, T_IW, T_END = 0, 512, 1024, 1280, 1344, 1856, 1872


def _proj_kernel(x_ref, g_ref, ws_ref, wt_ref, bd_ref, gs_ref, gt_ref, *refs):
    ng = len(DIL_GROUPS)
    a_refs = refs[:ng]
    bk_ref, ck_ref, gate_ref, bqt_ref, bvt_ref, cqt_ref, cvt_ref, iqt_ref, iwt_ref = refs[ng:-1]
    shuffle_ref = refs[-1]
    tm = x_ref.shape[0]

    def store_by_residue(y, which):
        cols = slice(which * A_OUT, (which + 1) * A_OUT)
        for g, (_, dil) in enumerate(DIL_GROUPS):
            out = a_refs[g]
            part = y[:, g * A_OUT:(g + 1) * A_OUT]
            if dil == 1:
                out[0, :, cols] = part.astype(BF16)
            else:
                for half in range(A_OUT // LANES):
                    shuffle_ref[half] = part[:, half * LANES:(half + 1) * LANES]
                for r in range(dil):
                    out[r, :, cols] = jnp.concatenate(
                        [shuffle_ref[half, pl.ds(r, tm // dil, stride=dil), :] for half in range(A_OUT // LANES)],
                        axis=1).astype(BF16)

    x = x_ref[...]
    ms = jnp.mean(x * x, axis=-1, keepdims=True)
    h = (x * lax.rsqrt(ms + RMS_EPS) * g_ref[...]).astype(BF16)
    bd = bd_ref[...]

    def dot_s(c0, c1):
        return jnp.dot(h, ws_ref[:, c0:c1], preferred_element_type=F32)

    def head_inv_rms(y):
        width = bd.shape[0]
        outs = []
        for c0 in range(0, y.shape[1], width):
            sq = y[:, c0:c0 + width]
            n = sq.shape[1]
            msq = jnp.dot((sq * sq).astype(BF16), bd[:n, :n], preferred_element_type=F32)
            outs.append(lax.rsqrt(msq + RMS_EPS))
        return outs[0] if len(outs) == 1 else jnp.concatenate(outs, axis=1)

    y = dot_s(S_AQ, S_AK)
    store_by_residue(y * head_inv_rms(y) * gs_ref[:, 0:768], 0)
    y = dot_s(S_AK, S_AV)
    store_by_residue(y * head_inv_rms(y) * gs_ref[:, 768:1536], 1)
    store_by_residue(dot_s(S_AV, S_BK), 2)
    y = dot_s(S_BK, S_CK)
    bk_ref[...] = (y * head_inv_rms(y) * gs_ref[:, 1536:2048]).astype(BF16)
    y = dot_s(S_CK, S_GATE)
    lane = lax.broadcasted_iota(jnp.int32, y.shape, 1)
    inv = jnp.where(lane < HEAD_DIM, head_inv_rms(y), 1.0)
    ck_ref[...] = (y * inv * gs_ref[:, 2048:2176]).astype(BF16)
    for c in range(3):
        y = dot_s(S_GATE + c * 1024, S_GATE + (c + 1) * 1024)
        gate_ref[:, c * 1024:(c + 1) * 1024] = jax.nn.sigmoid(y).astype(BF16)

    def dot_t(r0, r1):
        return lax.dot_general(wt_ref[r0:r1, :], h, (((1,), (1,)), ((), ())),
                               preferred_element_type=F32)

    def norm_t(y, gain):
        r = y.shape[0] // HEAD_DIM
        y3 = y.reshape(r, HEAD_DIM, tm)
        msq = jnp.mean(y3 * y3, axis=1, keepdims=True)
        return (y3 * lax.rsqrt(msq + RMS_EPS)).reshape(r * HEAD_DIM, tm) * gain

    bqt_ref[...] = norm_t(dot_t(T_BQ, T_BV), gt_ref[0:512, :]).astype(BF16)
    bvt_ref[...] = dot_t(T_BV, T_CQ).astype(BF16)
    cqt_ref[...] = norm_t(dot_t(T_CQ, T_CV), gt_ref[512:768, :]).astype(BF16)
    cvt_ref[...] = dot_t(T_CV, T_IQ).astype(BF16)
    iqt_ref[...] = dot_t(T_IQ, T_IW).astype(BF16)
    iwt_ref[...] = dot_t(T_IW, T_END) * (IDX_HEADS ** -0.5 * IDX_DIM ** -0.5)


def _proj_weights(w_in, qk_gain, tm):
    d = w_in.shape[0]
    o = 0
    a_qkv = w_in[:, o:o + 3 * A_HEADS * HEAD_DIM].reshape(d, 3, A_HEADS * HEAD_DIM)
    o += 3 * A_HEADS * HEAD_DIM
    b_qk = w_in[:, o:o + 4 * B_HEADS * HEAD_DIM].reshape(d, 4, B_HEADS, HEAD_DIM)
    o += 4 * B_HEADS * HEAD_DIM
    b_v = w_in[:, o:o + B_OUT]
    o += B_OUT
    c_q = w_in[:, o:o + C_OUT]
    c_k = w_in[:, o + C_OUT:o + C_OUT + HEAD_DIM]
    c_v = w_in[:, o + C_OUT + HEAD_DIM:o + C_OUT + 2 * HEAD_DIM]
    o += C_OUT + 2 * HEAD_DIM
    i_q = w_in[:, o:o + IDX_HEADS * IDX_DIM]
    i_k = w_in[:, o + IDX_HEADS * IDX_DIM:o + IDX_HEADS * IDX_DIM + IDX_DIM]
    i_w = w_in[:, o + IDX_HEADS * IDX_DIM + IDX_DIM:o + IDX_HEADS * IDX_DIM + IDX_DIM + IDX_HEADS]
    o += IDX_HEADS * IDX_DIM + IDX_DIM + IDX_HEADS
    gates = w_in[:, o:]
    b_k = jnp.stack([b_qk[:, 2], b_qk[:, 3]], axis=2).reshape(d, 2 * B_HEADS * HEAD_DIM)
    b_q = jnp.stack([b_qk[:, 0], b_qk[:, 1]], axis=2).reshape(d, 2 * B_HEADS * HEAD_DIM)
    w_s = jnp.concatenate([a_qkv[:, 0], a_qkv[:, 1], a_qkv[:, 2], b_k, c_k, i_k, gates], axis=1)
    w_t = jnp.concatenate([b_q, b_v, c_q, c_v, i_q, i_w, jnp.zeros((d, 8), w_in.dtype)], axis=1).T
    assert w_s.shape[1] == S_END and w_t.shape[0] == T_END
    scale = HEAD_DIM ** -0.5
    g = qk_gain.astype(F32)
    gs = jnp.concatenate([jnp.tile(g[0, 0] * scale, A_HEADS), jnp.tile(g[0, 1], A_HEADS),
                          jnp.tile(g[1, 1], 2 * B_HEADS), g[2, 1], jnp.ones((IDX_DIM,), F32)])[None]
    gt = jnp.concatenate([jnp.tile(g[1, 0] * (scale * LOG2E), 2 * B_HEADS),
                          jnp.tile(g[2, 0] * (scale * LOG2E), C_HEADS)])
    gt = jnp.broadcast_to(gt[:, None], (gt.shape[0], tm))
    return w_s.astype(BF16), w_t.astype(BF16), gs, gt


MXU_TILE = 256


def _head_block_diag():
    r = np.arange(MXU_TILE)
    return jnp.asarray((r[:, None] // HEAD_DIM == r[None, :] // HEAD_DIM) / HEAD_DIM, BF16)


def _project(x, gain, w_in, qk_gain, batch, t):
    n, d = x.shape
    tm = TOKEN_TILE
    nt = n // tm
    per_batch = t // tm
    w_s, w_t, gs, gt = _proj_weights(w_in, qk_gain, tm)
    tok = lambda c: pl.BlockSpec((tm, c), lambda i: (i, 0))
    feat = lambda r: pl.BlockSpec((None, r, tm), lambda i: (i, 0, 0))
    a_shapes, a_specs = [], []
    for _, dil in DIL_GROUPS:
        a_shapes.append(jax.ShapeDtypeStruct((batch, dil, t // dil, 3 * A_OUT), BF16))
        a_specs.append(pl.BlockSpec((None, dil, tm // dil, 3 * A_OUT),
                                    lambda i: (i // per_batch, 0, i % per_batch, 0)))
    out_shape = a_shapes + [
        jax.ShapeDtypeStruct((n, 512), BF16), jax.ShapeDtypeStruct((n, 128), BF16),
        jax.ShapeDtypeStruct((n, 3072), BF16),
        jax.ShapeDtypeStruct((nt, 512, tm), BF16), jax.ShapeDtypeStruct((nt, 512, tm), BF16),
        jax.ShapeDtypeStruct((nt, 256, tm), BF16), jax.ShapeDtypeStruct((nt, 64, tm), BF16),
        jax.ShapeDtypeStruct((nt, 512, tm), BF16), jax.ShapeDtypeStruct((nt, 16, tm), F32)]
    out_specs = a_specs + [tok(512), tok(128), tok(3072),
                           feat(512), feat(512), feat(256), feat(64), feat(512), feat(16)]
    return pl.pallas_call(
        _proj_kernel,
        out_shape=out_shape,
        grid=(nt,),
        in_specs=[tok(d), _const_spec((1, d)), _weight_spec(w_s.shape), _weight_spec(w_t.shape),
                  _const_spec((MXU_TILE, MXU_TILE)), _const_spec(gs.shape), _const_spec(gt.shape)],
        out_specs=out_specs,
        scratch_shapes=[pltpu.VMEM((A_OUT // LANES, tm, LANES), F32)],
        compiler_params=_cparams(("parallel",)),
        name="proj",
    )(x, gain.reshape(1, d).astype(F32), w_s, w_t, _head_block_diag(), gs, gt)


def _dil_kernel(cur_ref, prev_ref, bias_ref, o_ref, lse_ref):
    nres, rows, _ = cur_ref.shape
    q_ref = cur_ref.at[:, :, 0:A_OUT]
    kc_ref, kp_ref = cur_ref.at[:, :, A_OUT:2 * A_OUT], prev_ref.at[:, :, A_OUT:2 * A_OUT]
    vc_ref, vp_ref = cur_ref.at[:, :, 2 * A_OUT:3 * A_OUT], prev_ref.at[:, :, 2 * A_OUT:3 * A_OUT]
    nq = rows // LANES
    qi = pl.program_id(2)
    lane = lax.broadcasted_iota(jnp.int32, (LANES, A_OUT), 1) // HEAD_DIM
    mine = [lane == h for h in range(A_GROUP_HEADS)]
    blocks = [slice(jb * LANES, (jb + 1) * LANES) for jb in range(nq)]
    work = [(r, jb) for r in range(nres) for jb in range(nq)]

    def band(prev_ref, cur_ref, r, jb):
        if jb == 0:
            return jnp.concatenate([prev_ref[r], cur_ref[r, blocks[0], :]], axis=0)
        return cur_ref[r, (jb - 1) * LANES:(jb + 1) * LANES, :]

    logits = []
    for r, jb in work:
        q = q_ref[r, blocks[jb], :]
        q4 = jnp.concatenate([jnp.where(mine[h], q, jnp.zeros_like(q)) for h in range(A_GROUP_HEADS)], axis=0)
        s = lax.dot_general(q4, band(kp_ref, kc_ref, r, jb), (((1,), (1,)), ((), ())), preferred_element_type=F32)
        bias = bias_ref[jnp.minimum(qi, 1) if jb == 0 else 1]
        logits.append(s + bias.reshape(A_GROUP_HEADS * LANES, 2 * LANES))
    probs, stats = [], []
    for s in logits:
        m = jnp.max(s, axis=1, keepdims=True)
        p = jnp.exp(s - m)
        ssum = jnp.sum(p, axis=1, keepdims=True)
        probs.append(p.astype(BF16))
        stats.append((1.0 / ssum, m + jnp.log(ssum)))
    for (r, jb), p, (inv, lse4) in zip(work, probs, stats):
        pv = jnp.dot(p, band(vp_ref, vc_ref, r, jb), preferred_element_type=F32)
        o = jnp.zeros((LANES, A_OUT), F32)
        lse = jnp.zeros((LANES, A_OUT), F32)
        for h in range(A_GROUP_HEADS):
            head = slice(h * LANES, (h + 1) * LANES)
            o = jnp.where(mine[h], pv[head] * inv[head], o)
            lse = jnp.where(mine[h], lse4[head], lse)
        o_ref[r, blocks[jb], :] = o.astype(BF16)
        lse_ref[r, blocks[jb], :] = lse


def _dilated_group(qkv, bias, dilation):
    batch, _, n, _ = qkv.shape
    nblk = n // LANES
    nq = min(nblk, A_BLOCKS_PER_STEP)
    nres = min(dilation, A_BLOCKS_PER_STEP // nq)
    qt = nq * LANES
    cur = lambda c: pl.BlockSpec((None, nres, qt, c), lambda b, r, i: (b, r, i, 0))
    prev = pl.BlockSpec((None, nres, LANES, 3 * A_OUT), lambda b, r, i: (b, r, jnp.maximum(i * nq - 1, 0), 0))
    shp = (batch, dilation, n, A_OUT)
    return pl.pallas_call(
        _dil_kernel,
        out_shape=[jax.ShapeDtypeStruct(shp, BF16), jax.ShapeDtypeStruct(shp, F32)],
        grid=(batch, dilation // nres, nblk // nq),
        in_specs=[cur(3 * A_OUT), prev, _const_spec(bias.shape)],
        out_specs=[cur(A_OUT), cur(A_OUT)],
        compiler_params=_cparams(("parallel", "parallel", "parallel")),
        name=f"dilated_d{dilation}",
    )(qkv, qkv, bias)


def _fold_rows(x, op):
    r, c = x.shape
    return op(x.reshape(r // 64, 64, c), axis=0) if r > 64 else x


def _reduce_rows(x, op):
    x = _fold_rows(x, op)
    x = op(x.reshape(8, 8, x.shape[1]), axis=0)
    return op(x, axis=0, keepdims=True)


def _bias_tile(tab_ref, head, qblk, kblk):
    delta = qblk - kblk
    idx = jnp.where(delta < 0, MASKED, jnp.minimum(delta, FAR))
    if head is None:
        return tab_ref[idx]
    return tab_ref[head, idx]


def _diff_kernel(qt_ref, k_ref, vt_ref, tab_ref, lam_ref, gn_ref, o_ref):
    tq = qt_ref.shape[1]
    tk = TOKEN_TILE
    qi = pl.program_id(2)
    qt = qt_ref[...]
    row = lax.broadcasted_iota(jnp.int32, qt.shape, 0)
    q12 = jnp.concatenate([jnp.where(row < HEAD_DIM, qt, jnp.zeros_like(qt)),
                           jnp.where(row >= HEAD_DIM, qt, jnp.zeros_like(qt))], axis=1)
    nqb = tq // LANES
    nkb = tk // LANES

    def logits(c):
        kc = k_ref[pl.ds(pl.multiple_of(c * tk, tk), tk), :]
        bias = jnp.concatenate(
            [jnp.concatenate([_bias_tile(tab_ref, None, qi * nqb + iq, c * nkb + jk)
                              for iq in range(nqb)] * 2, axis=1) for jk in range(nkb)], axis=0)
        return jnp.dot(kc, q12, preferred_element_type=F32) + bias

    ones_rows = jnp.ones((ONES_ROWS, tk), BF16)

    def weigh(c, p):
        return jnp.dot(jnp.concatenate([vt_ref[c], ones_rows], axis=0), p.astype(BF16), preferred_element_type=F32)

    def exact_step(c, carry):
        m, acc = carry
        s = logits(c)
        m_new = jnp.maximum(m, _reduce_rows(s, jnp.max))
        alpha = jnp.exp2(m - m_new)
        return m_new, alpha * acc + weigh(c, jnp.exp2(s - m_new))

    def lagged_update(c, s, m, acc, jump):
        p = jnp.exp2(s - m)
        top =_reduce_rows(s, jnp.max)
        acc = acc + weigh(c, p)
        m_new = jnp.maximum(m, top)
        alpha = jnp.exp2(m - m_new)
        return m_new, alpha * acc, jnp.maximum(jump, top - m)

    nch = ((qi + 1) * tq + tk - 1) // tk
    zero = jnp.zeros((1, 2 * tq), F32)
    acc0 = jnp.zeros((B_V_DIM + ONES_ROWS, 2 * tq), F32)

    def opening(count):
        ss = [logits(c) for c in range(count)]
        state = (jnp.max(ss[0][0:SUBLANES], axis=0, keepdims=True), acc0, zero)
        for c in range(count):
            state = lagged_update(c, ss[c], *state)
        return state

    def lagged_pair(c, carry):
        s_a, s_b = logits(c), logits(c + 1)
        return lagged_update(c + 1, s_b, *lagged_update(c, s_a, *carry))

    odd = nch % 2
    state = lax.cond(odd == 1, lambda: opening(1), lambda: opening(2))
    _, acc, jump = lax.fori_loop(0, (nch - 2 + odd) // 2,
                                 lambda i, carry: lagged_pair(2 - odd + 2 * i, carry), state)
    acc = lax.cond(jnp.max(jump) > MAX_LAG,
                   lambda: lax.fori_loop(0, nch, exact_step, (jnp.full((1, 2 * tq), M_INIT, F32), acc0))[1],
                   lambda: acc)
    a1, a2 = acc[:B_V_DIM, :tq], acc[:B_V_DIM, tq:]
    l1, l2 = acc[B_V_DIM:B_V_DIM + 1, :tq], acc[B_V_DIM:B_V_DIM + 1, tq:]

    lv = lam_ref[...]
    lam = (jnp.exp(jnp.sum(lv[0:1] * lv[1:2], axis=1, keepdims=True))
           - jnp.exp(jnp.sum(lv[2:3] * lv[3:4], axis=1, keepdims=True)) + lv[4:5, 0:1])
    o = a1 / l1 - lam * (a2 / l2)
    ms = jnp.mean(o * o, axis=0, keepdims=True)
    o = o * lax.rsqrt(ms + RMS_EPS) * gn_ref[...]
    o_ref[...] = o.T.astype(BF16)


def _diff_attention(bqt, bk, bvt, tab, lam_rows, gn, batch, t):
    tq = B_Q_TILE
    tk = TOKEN_TILE
    per = tk // tq
    nkt = t // tk
    bqt = bqt.reshape(batch, nkt, B_HEADS * LANES, tk)
    bvt = bvt.reshape(batch, nkt, B_OUT, tk)
    bk = bk.reshape(batch, t, B_HEADS * LANES)
    return pl.pallas_call(
        _diff_kernel,
        out_shape=jax.ShapeDtypeStruct((batch, t, B_OUT), BF16),
        grid=(batch, B_HEADS, t // tq),
        in_specs=[pl.BlockSpec((None, None, LANES, tq), lambda b, h, i: (b, i // per, h, i % per)),
                  pl.BlockSpec((None, t, LANES), lambda b, h, i: (b, 0, h)),
                  pl.BlockSpec((None, nkt, B_V_DIM, tk), lambda b, h, i: (b, 0, h, 0)),
                  pl.BlockSpec((None, FAR + 2, LANES, LANES), lambda b, h, i: (h, 0, 0, 0)),
                  _const_spec(lam_rows.shape), _const_spec(gn.shape)],
        out_specs=pl.BlockSpec((None, tq, B_V_DIM), lambda b, h, i: (b, i, h)),
        compiler_params=_cparams(("parallel", "parallel", "arbitrary")),
        name="diff_attention",
    )(bqt, bk, bvt, tab, lam_rows, gn).reshape(batch * t, B_OUT)


def _dsa_kernel(iqt_ref, iwt_ref, cqt_ref, k_ref, vt_ref, tab_ref, tri_ref, o_ref, s_ref, *, k_sel):
    tk = TOKEN_TILE
    nkb = tk // LANES
    qw = o_ref.shape[0]
    nqb = qw // LANES
    qi = pl.program_id(1)
    nch = ((qi + 1) * qw + tk - 1) // tk
    qpos = qi * qw + lax.broadcasted_iota(jnp.int32, (1, qw), 1)
    zeros = jnp.zeros((HEAD_DIM, qw), BF16)
    iq = iqt_ref[...]
    w = iwt_ref[...]
    iq_all = jnp.concatenate([jnp.concatenate([zeros, iq[h * IDX_DIM:(h + 1) * IDX_DIM]], axis=0)
                              for h in range(IDX_HEADS)], axis=1)
    cq = cqt_ref[...]
    cq_all = jnp.concatenate([jnp.concatenate([cq[h * HEAD_DIM:(h + 1) * HEAD_DIM], zeros], axis=0)
                              for h in range(C_HEADS)], axis=1)

    def chunk(c):
        return pl.ds(pl.multiple_of(c * tk, tk), tk)

    def raw_scores(c):
        return jnp.dot(k_ref[chunk(c), :], iq_all, preferred_element_type=F32)

    def score_chunk(c, raw, mn, mx, last):
        acc = w[0:1, :] * jnp.maximum(raw[:, 0:qw], 0.0)
        for h in range(1, IDX_HEADS):
            acc = acc + w[h:h + 1, :] * jnp.maximum(raw[:, h * qw:(h + 1) * qw], 0.0)
        if last:
            kpos = c * tk + lax.broadcasted_iota(jnp.int32, (tk, qw), 0)
            causal = kpos <= qpos
            s_ref[chunk(c), :] = jnp.where(causal, acc, NEG)
            mn = jnp.minimum(mn, _fold_rows(jnp.where(causal, acc, BIG), jnp.min))
            mx = jnp.maximum(mx, _fold_rows(jnp.where(causal, acc, NEG), jnp.max))
        else:
            s_ref[chunk(c), :] = acc
            mn = jnp.minimum(mn, _fold_rows(acc, jnp.min))
            mx = jnp.maximum(mx, _fold_rows(acc, jnp.max))
        return mn, mx

    def score_pair(c, carry, last):
        raw_a, raw_b = raw_scores(c), raw_scores(c + 1)
        return score_chunk(c + 1, raw_b, *score_chunk(c, raw_a, *carry, last=False), last=last)

    carry = lax.fori_loop(0, (nch - 1) // 2, lambda i, carry: score_pair(2 * i, carry, last=False),
                          (jnp.full((64, qw), BIG, F32), jnp.full((64, qw), NEG, F32)))
    mn, mx = lax.cond((nch - 1) % 2 == 1,
                      lambda st: score_pair(nch - 2, st, last=True),
                      lambda st: score_chunk(nch - 1, raw_scores(nch - 1), *st, last=True), carry)
    lo, hi = _reduce_rows(mn, jnp.min), _reduce_rows(mx, jnp.max)

    def count_gt(thr):
        def body(c, gt):
            return gt + _fold_rows(jnp.where(s_ref[chunk(c), :] > thr, 1.0, 0.0), jnp.sum)
        return _reduce_rows(lax.fori_loop(0, nch, body, jnp.zeros((64, qw), F32)), jnp.sum)

    def largest_upto(bound):
        def body(c, mx):
            s = s_ref[chunk(c), :]
            return jnp.maximum(mx, _fold_rows(jnp.where(s <= bound, s, NEG), jnp.max))
        return _reduce_rows(lax.fori_loop(0, nch, body, jnp.full((64, qw), NEG, F32)), jnp.max)

    def next_below_and_multiplicity(cand):
        def body(c, carry):
            mx, eq = carry
            s = s_ref[chunk(c), :]
            mx = jnp.maximum(mx, _fold_rows(jnp.where(s < cand, s, NEG), jnp.max))
            eq = eq + _fold_rows(jnp.where(s == cand, 1.0, 0.0), jnp.sum)
            return mx, eq
        mx, eq = lax.fori_loop(0, nch, body, (jnp.full((64, qw), NEG, F32), jnp.zeros((64, qw), F32)))
        return _reduce_rows(mx, jnp.max), _reduce_rows(eq, jnp.sum)

    kf = float(k_sel)
    need = qpos >= k_sel

    def bisect(_, carry):
        lo, hi, above = carry
        mid = lo + (hi - lo) * 0.5
        cnt = count_gt(mid)
        below = cnt < kf
        return jnp.where(below, lo, mid), jnp.where(below, mid, hi), jnp.where(below, cnt, above)

    _, hi, above = lax.fori_loop(0, N_BISECT, bisect, (lo - 1.0, hi, jnp.zeros((1, qw), F32)))

    def walk_cond(carry):
        _, _, ge = carry
        return jnp.max(jnp.where(need & (ge < kf), 1.0, 0.0)) > 0.0

    def walk_body(carry):
        cand, gt, ge = carry
        active = ge < kf
        nxt, mult = next_below_and_multiplicity(cand)
        ge_new = gt + mult
        moved = active & (ge_new < kf)
        return jnp.where(moved, nxt, cand), jnp.where(moved, ge_new, gt), jnp.where(active, ge_new, ge)

    cand, gt, ge = lax.while_loop(walk_cond, walk_body, (largest_upto(hi), above, above))
    thr = jnp.where(need, cand, THR_ALL)
    want_eq = jnp.where(need, kf - gt, 0.0)

    any_tie = jnp.max(jnp.where(need & (ge > kf), 1.0, 0.0)) > 0.0

    def mark_with_ties(c, eq_seen):
        s = s_ref[chunk(c), :]
        eq = jnp.where(s == thr, 1.0, 0.0)
        rank = eq_seen + jnp.dot(tri_ref[...], eq.astype(BF16), preferred_element_type=F32)
        keep = jnp.where(s > thr, 1.0, jnp.where(rank <= want_eq, eq, 0.0))
        s_ref[chunk(c), :] = jnp.where(keep > 0.5, 0.0, NEG)
        return eq_seen + _reduce_rows(eq, jnp.sum)

    def mark_no_ties(c, carry):
        s_ref[chunk(c), :] = jnp.where(s_ref[chunk(c), :] >= thr, 0.0, NEG)
        return carry

    @pl.when(any_tie)
    def _():
        lax.fori_loop(0, nch, mark_with_ties, jnp.zeros((1, qw), F32))

    @pl.when(jnp.logical_not(any_tie))
    def _():
        lax.fori_loop(0, nch, mark_no_ties, 0)

    def masked_logits(c):
        sel = s_ref[chunk(c), :]
        bias = jnp.concatenate(
            [jnp.concatenate([jnp.concatenate([_bias_tile(tab_ref, h, qi * nqb + iq, c * nkb + jk)
                                               for iq in range(nqb)], axis=1) for jk in range(nkb)], axis=0) + sel
             for h in range(C_HEADS)], axis=1)
        return jnp.dot(k_ref[chunk(c), :], cq_all, preferred_element_type=F32) + bias

    ones_rows = jnp.ones((ONES_ROWS, tk), BF16)

    def weigh(c, p):
        return jnp.dot(jnp.concatenate([vt_ref[c], ones_rows], axis=0), p.astype(BF16), preferred_element_type=F32)

    def exact_update(c, lg, m, acc):
        m_new = jnp.maximum(m, _reduce_rows(lg, jnp.max))
        alpha = jnp.exp2(m - m_new)
        return m_new, alpha * acc + weigh(c, jnp.exp2(lg - m_new))

    def lagged_update(c, lg, m, acc, jump):
        p = jnp.exp2(lg - m)
        top = _reduce_rows(lg, jnp.max)
        acc = acc + weigh(c, p)
        m_new = jnp.maximum(m, top)
        alpha = jnp.exp2(m - m_new)
        return m_new, alpha * acc, jnp.maximum(jump, top - m)

    wide = C_HEADS * qw
    init = (jnp.full((1, wide), M_INIT, F32), jnp.zeros((HEAD_DIM + ONES_ROWS, wide), F32))

    def opening(count):
        lgs = [masked_logits(c) for c in range(count)]
        state = exact_update(0, lgs[0], *init) + (jnp.zeros((1, wide), F32),)
        return lagged_update(1, lgs[1], *state) if count == 2 else state

    def lagged_pair(c, carry):
        lg_a, lg_b = masked_logits(c), masked_logits(c + 1)
        return lagged_update(c + 1, lg_b, *lagged_update(c, lg_a, *carry))

    odd = nch % 2
    state = lax.cond(odd == 1, lambda: opening(1), lambda: opening(2))
    _, acc, jump = lax.fori_loop(0, (nch - 2 + odd) // 2,
                                 lambda i, carry: lagged_pair(2 - odd + 2 * i, carry), state)
    acc = lax.cond(jnp.max(jump) > MAX_LAG,
                   lambda: lax.fori_loop(0, nch, lambda c, st: exact_update(c, masked_logits(c), *st), init)[1],
                   lambda: acc)
    o = acc[:HEAD_DIM] / acc[HEAD_DIM:HEAD_DIM + 1]
    ot = jnp.concatenate([o[:, h * qw:(h + 1) * qw] for h in range(C_HEADS)], axis=0)
    o_ref[...] = ot.T.astype(BF16)


def _dsa_attention(iqt, iwt, cqt, ck, cvt, tab, batch, t):
    tk = TOKEN_TILE
    qw = C_Q_TILE
    per = tk // qw
    nkt = t // tk
    k_sel = min(TOPK_MAX, t // 4)
    iqt = iqt.reshape(batch, nkt, IDX_HEADS * IDX_DIM, tk)
    iwt = iwt.reshape(batch, nkt, 16, tk)
    cqt = cqt.reshape(batch, nkt, C_OUT, tk)
    cvt = cvt.reshape(batch, nkt, HEAD_DIM, tk)
    ck = ck.reshape(batch, t, LANES)
    r = np.arange(tk)
    tri = jnp.asarray(r[:, None] >= r[None, :], BF16)
    qblock = lambda rows: pl.BlockSpec((None, None, rows, qw), lambda b, i: (b, i // per, 0, i % per))
    return pl.pallas_call(
        functools.partial(_dsa_kernel, k_sel=k_sel),
        out_shape=jax.ShapeDtypeStruct((batch, t, C_OUT), BF16),
        grid=(batch, t // qw),
        in_specs=[qblock(IDX_HEADS * IDX_DIM), qblock(16), qblock(C_OUT),
                  pl.BlockSpec((None, t, LANES), lambda b, i: (b, 0, 0)),
                  pl.BlockSpec((None, nkt, HEAD_DIM, tk), lambda b, i: (b, 0, 0, 0)),
                  _const_spec(tab.shape), _const_spec(tri.shape)],
        out_specs=pl.BlockSpec((None, qw, C_OUT), lambda b, i: (b, i, 0)),
        scratch_shapes=[pltpu.VMEM((t, qw), F32)],
        compiler_params=_cparams(("parallel", "arbitrary")),
        name="dsa_attention",
    )(iqt, iwt, cqt, ck, cvt, tab, tri).reshape(batch * t, C_OUT)


def _merge_kernel(x_ref, *refs):
    ng = len(DIL_GROUPS)
    a_refs = refs[:2 * ng]
    ob_ref, oc_ref, gate_ref, wa_ref, wb_ref, wc_ref, wo_ref, out_ref = refs[2 * ng:-1]
    shuffle_ref = refs[-1]
    tm, d = x_ref.shape

    def token_order(ref, slot):
        dil = ref.shape[0]
        if dil == 1:
            return ref[0].astype(F32)
        halves = range(A_OUT // LANES)
        for r in range(dil):
            for half in halves:
                shuffle_ref[slot, half, pl.ds(r, tm // dil, stride=dil), :] = (
                    ref[r, :, half * LANES:(half + 1) * LANES].astype(F32))
        return jnp.concatenate([shuffle_ref[slot, half] for half in halves], axis=1)

    outs = [token_order(a_refs[2 * g], 2 * g) for g in range(ng)]
    lses = [token_order(a_refs[2 * g + 1], 2 * g + 1) for g in range(ng)]
    top = functools.reduce(jnp.maximum, lses)
    es = [jnp.exp(lse - top) for lse in lses]
    num = sum(e * o for e, o in zip(es, outs))
    oa = (num / sum(es)).astype(BF16)
    y = gate_ref[:, 0:d].astype(F32) * jnp.dot(oa, wa_ref[...], preferred_element_type=F32)
    y = y + gate_ref[:, d:2 * d].astype(F32) * jnp.dot(ob_ref[...], wb_ref[...], preferred_element_type=F32)
    y = y + gate_ref[:, 2 * d:3 * d].astype(F32) * jnp.dot(oc_ref[...], wc_ref[...], preferred_element_type=F32)
    out_ref[...] = x_ref[...] + jnp.dot(y.astype(BF16), wo_ref[...], preferred_element_type=F32)


def _merge(x, a_parts, ob, oc, gates, wa, wb, wc, wo, t):
    n, d = x.shape
    tm = TOKEN_TILE
    per_batch = t // tm
    tok = lambda c: pl.BlockSpec((tm, c), lambda i: (i, 0))
    by_residue = lambda dil: pl.BlockSpec((None, dil, tm // dil, A_OUT),
                                          lambda i: (i // per_batch, 0, i % per_batch, 0))
    ws = [w.astype(BF16) for w in (wa, wb, wc, wo)]
    return pl.pallas_call(
        _merge_kernel,
        out_shape=jax.ShapeDtypeStruct((n, d), F32),
        grid=(n // tm,),
        in_specs=[tok(d)] + [by_residue(z.shape[1]) for z in a_parts] + [tok(B_OUT), tok(C_OUT), tok(3 * d)]
                 + [_weight_spec(w.shape) for w in ws],
        out_specs=tok(d),
        scratch_shapes=[pltpu.VMEM((len(a_parts), A_OUT // LANES, tm, LANES), F32)],
        compiler_params=_cparams(("parallel",)),
        name="merge",
    )(x, *a_parts, ob, oc, gates, *ws)


def _token_mixer(x, batch, t, layer, mix_norm, w_in, qk_gain, diff_lambda, diff_out_norm,
                 w_branch_a, w_branch_b, w_branch_c, w_out, band_tabs, tab_b, tab_c):
    ng = len(DIL_GROUPS)
    outs = _project(x, mix_norm, w_in, qk_gain, batch, t)
    a_in, (bk, ck, gates, bqt, bvt, cqt, cvt, iqt, iwt) = outs[:ng], outs[ng:]
    a_parts = []
    for g, (_, dilation) in enumerate(DIL_GROUPS):
        a_parts += _dilated_group(a_in[g], band_tabs[g], dilation)
    lam_init = 0.8 - 0.6 * np.exp(-0.3 * layer)
    lam_rows = jnp.concatenate([diff_lambda.astype(F32), jnp.full((4, HEAD_DIM), lam_init, F32)], axis=0)
    gn = jnp.broadcast_to((diff_out_norm.astype(F32) * (1.0 - lam_init))[:, None], (B_V_DIM, B_Q_TILE))
    ob = _diff_attention(bqt, bk, bvt, tab_b, lam_rows, gn, batch, t)
    oc = _dsa_attention(iqt, iwt, cqt, ck, cvt, tab_c, batch, t)
    return _merge(x, a_parts, ob, oc, gates, w_branch_a, w_branch_b, w_branch_c, w_out, t)


def kernel(x, rel_bias, ffn1_norm, ffn1_w_gate, ffn1_w_up, ffn1_w_down, mix_norm, w_in, qk_gain,
           diff_lambda, diff_out_norm, w_branch_a, w_branch_b, w_branch_c, w_out,
           ffn2_norm, ffn2_w_gate, ffn2_w_up, ffn2_w_down):
    batch, t, d = x.shape
    depth = w_in.shape[0]
    assert t % (DIL_GROUPS[-1][1] * LANES) == 0 and t % TOKEN_TILE == 0
    band_tabs = [_band_tables(rel_bias[:, g * A_GROUP_HEADS:(g + 1) * A_GROUP_HEADS], dil)
                 for g, (_, dil) in enumerate(DIL_GROUPS)]
    tab_b = _toeplitz_tables(rel_bias[:, A_HEADS:A_HEADS + B_HEADS] * LOG2E)
    tab_c = _toeplitz_tables(rel_bias[:, A_HEADS + B_HEADS:] * LOG2E)
    h = x.reshape(batch * t, d).astype(F32)
    for i in range(depth):
        h = _ffn(h, ffn1_norm[i], ffn1_w_gate[i], ffn1_w_up[i], ffn1_w_down[i])
        h = _token_mixer(h, batch, t, i, mix_norm[i], w_in[i], qk_gain[i], diff_lambda[i], diff_out_norm[i],
                         w_branch_a[i], w_branch_b[i], w_branch_c[i], w_out[i], band_tabs, tab_b, tab_c)
        h = _ffn(h, ffn2_norm[i], ffn2_w_gate[i], ffn2_w_up[i], ffn2_w_down[i])
    return h.reshape(batch, t, d).astype(x.dtype)
```

```python
import functools

import numpy as np
import jax
import jax.numpy as jnp
from jax import lax
from jax.experimental import pallas as pl
from jax.experimental.pallas import tpu as pltpu

F32 = jnp.float32
BF16 = jnp.bfloat16

HEAD_DIM = 64
DIL_GROUPS = ((128, 1), (512, 4), (2048, 16))
A_GROUP_HEADS = 4
A_HEADS = A_GROUP_HEADS * len(DIL_GROUPS)
A_OUT = A_GROUP_HEADS * HEAD_DIM
B_HEADS = 4
B_V_DIM = 2 * HEAD_DIM
B_OUT = B_HEADS * B_V_DIM
C_HEADS = 4
C_OUT = C_HEADS * HEAD_DIM
IDX_HEADS = 8
IDX_DIM = 64
TOPK_MAX = 256
NUM_BUCKETS = 32
MAX_DISTANCE = 2048
RMS_EPS = 1e-6
LOG2E = 1.4426950408889634

LANES = 128
SUBLANES = 8
ONES_ROWS = 16
TOKEN_TILE = 512
FFN_TILE = 1024
A_BLOCKS_PER_STEP = 4
B_Q_TILE = 512
C_Q_TILE = 256
MAX_LAG = 60.0
FFN_CHUNK = 256
VMEM_LIMIT = 58 * 1024 * 1024

NEG = -1e30
M_INIT = -1e29
BIG = 1e30
THR_ALL = -1e29
N_BISECT = 14


def _cparams(sem):
    return pltpu.CompilerParams(dimension_semantics=sem, vmem_limit_bytes=VMEM_LIMIT)


def _const_spec(shape):
    nd = len(shape)
    return pl.BlockSpec(shape, lambda *_: (0,) * nd)


def _weight_spec(shape):
    nd = len(shape)
    return pl.BlockSpec(shape, lambda *_: (0,) * nd, pipeline_mode=pl.Buffered(1))


def _rel_bucket_np(dist):
    n = np.maximum(dist, 0)
    max_exact = NUM_BUCKETS // 2
    nf = np.maximum(n, 1).astype(np.float64)
    large = max_exact + (np.log(nf / max_exact) / np.log(MAX_DISTANCE / max_exact)
                         * (NUM_BUCKETS - max_exact)).astype(np.int64)
    large = np.minimum(large, NUM_BUCKETS - 1)
    return np.where(n < max_exact, n, large)


def _far_delta():
    d = 1
    while not np.all(_rel_bucket_np(np.arange(d * LANES - LANES + 1, d * LANES + LANES)) == NUM_BUCKETS - 1):
        d += 1
    return d


FAR = _far_delta()
MASKED = FAR + 1


def _toeplitz(w, n_rows, n_cols):
    period = n_rows + n_cols
    w = jnp.pad(w, ((0, 0), (0, period - w.shape[1])))
    m = jnp.tile(w, (1, n_rows))[:, :n_rows * (period - 1)].reshape(-1, n_rows, period - 1)
    return m[:, :, n_rows - 1:n_rows - 1 + n_cols]


def _bias_by_distance(bias_heads, dist, valid):
    vals = jnp.take(bias_heads.astype(F32), jnp.asarray(_rel_bucket_np(dist), jnp.int32), axis=0).T
    return jnp.where(jnp.asarray(valid)[None], vals, NEG)


def _toeplitz_tables(bias_heads):
    n_cols = (FAR + 1) * LANES
    dist = np.arange(LANES - 1 + n_cols) - (LANES - 1)
    tiles = _toeplitz(_bias_by_distance(bias_heads, dist, dist >= 0), LANES, n_cols)
    tiles = tiles.reshape(-1, LANES, FAR + 1, LANES).transpose(0, 2, 1, 3)
    masked = jnp.full((tiles.shape[0], 1, LANES, LANES), NEG, F32)
    return jnp.concatenate([tiles, masked], axis=1)


def _band_tables(bias_heads, dilation):
    wn = LANES
    sub = np.arange(3 * wn - 1) - (wn - 1)
    w = _bias_by_distance(bias_heads, sub * dilation, (sub >= 0) & (sub <= wn))
    later = jnp.flip(_toeplitz(w, wn, 2 * wn), axis=(1, 2))
    first = jnp.where(jnp.asarray(np.arange(2 * wn) >= wn)[None, None], later, NEG)
    return jnp.stack([first, later])


def _ffn_kernel(x_ref, g_ref, wg_ref, wu_ref, wd_ref, o_ref, acc_ref):
    x = x_ref[...]
    ms = jnp.mean(x * x, axis=-1, keepdims=True)
    h = (x * lax.rsqrt(ms + RMS_EPS) * g_ref[...]).astype(BF16)
    acc_ref[...] = jnp.zeros_like(acc_ref)

    def body(c, carry):
        cols = pl.ds(pl.multiple_of(c * FFN_CHUNK, FFN_CHUNK), FFN_CHUNK)
        g = jnp.dot(h, wg_ref[:, cols], preferred_element_type=F32)
        u = jnp.dot(h, wu_ref[:, cols], preferred_element_type=F32)
        a = (g * jax.nn.sigmoid(g) * u).astype(BF16)
        acc_ref[...] += jnp.dot(a, wd_ref[cols, :], preferred_element_type=F32)
        return carry

    lax.fori_loop(0, wg_ref.shape[1] // FFN_CHUNK, body, 0)
    o_ref[...] = x + 0.5 * acc_ref[...]


def _ffn(x, gain, w_gate, w_up, w_down):
    n, d = x.shape
    f = w_gate.shape[1]
    tm = FFN_TILE
    return pl.pallas_call(
        _ffn_kernel,
        out_shape=jax.ShapeDtypeStruct((n, d), F32),
        grid=(n // tm,),
        in_specs=[pl.BlockSpec((tm, d), lambda i: (i, 0)),
                  _const_spec((1, d)), _weight_spec((d, f)), _weight_spec((d, f)), _weight_spec((f, d))],
        out_specs=pl.BlockSpec((tm, d), lambda i: (i, 0)),
        scratch_shapes=[pltpu.VMEM((tm, d), F32)],
        compiler_params=_cparams(("parallel",)),
        name="ffn",
    )(x, gain.reshape(1, d).astype(F32), w_gate.astype(BF16), w_up.astype(BF16), w_down.astype(BF16))


S_AQ, S_AK, S_AV = 0, 768, 1536
S_BK, S_CK, S_GATE, S_END = 2304, 2816, 2944, 6016
T_BQ, T_BV, T_CQ, T_CV, T_IQ, T_IW, T_END = 0, 512, 1024, 1280, 1344, 1856, 1872


def _proj_kernel(x_ref, g_ref, ws_ref, wt_ref, bd_ref, gs_ref, gt_ref, *refs):
    ng = len(DIL_GROUPS)
    a_refs = refs[:3 * ng]
    bk_ref, ck_ref, gate_ref, bqt_ref, bvt_ref, cqt_ref, cvt_ref, iqt_ref, iwt_ref = refs[3 * ng:-1]
    shuffle_ref = refs[-1]
    tm = x_ref.shape[0]

    def store_by_residue(y, which):
        for g, (_, dil) in enumerate(DIL_GROUPS):
            out = a_refs[which * ng + g]
            part = y[:, g * A_OUT:(g + 1) * A_OUT]
            if dil == 1:
                out[0] = part.astype(BF16)
            else:
                for half in range(A_OUT // LANES):
                    shuffle_ref[half] = part[:, half * LANES:(half + 1) * LANES]
                for r in range(dil):
                    out[r] = jnp.concatenate(
                        [shuffle_ref[half, pl.ds(r, tm // dil, stride=dil), :] for half in range(A_OUT // LANES)],
                        axis=1).astype(BF16)

    x = x_ref[...]
    ms = jnp.mean(x * x, axis=-1, keepdims=True)
    h = (x * lax.rsqrt(ms + RMS_EPS) * g_ref[...]).astype(BF16)
    bd = bd_ref[...]

    def dot_s(c0, c1):
        return jnp.dot(h, ws_ref[:, c0:c1], preferred_element_type=F32)

    def head_inv_rms(y):
        width = bd.shape[0]
        outs = []
        for c0 in range(0, y.shape[1], width):
            sq = y[:, c0:c0 + width]
            n = sq.shape[1]
            msq = jnp.dot((sq * sq).astype(BF16), bd[:n, :n], preferred_element_type=F32)
            outs.append(lax.rsqrt(msq + RMS_EPS))
        return outs[0] if len(outs) == 1 else jnp.concatenate(outs, axis=1)

    y = dot_s(S_AQ, S_AK)
    store_by_residue(y * head_inv_rms(y) * gs_ref[:, 0:768], 0)
    y = dot_s(S_AK, S_AV)
    store_by_residue(y * head_inv_rms(y) * gs_ref[:, 768:1536], 1)
    store_by_residue(dot_s(S_AV, S_BK), 2)
    y = dot_s(S_BK, S_CK)
    bk_ref[...] = (y * head_inv_rms(y) * gs_ref[:, 1536:2048]).astype(BF16)
    y = dot_s(S_CK, S_GATE)
    lane = lax.broadcasted_iota(jnp.int32, y.shape, 1)
    inv = jnp.where(lane < HEAD_DIM, head_inv_rms(y), 1.0)
    ck_ref[...] = (y * inv * gs_ref[:, 2048:2176]).astype(BF16)
    for c in range(3):
        y = dot_s(S_GATE + c * 1024, S_GATE + (c + 1) * 1024)
        gate_ref[:, c * 1024:(c + 1) * 1024] = jax.nn.sigmoid(y).astype(BF16)

    def dot_t(r0, r1):
        return lax.dot_general(wt_ref[r0:r1, :], h, (((1,), (1,)), ((), ())),
                               preferred_element_type=F32)

    def norm_t(y, gain):
        r = y.shape[0] // HEAD_DIM
        y3 = y.reshape(r, HEAD_DIM, tm)
        msq = jnp.mean(y3 * y3, axis=1, keepdims=True)
        return (y3 * lax.rsqrt(msq + RMS_EPS)).reshape(r * HEAD_DIM, tm) * gain

    bqt_ref[...] = norm_t(dot_t(T_BQ, T_BV), gt_ref[0:512, :]).astype(BF16)
    bvt_ref[...] = dot_t(T_BV, T_CQ).astype(BF16)
    cqt_ref[...] = norm_t(dot_t(T_CQ, T_CV), gt_ref[512:768, :]).astype(BF16)
    cvt_ref[...] = dot_t(T_CV, T_IQ).astype(BF16)
    iqt_ref[...] = dot_t(T_IQ, T_IW).astype(BF16)
    iwt_ref[...] = dot_t(T_IW, T_END) * (IDX_HEADS ** -0.5 * IDX_DIM ** -0.5)


def _proj_weights(w_in, qk_gain, tm):
    d = w_in.shape[0]
    o = 0
    a_qkv = w_in[:, o:o + 3 * A_HEADS * HEAD_DIM].reshape(d, 3, A_HEADS * HEAD_DIM)
    o += 3 * A_HEADS * HEAD_DIM
    b_qk = w_in[:, o:o + 4 * B_HEADS * HEAD_DIM].reshape(d, 4, B_HEADS, HEAD_DIM)
    o += 4 * B_HEADS * HEAD_DIM
    b_v = w_in[:, o:o + B_OUT]
    o += B_OUT
    c_q = w_in[:, o:o + C_OUT]
    c_k = w_in[:, o + C_OUT:o + C_OUT + HEAD_DIM]
    c_v = w_in[:, o + C_OUT + HEAD_DIM:o + C_OUT + 2 * HEAD_DIM]
    o += C_OUT + 2 * HEAD_DIM
    i_q = w_in[:, o:o + IDX_HEADS * IDX_DIM]
    i_k = w_in[:, o + IDX_HEADS * IDX_DIM:o + IDX_HEADS * IDX_DIM + IDX_DIM]
    i_w = w_in[:, o + IDX_HEADS * IDX_DIM + IDX_DIM:o + IDX_HEADS * IDX_DIM + IDX_DIM + IDX_HEADS]
    o += IDX_HEADS * IDX_DIM + IDX_DIM + IDX_HEADS
    gates = w_in[:, o:]
    b_k = jnp.stack([b_qk[:, 2], b_qk[:, 3]], axis=2).reshape(d, 2 * B_HEADS * HEAD_DIM)
    b_q = jnp.stack([b_qk[:, 0], b_qk[:, 1]], axis=2).reshape(d, 2 * B_HEADS * HEAD_DIM)
    w_s = jnp.concatenate([a_qkv[:, 0], a_qkv[:, 1], a_qkv[:, 2], b_k, c_k, i_k, gates], axis=1)
    w_t = jnp.concatenate([b_q, b_v, c_q, c_v, i_q, i_w, jnp.zeros((d, 8), w_in.dtype)], axis=1).T
    assert w_s.shape[1] == S_END and w_t.shape[0] == T_END
    scale = HEAD_DIM ** -0.5
    g = qk_gain.astype(F32)
    gs = jnp.concatenate([jnp.tile(g[0, 0] * scale, A_HEADS), jnp.tile(g[0, 1], A_HEADS),
                          jnp.tile(g[1, 1], 2 * B_HEADS), g[2, 1], jnp.ones((IDX_DIM,), F32)])[None]
    gt = jnp.concatenate([jnp.tile(g[1, 0] * (scale * LOG2E), 2 * B_HEADS),
                          jnp.tile(g[2, 0] * (scale * LOG2E), C_HEADS)])
    gt = jnp.broadcast_to(gt[:, None], (gt.shape[0], tm))
    return w_s.astype(BF16), w_t.astype(BF16), gs, gt


MXU_TILE = 256


def _head_block_diag():
    r = np.arange(MXU_TILE)
    return jnp.asarray((r[:, None] // HEAD_DIM == r[None, :] // HEAD_DIM) / HEAD_DIM, BF16)


def _project(x, gain, w_in, qk_gain, batch, t):
    n, d = x.shape
    tm = TOKEN_TILE
    nt = n // tm
    per_batch = t // tm
    w_s, w_t, gs, gt = _proj_weights(w_in, qk_gain, tm)
    tok = lambda c: pl.BlockSpec((tm, c), lambda i: (i, 0))
    feat = lambda r: pl.BlockSpec((None, r, tm), lambda i: (i, 0, 0))
    a_shapes, a_specs = [], []
    for _ in range(3):
        for _, dil in DIL_GROUPS:
            a_shapes.append(jax.ShapeDtypeStruct((batch, dil, t // dil, A_OUT), BF16))
            a_specs.append(pl.BlockSpec((None, dil, tm // dil, A_OUT),
                                        lambda i: (i // per_batch, 0, i % per_batch, 0)))
    out_shape = a_shapes + [
        jax.ShapeDtypeStruct((n, 512), BF16), jax.ShapeDtypeStruct((n, 128), BF16),
        jax.ShapeDtypeStruct((n, 3072), BF16),
        jax.ShapeDtypeStruct((nt, 512, tm), BF16), jax.ShapeDtypeStruct((nt, 512, tm), BF16),
        jax.ShapeDtypeStruct((nt, 256, tm), BF16), jax.ShapeDtypeStruct((nt, 64, tm), BF16),
        jax.ShapeDtypeStruct((nt, 512, tm), BF16), jax.ShapeDtypeStruct((nt, 16, tm), F32)]
    out_specs = a_specs + [tok(512), tok(128), tok(3072),
                           feat(512), feat(512), feat(256), feat(64), feat(512), feat(16)]
    return pl.pallas_call(
        _proj_kernel,
        out_shape=out_shape,
        grid=(nt,),
        in_specs=[tok(d), _const_spec((1, d)), _weight_spec(w_s.shape), _weight_spec(w_t.shape),
                  _const_spec((MXU_TILE, MXU_TILE)), _const_spec(gs.shape), _const_spec(gt.shape)],
        out_specs=out_specs,
        scratch_shapes=[pltpu.VMEM((A_OUT // LANES, tm, LANES), F32)],
        compiler_params=_cparams(("parallel",)),
        name="proj",
    )(x, gain.reshape(1, d).astype(F32), w_s, w_t, _head_block_diag(), gs, gt)


def _dil_kernel(q_ref, kp_ref, kc_ref, vp_ref, vc_ref, bias_ref, o_ref, lse_ref):
    nres, rows, _ = q_ref.shape
    nq = rows // LANES
    qi = pl.program_id(2)
    lane = lax.broadcasted_iota(jnp.int32, (LANES, A_OUT), 1) // HEAD_DIM
    mine = [lane == h for h in range(A_GROUP_HEADS)]
    blocks = [slice(jb * LANES, (jb + 1) * LANES) for jb in range(nq)]
    work = [(r, jb) for r in range(nres) for jb in range(nq)]

    def band(prev_ref, cur_ref, r, jb):
        if jb == 0:
            return jnp.concatenate([prev_ref[r], cur_ref[r, blocks[0], :]], axis=0)
        return cur_ref[r, (jb - 1) * LANES:(jb + 1) * LANES, :]

    logits = []
    for r, jb in work:
        q = q_ref[r, blocks[jb], :]
        q4 = jnp.concatenate([jnp.where(mine[h], q, jnp.zeros_like(q)) for h in range(A_GROUP_HEADS)], axis=0)
        s = lax.dot_general(q4, band(kp_ref, kc_ref, r, jb), (((1,), (1,)), ((), ())), preferred_element_type=F32)
        bias = bias_ref[jnp.minimum(qi, 1) if jb == 0 else 1]
        logits.append(s + bias.reshape(A_GROUP_HEADS * LANES, 2 * LANES))
    probs, stats = [], []
    for s in logits:
        m = jnp.max(s, axis=1, keepdims=True)
        p = jnp.exp(s - m)
        ssum = jnp.sum(p, axis=1, keepdims=True)
        probs.append(p.astype(BF16))
        stats.append((1.0 / ssum, m + jnp.log(ssum)))
    for (r, jb), p, (inv, lse4) in zip(work, probs, stats):
        pv = jnp.dot(p, band(vp_ref, vc_ref, r, jb), preferred_element_type=F32)
        o = jnp.zeros((LANES, A_OUT), F32)
        lse = jnp.zeros((LANES, A_OUT), F32)
        for h in range(A_GROUP_HEADS):
            head = slice(h * LANES, (h + 1) * LANES)
            o = jnp.where(mine[h], pv[head] * inv[head], o)
            lse = jnp.where(mine[h], lse4[head], lse)
        o_ref[r, blocks[jb], :] = o.astype(BF16)
        lse_ref[r, blocks[jb], :] = lse


def _dilated_group(aq, ak, av, bias, dilation):
    batch, _, n, _ = aq.shape
    nblk = n // LANES
    nq = min(nblk, A_BLOCKS_PER_STEP)
    nres = min(dilation, A_BLOCKS_PER_STEP // nq)
    qt = nq * LANES
    cur = pl.BlockSpec((None, nres, qt, A_OUT), lambda b, r, i: (b, r, i, 0))
    prev = pl.BlockSpec((None, nres, LANES, A_OUT), lambda b, r, i: (b, r, jnp.maximum(i * nq - 1, 0), 0))
    shp = (batch, dilation, n, A_OUT)
    return pl.pallas_call(
        _dil_kernel,
        out_shape=[jax.ShapeDtypeStruct(shp, BF16), jax.ShapeDtypeStruct(shp, F32)],
        grid=(batch, dilation // nres, nblk // nq),
        in_specs=[cur, prev, cur, prev, cur, _const_spec(bias.shape)],
        out_specs=[cur, cur],
        compiler_params=_cparams(("parallel", "parallel", "parallel")),
        name=f"dilated_d{dilation}",
    )(aq, ak, ak, av, av, bias)


def _fold_rows(x, op):
    r, c = x.shape
    return op(x.reshape(r // 64, 64, c), axis=0) if r > 64 else x


def _reduce_rows(x, op):
    x = _fold_rows(x, op)
    x = op(x.reshape(8, 8, x.shape[1]), axis=0)
    return op(x, axis=0, keepdims=True)


def _bias_tile(tab_ref, head, qblk, kblk):
    delta = qblk - kblk
    idx = jnp.where(delta < 0, MASKED, jnp.minimum(delta, FAR))
    if head is None:
        return tab_ref[idx]
    return tab_ref[head, idx]


def _diff_kernel(qt_ref, k_ref, vt_ref, tab_ref, lam_ref, gn_ref, o_ref):
    tq = qt_ref.shape[1]
    tk = TOKEN_TILE
    qi = pl.program_id(2)
    qt = qt_ref[...]
    row = lax.broadcasted_iota(jnp.int32, qt.shape, 0)
    q12 = jnp.concatenate([jnp.where(row < HEAD_DIM, qt, jnp.zeros_like(qt)),
                           jnp.where(row >= HEAD_DIM, qt, jnp.zeros_like(qt))], axis=1)
    nqb = tq // LANES
    nkb = tk // LANES

    def logits(c):
        kc = k_ref[pl.ds(pl.multiple_of(c * tk, tk), tk), :]
        bias = jnp.concatenate(
            [jnp.concatenate([_bias_tile(tab_ref, None, qi * nqb + iq, c * nkb + jk)
                              for iq in range(nqb)] * 2, axis=1) for jk in range(nkb)], axis=0)
        return jnp.dot(kc, q12, preferred_element_type=F32) + bias

    ones_rows = jnp.ones((ONES_ROWS, tk), BF16)

    def weigh(c, p):
        return jnp.dot(jnp.concatenate([vt_ref[c], ones_rows], axis=0), p.astype(BF16), preferred_element_type=F32)

    def exact_step(c, carry):
        m, acc = carry
        s = logits(c)
        m_new = jnp.maximum(m, _reduce_rows(s, jnp.max))
        alpha = jnp.exp2(m - m_new)
        return m_new, alpha * acc + weigh(c, jnp.exp2(s - m_new))

    def lagged_update(c, s, m, acc, jump):
        p = jnp.exp2(s - m)
        top =_reduce_rows(s, jnp.max)
        acc = acc + weigh(c, p)
        m_new = jnp.maximum(m, top)
        alpha = jnp.exp2(m - m_new)
        return m_new, alpha * acc, jnp.maximum(jump, top - m)

    nch = ((qi + 1) * tq + tk - 1) // tk
    zero = jnp.zeros((1, 2 * tq), F32)
    acc0 = jnp.zeros((B_V_DIM + ONES_ROWS, 2 * tq), F32)

    def opening(count):
        ss = [logits(c) for c in range(count)]
        state = (jnp.max(ss[0][0:SUBLANES], axis=0, keepdims=True), acc0, zero)
        for c in range(count):
            state = lagged_update(c, ss[c], *state)
        return state

    def lagged_pair(c, carry):
        s_a, s_b = logits(c), logits(c + 1)
        return lagged_update(c + 1, s_b, *lagged_update(c, s_a, *carry))

    odd = nch % 2
    state = lax.cond(odd == 1, lambda: opening(1), lambda: opening(2))
    _, acc, jump = lax.fori_loop(0, (nch - 2 + odd) // 2,
                                 lambda i, carry: lagged_pair(2 - odd + 2 * i, carry), state)
    acc = lax.cond(jnp.max(jump) > MAX_LAG,
                   lambda: lax.fori_loop(0, nch, exact_step, (jnp.full((1, 2 * tq), M_INIT, F32), acc0))[1],
                   lambda: acc)
    a1, a2 = acc[:B_V_DIM, :tq], acc[:B_V_DIM, tq:]
    l1, l2 = acc[B_V_DIM:B_V_DIM + 1, :tq], acc[B_V_DIM:B_V_DIM + 1, tq:]

    lv = lam_ref[...]
    lam = (jnp.exp(jnp.sum(lv[0:1] * lv[1:2], axis=1, keepdims=True))
           - jnp.exp(jnp.sum(lv[2:3] * lv[3:4], axis=1, keepdims=True)) + lv[4:5, 0:1])
    o = a1 / l1 - lam * (a2 / l2)
    ms = jnp.mean(o * o, axis=0, keepdims=True)
    o = o * lax.rsqrt(ms + RMS_EPS) * gn_ref[...]
    o_ref[...] = o.T.astype(BF16)


def _diff_attention(bqt, bk, bvt, tab, lam_rows, gn, batch, t):
    tq = B_Q_TILE
    tk = TOKEN_TILE
    per = tk // tq
    nkt = t // tk
    bqt = bqt.reshape(batch, nkt, B_HEADS * LANES, tk)
    bvt = bvt.reshape(batch, nkt, B_OUT, tk)
    bk = bk.reshape(batch, t, B_HEADS * LANES)
    return pl.pallas_call(
        _diff_kernel,
        out_shape=jax.ShapeDtypeStruct((batch, t, B_OUT), BF16),
        grid=(batch, B_HEADS, t // tq),
        in_specs=[pl.BlockSpec((None, None, LANES, tq), lambda b, h, i: (b, i // per, h, i % per)),
                  pl.BlockSpec((None, t, LANES), lambda b, h, i: (b, 0, h)),
                  pl.BlockSpec((None, nkt, B_V_DIM, tk), lambda b, h, i: (b, 0, h, 0)),
                  pl.BlockSpec((None, FAR + 2, LANES, LANES), lambda b, h, i: (h, 0, 0, 0)),
                  _const_spec(lam_rows.shape), _const_spec(gn.shape)],
        out_specs=pl.BlockSpec((None, tq, B_V_DIM), lambda b, h, i: (b, i, h)),
        compiler_params=_cparams(("parallel", "parallel", "arbitrary")),
        name="diff_attention",
    )(bqt, bk, bvt, tab, lam_rows, gn).reshape(batch * t, B_OUT)


def _dsa_kernel(iqt_ref, iwt_ref, cqt_ref, k_ref, vt_ref, tab_ref, tri_ref, o_ref, s_ref, *, k_sel):
    tk = TOKEN_TILE
    nkb = tk // LANES
    qw = o_ref.shape[0]
    nqb = qw // LANES
    qi = pl.program_id(1)
    nch = ((qi + 1) * qw + tk - 1) // tk
    qpos = qi * qw + lax.broadcasted_iota(jnp.int32, (1, qw), 1)
    zeros = jnp.zeros((HEAD_DIM, qw), BF16)
    iq = iqt_ref[...]
    w = iwt_ref[...]
    iq_all = jnp.concatenate([jnp.concatenate([zeros, iq[h * IDX_DIM:(h + 1) * IDX_DIM]], axis=0)
                              for h in range(IDX_HEADS)], axis=1)
    cq = cqt_ref[...]
    cq_all = jnp.concatenate([jnp.concatenate([cq[h * HEAD_DIM:(h + 1) * HEAD_DIM], zeros], axis=0)
                              for h in range(C_HEADS)], axis=1)

    def chunk(c):
        return pl.ds(pl.multiple_of(c * tk, tk), tk)

    def raw_scores(c):
        return jnp.dot(k_ref[chunk(c), :], iq_all, preferred_element_type=F32)

    def score_chunk(c, raw, mn, mx, last):
        acc = w[0:1, :] * jnp.maximum(raw[:, 0:qw], 0.0)
        for h in range(1, IDX_HEADS):
            acc = acc + w[h:h + 1, :] * jnp.maximum(raw[:, h * qw:(h + 1) * qw], 0.0)
        if last:
            kpos = c * tk + lax.broadcasted_iota(jnp.int32, (tk, qw), 0)
            causal = kpos <= qpos
            s_ref[chunk(c), :] = jnp.where(causal, acc, NEG)
            mn = jnp.minimum(mn, _fold_rows(jnp.where(causal, acc, BIG), jnp.min))
            mx = jnp.maximum(mx, _fold_rows(jnp.where(causal, acc, NEG), jnp.max))
        else:
            s_ref[chunk(c), :] = acc
            mn = jnp.minimum(mn, _fold_rows(acc, jnp.min))
            mx = jnp.maximum(mx, _fold_rows(acc, jnp.max))
        return mn, mx

    def score_pair(c, carry, last):
        raw_a, raw_b = raw_scores(c), raw_scores(c + 1)
        return score_chunk(c + 1, raw_b, *score_chunk(c, raw_a, *carry, last=False), last=last)

    carry = lax.fori_loop(0, (nch - 1) // 2, lambda i, carry: score_pair(2 * i, carry, last=False),
                          (jnp.full((64, qw), BIG, F32), jnp.full((64, qw), NEG, F32)))
    mn, mx = lax.cond((nch - 1) % 2 == 1,
                      lambda st: score_pair(nch - 2, st, last=True),
                      lambda st: score_chunk(nch - 1, raw_scores(nch - 1), *st, last=True), carry)
    lo, hi = _reduce_rows(mn, jnp.min), _reduce_rows(mx, jnp.max)

    def count_gt(thr):
        def body(c, gt):
            return gt + _fold_rows(jnp.where(s_ref[chunk(c), :] > thr, 1.0, 0.0), jnp.sum)
        return _reduce_rows(lax.fori_loop(0, nch, body, jnp.zeros((64, qw), F32)), jnp.sum)

    def largest_upto(bound):
        def body(c, mx):
            s = s_ref[chunk(c), :]
            return jnp.maximum(mx, _fold_rows(jnp.where(s <= bound, s, NEG), jnp.max))
        return _reduce_rows(lax.fori_loop(0, nch, body, jnp.full((64, qw), NEG, F32)), jnp.max)

    def next_below_and_multiplicity(cand):
        def body(c, carry):
            mx, eq = carry
            s = s_ref[chunk(c), :]
            mx = jnp.maximum(mx, _fold_rows(jnp.where(s < cand, s, NEG), jnp.max))
            eq = eq + _fold_rows(jnp.where(s == cand, 1.0, 0.0), jnp.sum)
            return mx, eq
        mx, eq = lax.fori_loop(0, nch, body, (jnp.full((64, qw), NEG, F32), jnp.zeros((64, qw), F32)))
        return _reduce_rows(mx, jnp.max), _reduce_rows(eq, jnp.sum)

    kf = float(k_sel)
    need = qpos >= k_sel

    def bisect(_, carry):
        lo, hi, above = carry
        mid = lo + (hi - lo) * 0.5
        cnt = count_gt(mid)
        below = cnt < kf
        return jnp.where(below, lo, mid), jnp.where(below, mid, hi), jnp.where(below, cnt, above)

    _, hi, above = lax.fori_loop(0, N_BISECT, bisect, (lo - 1.0, hi, jnp.zeros((1, qw), F32)))

    def walk_cond(carry):
        _, _, ge = carry
        return jnp.max(jnp.where(need & (ge < kf), 1.0, 0.0)) > 0.0

    def walk_body(carry):
        cand, gt, ge = carry
        active = ge < kf
        nxt, mult = next_below_and_multiplicity(cand)
        ge_new = gt + mult
        moved = active & (ge_new < kf)
        return jnp.where(moved, nxt, cand), jnp.where(moved, ge_new, gt), jnp.where(active, ge_new, ge)

    cand, gt, ge = lax.while_loop(walk_cond, walk_body, (largest_upto(hi), above, above))
    thr = jnp.where(need, cand, THR_ALL)
    want_eq = jnp.where(need, kf - gt, 0.0)

    any_tie = jnp.max(jnp.where(need & (ge > kf), 1.0, 0.0)) > 0.0

    def mark_with_ties(c, eq_seen):
        s = s_ref[chunk(c), :]
        eq = jnp.where(s == thr, 1.0, 0.0)
        rank = eq_seen + jnp.dot(tri_ref[...], eq.astype(BF16), preferred_element_type=F32)
        keep = jnp.where(s > thr, 1.0, jnp.where(rank <= want_eq, eq, 0.0))
        s_ref[chunk(c), :] = jnp.where(keep > 0.5, 0.0, NEG)
        return eq_seen + _reduce_rows(eq, jnp.sum)

    def mark_no_ties(c, carry):
        s_ref[chunk(c), :] = jnp.where(s_ref[chunk(c), :] >= thr, 0.0, NEG)
        return carry

    @pl.when(any_tie)
    def _():
        lax.fori_loop(0, nch, mark_with_ties, jnp.zeros((1, qw), F32))

    @pl.when(jnp.logical_not(any_tie))
    def _():
        lax.fori_loop(0, nch, mark_no_ties, 0)

    def masked_logits(c):
        sel = s_ref[chunk(c), :]
        bias = jnp.concatenate(
            [jnp.concatenate([jnp.concatenate([_bias_tile(tab_ref, h, qi * nqb + iq, c * nkb + jk)
                                               for iq in range(nqb)], axis=1) for jk in range(nkb)], axis=0) + sel
             for h in range(C_HEADS)], axis=1)
        return jnp.dot(k_ref[chunk(c), :], cq_all, preferred_element_type=F32) + bias

    ones_rows = jnp.ones((ONES_ROWS, tk), BF16)

    def weigh(c, p):
        return jnp.dot(jnp.concatenate([vt_ref[c], ones_rows], axis=0), p.astype(BF16), preferred_element_type=F32)

    def exact_update(c, lg, m, acc):
        m_new = jnp.maximum(m, _reduce_rows(lg, jnp.max))
        alpha = jnp.exp2(m - m_new)
        return m_new, alpha * acc + weigh(c, jnp.exp2(lg - m_new))

    def lagged_update(c, lg, m, acc, jump):
        p = jnp.exp2(lg - m)
        top = _reduce_rows(lg, jnp.max)
        acc = acc + weigh(c, p)
        m_new = jnp.maximum(m, top)
        alpha = jnp.exp2(m - m_new)
        return m_new, alpha * acc, jnp.maximum(jump, top - m)

    wide = C_HEADS * qw
    init = (jnp.full((1, wide), M_INIT, F32), jnp.zeros((HEAD_DIM + ONES_ROWS, wide), F32))

    def opening(count):
        lgs = [masked_logits(c) for c in range(count)]
        state = exact_update(0, lgs[0], *init) + (jnp.zeros((1, wide), F32),)
        return lagged_update(1, lgs[1], *state) if count == 2 else state

    def lagged_pair(c, carry):
        lg_a, lg_b = masked_logits(c), masked_logits(c + 1)
        return lagged_update(c + 1, lg_b, *lagged_update(c, lg_a, *carry))

    odd = nch % 2
    state = lax.cond(odd == 1, lambda: opening(1), lambda: opening(2))
    _, acc, jump = lax.fori_loop(0, (nch - 2 + odd) // 2,
                                 lambda i, carry: lagged_pair(2 - odd + 2 * i, carry), state)
    acc = lax.cond(jnp.max(jump) > MAX_LAG,
                   lambda: lax.fori_loop(0, nch, lambda c, st: exact_update(c, masked_logits(c), *st), init)[1],
                   lambda: acc)
    o = acc[:HEAD_DIM] / acc[HEAD_DIM:HEAD_DIM + 1]
    ot = jnp.concatenate([o[:, h * qw:(h + 1) * qw] for h in range(C_HEADS)], axis=0)
    o_ref[...] = ot.T.astype(BF16)


def _dsa_attention(iqt, iwt, cqt, ck, cvt, tab, batch, t):
    tk = TOKEN_TILE
    qw = C_Q_TILE
    per = tk // qw
    nkt = t // tk
    k_sel = min(TOPK_MAX, t // 4)
    iqt = iqt.reshape(batch, nkt, IDX_HEADS * IDX_DIM, tk)
    iwt = iwt.reshape(batch, nkt, 16, tk)
    cqt = cqt.reshape(batch, nkt, C_OUT, tk)
    cvt = cvt.reshape(batch, nkt, HEAD_DIM, tk)
    ck = ck.reshape(batch, t, LANES)
    r = np.arange(tk)
    tri = jnp.asarray(r[:, None] >= r[None, :], BF16)
    qblock = lambda rows: pl.BlockSpec((None, None, rows, qw), lambda b, i: (b, i // per, 0, i % per))
    return pl.pallas_call(
        functools.partial(_dsa_kernel, k_sel=k_sel),
        out_shape=jax.ShapeDtypeStruct((batch, t, C_OUT), BF16),
        grid=(batch, t // qw),
        in_specs=[qblock(IDX_HEADS * IDX_DIM), qblock(16), qblock(C_OUT),
                  pl.BlockSpec((None, t, LANES), lambda b, i: (b, 0, 0)),
                  pl.BlockSpec((None, nkt, HEAD_DIM, tk), lambda b, i: (b, 0, 0, 0)),
                  _const_spec(tab.shape), _const_spec(tri.shape)],
        out_specs=pl.BlockSpec((None, qw, C_OUT), lambda b, i: (b, i, 0)),
        scratch_shapes=[pltpu.VMEM((t, qw), F32)],
        compiler_params=_cparams(("parallel", "arbitrary")),
        name="dsa_attention",
    )(iqt, iwt, cqt, ck, cvt, tab, tri).reshape(batch * t, C_OUT)


def _merge_kernel(x_ref, *refs):
    ng = len(DIL_GROUPS)
    a_refs = refs[:2 * ng]
    ob_ref, oc_ref, gate_ref, wa_ref, wb_ref, wc_ref, wo_ref, out_ref = refs[2 * ng:-1]
    shuffle_ref = refs[-1]
    tm, d = x_ref.shape

    def token_order(ref, slot):
        dil = ref.shape[0]
        if dil == 1:
            return ref[0].astype(F32)
        halves = range(A_OUT // LANES)
        for r in range(dil):
            for half in halves:
                shuffle_ref[slot, half, pl.ds(r, tm // dil, stride=dil), :] = (
                    ref[r, :, half * LANES:(half + 1) * LANES].astype(F32))
        return jnp.concatenate([shuffle_ref[slot, half] for half in halves], axis=1)

    outs = [token_order(a_refs[2 * g], 2 * g) for g in range(ng)]
    lses = [token_order(a_refs[2 * g + 1], 2 * g + 1) for g in range(ng)]
    top = functools.reduce(jnp.maximum, lses)
    es = [jnp.exp(lse - top) for lse in lses]
    num = sum(e * o for e, o in zip(es, outs))
    oa = (num / sum(es)).astype(BF16)
    y = gate_ref[:, 0:d].astype(F32) * jnp.dot(oa, wa_ref[...], preferred_element_type=F32)
    y = y + gate_ref[:, d:2 * d].astype(F32) * jnp.dot(ob_ref[...], wb_ref[...], preferred_element_type=F32)
    y = y + gate_ref[:, 2 * d:3 * d].astype(F32) * jnp.dot(oc_ref[...], wc_ref[...], preferred_element_type=F32)
    out_ref[...] = x_ref[...] + jnp.dot(y.astype(BF16), wo_ref[...], preferred_element_type=F32)


def _merge(x, a_parts, ob, oc, gates, wa, wb, wc, wo, t):
    n, d = x.shape
    tm = FFN_TILE
    per_batch = t // tm
    tok = lambda c: pl.BlockSpec((tm, c), lambda i: (i, 0))
    by_residue = lambda dil: pl.BlockSpec((None, dil, tm // dil, A_OUT),
                                          lambda i: (i // per_batch, 0, i % per_batch, 0))
    ws = [w.astype(BF16) for w in (wa, wb, wc, wo)]
    return pl.pallas_call(
        _merge_kernel,
        out_shape=jax.ShapeDtypeStruct((n, d), F32),
        grid=(n // tm,),
        in_specs=[tok(d)] + [by_residue(z.shape[1]) for z in a_parts] + [tok(B_OUT), tok(C_OUT), tok(3 * d)]
                 + [_weight_spec(w.shape) for w in ws],
        out_specs=tok(d),
        scratch_shapes=[pltpu.VMEM((len(a_parts), A_OUT // LANES, tm, LANES), F32)],
        compiler_params=_cparams(("parallel",)),
        name="merge",
    )(x, *a_parts, ob, oc, gates, *ws)


def _token_mixer(x, batch, t, layer, mix_norm, w_in, qk_gain, diff_lambda, diff_out_norm,
                 w_branch_a, w_branch_b, w_branch_c, w_out, band_tabs, tab_b, tab_c):
    ng = len(DIL_GROUPS)
    outs = _project(x, mix_norm, w_in, qk_gain, batch, t)
    a_in, (bk, ck, gates, bqt, bvt, cqt, cvt, iqt, iwt) = outs[:3 * ng], outs[3 * ng:]
    a_parts = []
    for g, (_, dilation) in enumerate(DIL_GROUPS):
        a_parts += _dilated_group(a_in[g], a_in[ng + g], a_in[2 * ng + g], band_tabs[g], dilation)
    lam_init = 0.8 - 0.6 * np.exp(-0.3 * layer)
    lam_rows = jnp.concatenate([diff_lambda.astype(F32), jnp.full((4, HEAD_DIM), lam_init, F32)], axis=0)
    gn = jnp.broadcast_to((diff_out_norm.astype(F32) * (1.0 - lam_init))[:, None], (B_V_DIM, B_Q_TILE))
    ob = _diff_attention(bqt, bk, bvt, tab_b, lam_rows, gn, batch, t)
    oc = _dsa_attention(iqt, iwt, cqt, ck, cvt, tab_c, batch, t)
    return _merge(x, a_parts, ob, oc, gates, w_branch_a, w_branch_b, w_branch_c, w_out, t)


def kernel(x, rel_bias, ffn1_norm, ffn1_w_gate, ffn1_w_up, ffn1_w_down, mix_norm, w_in, qk_gain,
           diff_lambda, diff_out_norm, w_branch_a, w_branch_b, w_branch_c, w_out,
           ffn2_norm, ffn2_w_gate, ffn2_w_up, ffn2_w_down):
    batch, t, d = x.shape
    depth = w_in.shape[0]
    assert t % (DIL_GROUPS[-1][1] * LANES) == 0 and t % TOKEN_TILE == 0
    band_tabs = [_band_tables(rel_bias[:, g * A_GROUP_HEADS:(g + 1) * A_GROUP_HEADS], dil)
                 for g, (_, dil) in enumerate(DIL_GROUPS)]
    tab_b = _toeplitz_tables(rel_bias[:, A_HEADS:A_HEADS + B_HEADS] * LOG2E)
    tab_c = _toeplitz_tables(rel_bias[:, A_HEADS + B_HEADS:] * LOG2E)
    h = x.reshape(batch * t, d).astype(F32)
    for i in range(depth):
        h = _ffn(h, ffn1_norm[i], ffn1_w_gate[i], ffn1_w_up[i], ffn1_w_down[i])
        h = _token_mixer(h, batch, t, i, mix_norm[i], w_in[i], qk_gain[i], diff_lambda[i], diff_out_norm[i],
                         w_branch_a[i], w_branch_b[i], w_branch_c[i], w_out[i], band_tabs, tab_b, tab_c)
        h = _ffn(h, ffn2_norm[i], ffn2_w_gate[i], ffn2_w_up[i], ffn2_w_down[i])
    return h.reshape(batch, t, d).astype(x.dtype)
```
